```python
import jax
import jax.numpy as jnp
from jax import lax
import numpy as np


D_MODEL = 1024
BATCH = 8
SEQ = 4096
DEPTH = 1

NORM_EPS = 1e-6
RWKV_HEAD = 64
RWKV_HEADS = 8
RWKV_DIM = RWKV_HEADS * RWKV_HEAD
DECAY_LORA = 64
ICLR_LORA = 64
GATE_LORA = 160
GN_EPS = 64e-5
MLA_HEADS = 8
Q_LORA = 256
KV_LORA = 128
NOPE_DIM = 64
ROPE_DIM = 32
V_DIM = 64
ROPE_THETA = 10000.0
Q_BLOCK = 128
D_FF = 4 * D_MODEL

RWKV_SIZES = (RWKV_DIM, RWKV_DIM, RWKV_DIM, DECAY_LORA, ICLR_LORA, GATE_LORA)
RWKV_COLS = 3 * RWKV_DIM + DECAY_LORA + ICLR_LORA + GATE_LORA
IN_SIZES = (RWKV_COLS, Q_LORA, KV_LORA, ROPE_DIM, D_MODEL, D_MODEL)
IN_COLS = RWKV_COLS + Q_LORA + KV_LORA + ROPE_DIM + 2 * D_MODEL

kernel_name = 'hybrid_rwkv7_mla_gated_block'


def _split(z, sizes):
    idx = []
    acc = 0
    for s in sizes[:-1]:
        acc += s
        idx.append(acc)
    return jnp.split(z, idx, axis=-1)


def rms_norm(x, g, eps=NORM_EPS):
    xf = x.astype(jnp.float32)
    y = xf * lax.rsqrt(jnp.mean(xf * xf, axis=-1, keepdims=True) + eps)
    return (y * g.astype(jnp.float32)).astype(x.dtype)


def token_shift(z, mu):
    prev = jnp.pad(z, ((0, 0), (1, 0), (0, 0)))[:, :-1]
    return z + (prev - z) * mu


def apply_rope(x, cos, sin):
    half = x.shape[-1] // 2
    x1, x2 = x[..., :half], x[..., half:]
    return jnp.concatenate([x1 * cos - x2 * sin, x2 * cos + x1 * sin], axis=-1)


def rwkv7_scan(r, decay, k, v, a, b):
    def step(state, inp):
        r_t, w_t, k_t, v_t, a_t, b_t = inp
        sa = jnp.einsum('bhvk,bhk->bhv', state, a_t)
        state = (state * w_t[:, :, None, :] + sa[..., None] * b_t[:, :, None, :]
                 + v_t[..., None] * k_t[:, :, None, :])
        y_t = jnp.einsum('bhvk,bhk->bhv', state, r_t)
        return state, y_t
    B, S, H, N = r.shape
    xs = tuple(jnp.moveaxis(t, 1, 0) for t in (r, decay, k, v, a, b))
    s0 = jnp.zeros((B, H, N, N), jnp.float32)
    _, ys = lax.scan(step, s0, xs)
    return jnp.moveaxis(ys, 0, 1)


def rwkv7_branch(zr, zk, zv, zw, za, zg, w0, w_up, a0, a_up, g_up, k_k, k_a, r_k,
                 ln_w, ln_b, w_o):
    f32 = jnp.float32
    B, S, _ = zr.shape
    heads = lambda t: t.reshape(B, S, RWKV_HEADS, RWKV_HEAD)
    w_pre = (w0 + jnp.tanh(zw) @ w_up).astype(f32)
    decay = jnp.exp(-jnp.exp(-jax.nn.softplus(-w_pre) - 0.5))
    a = jax.nn.sigmoid((a0 + za @ a_up).astype(f32))
    g = jax.nn.sigmoid(zg) @ g_up
    kk = heads((zk * k_k).astype(f32))
    kk = kk / jnp.maximum(jnp.linalg.norm(kk, axis=-1, keepdims=True), 1e-12)
    k = zk.astype(f32) * (1.0 + (a - 1.0) * k_a.astype(f32))
    r = heads(zr.astype(f32))
    k = heads(k)
    v = heads(zv.astype(f32))
    a = heads(a)
    y = rwkv7_scan(r, heads(decay), k, v, -kk, kk * a)
    mean = jnp.mean(y, axis=-1, keepdims=True)
    var = jnp.mean(jnp.square(y - mean), axis=-1, keepdims=True)
    y = (y - mean) * lax.rsqrt(var + GN_EPS)
    y = y.reshape(B, S, RWKV_DIM) * ln_w.astype(f32) + ln_b.astype(f32)
    bonus = jnp.sum(r * k * r_k.astype(f32), axis=-1, keepdims=True) * v
    y = (y + bonus.reshape(B, S, RWKV_DIM)) * g.astype(f32)
    return y.astype(zr.dtype) @ w_o


def mla_branch(c_q, c_kv, k_rope, cos, sin, q_norm, w_uq, kv_norm, w_ukv, w_o):
    B, S, _ = c_q.shape
    q = (rms_norm(c_q, q_norm) @ w_uq).reshape(B, S, MLA_HEADS, NOPE_DIM + ROPE_DIM)
    q_nope = q[..., :NOPE_DIM]
    q_rope = apply_rope(q[..., NOPE_DIM:], cos[:, :, None, :], sin[:, :, None, :])
    kv = (rms_norm(c_kv, kv_norm) @ w_ukv).reshape(B, S, MLA_HEADS, NOPE_DIM + V_DIM)
    k_nope = kv[..., :NOPE_DIM]
    v = kv[..., NOPE_DIM:]
    k_rope = apply_rope(k_rope, cos, sin)
    nb = S // Q_BLOCK
    qn_blocks = q_nope.reshape(B, nb, Q_BLOCK, MLA_HEADS, NOPE_DIM).transpose(1, 0, 3, 2, 4)
    qr_blocks = q_rope.reshape(B, nb, Q_BLOCK, MLA_HEADS, ROPE_DIM).transpose(1, 0, 3, 2, 4)
    kn = k_nope.transpose(0, 2, 1, 3)
    vh = v.transpose(0, 2, 1, 3)
    key_idx = jnp.arange(S)
    scale = (NOPE_DIM + ROPE_DIM) ** -0.5
    neg = jnp.finfo(jnp.float32).min

    def attend(args):
        qn_b, qr_b, blk = args
        s = (jnp.einsum('bhqd,bhkd->bhqk', qn_b, kn)
             + jnp.einsum('bhqr,bkr->bhqk', qr_b, k_rope))
        s = s.astype(jnp.float32) * scale
        q_idx = blk * Q_BLOCK + jnp.arange(Q_BLOCK)
        s = jnp.where(key_idx[None, :] <= q_idx[:, None], s, neg)
        p = jax.nn.softmax(s, axis=-1)
        return jnp.einsum('bhqk,bhkd->bhqd', p.astype(vh.dtype), vh)

    o = lax.map(attend, (qn_blocks, qr_blocks, jnp.arange(nb)))
    o = o.transpose(1, 0, 3, 2, 4).reshape(B, S, MLA_HEADS * V_DIM)
    return o @ w_o


def setup_inputs(seed: int = 0) -> dict:
    key = jax.random.key(seed)
    ks = jax.random.split(key, 32)
    f32 = jnp.float32
    L = DEPTH

    def nrm(k, shape, scale):
        return jax.random.normal(k, shape, f32) * scale

    def gain(k, shape):
        return 1.0 + 0.05 * jax.random.normal(k, shape, f32)

    return {
        'x': jax.random.normal(ks[0], (BATCH, SEQ, D_MODEL), f32),
        'positions': jnp.broadcast_to(jnp.arange(SEQ, dtype=jnp.int32)[None, :], (BATCH, SEQ)),
        'norm_mix': gain(ks[1], (L, D_MODEL)),
        'w_in': nrm(ks[2], (L, D_MODEL, IN_COLS), D_MODEL ** -0.5),
        'mu_shift': jax.random.uniform(ks[3], (L, RWKV_COLS), f32, 0.0, 1.0),
        'w0': jax.random.uniform(ks[4], (L, RWKV_DIM), f32, -6.0, 1.0),
        'w_up': nrm(ks[5], (L, DECAY_LORA, RWKV_DIM), 0.5 * DECAY_LORA ** -0.5),
        'a0': nrm(ks[6], (L, RWKV_DIM), 0.5),
        'a_up': nrm(ks[7], (L, ICLR_LORA, RWKV_DIM), 0.5 * ICLR_LORA ** -0.5),
        'g_up': nrm(ks[8], (L, GATE_LORA, RWKV_DIM), GATE_LORA ** -0.5),
        'k_k': 0.85 + 0.05 * jax.random.normal(ks[9], (L, RWKV_DIM), f32),
        'k_a': gain(ks[10], (L, RWKV_DIM)),
        'r_k': nrm(ks[11], (L, RWKV_HEADS, RWKV_HEAD), 0.1),
        'ln_w': gain(ks[12], (L, RWKV_DIM)),
        'ln_b': nrm(ks[13], (L, RWKV_DIM), 0.02),
        'w_o_rwkv': nrm(ks[14], (L, RWKV_DIM, D_MODEL), RWKV_DIM ** -0.5),
        'q_norm': gain(ks[15], (L, Q_LORA)),
        'w_uq': nrm(ks[16], (L, Q_LORA, MLA_HEADS * (NOPE_DIM + ROPE_DIM)), Q_LORA ** -0.5),
        'kv_norm': gain(ks[17], (L, KV_LORA)),
        'w_ukv': nrm(ks[18], (L, KV_LORA, MLA_HEADS * (NOPE_DIM + V_DIM)), KV_LORA ** -0.5),
        'w_o_mla': nrm(ks[19], (L, MLA_HEADS * V_DIM, D_MODEL), (MLA_HEADS * V_DIM) ** -0.5),
        'w_out': nrm(ks[20], (L, D_MODEL, D_MODEL), D_MODEL ** -0.5),
        'norm_ffn': gain(ks[21], (L, D_MODEL)),
        'w_ff_up': nrm(ks[22], (L, D_MODEL, D_FF), D_MODEL ** -0.5),
        'w_ff_down': nrm(ks[23], (L, D_FF, D_MODEL), D_FF ** -0.5),
        'norm_final': gain(ks[24], (D_MODEL,)),
    }


def reference(x, positions, norm_mix, w_in, mu_shift, w0, w_up, a0, a_up, g_up, k_k, k_a,
              r_k, ln_w, ln_b, w_o_rwkv, q_norm, w_uq, kv_norm, w_ukv, w_o_mla, w_out,
              norm_ffn, w_ff_up, w_ff_down, norm_final):
    half = ROPE_DIM // 2
    inv_freq = 1.0 / (ROPE_THETA ** (jnp.arange(half, dtype=jnp.float32) * (2.0 / ROPE_DIM)))
    ang = positions.astype(jnp.float32)[..., None] * inv_freq
    cos = jnp.cos(ang).astype(x.dtype)
    sin = jnp.sin(ang).astype(x.dtype)
    h = x
    for l in range(DEPTH):
        u = rms_norm(h, norm_mix[l])
        proj = u @ w_in[l]
        z_rwkv, c_q, c_kv, k_rope, gate_a, gate_b = _split(proj, IN_SIZES)
        z_rwkv = token_shift(z_rwkv, mu_shift[l])
        zr, zk, zv, zw, za, zg = _split(z_rwkv, RWKV_SIZES)
        y_a = rwkv7_branch(zr, zk, zv, zw, za, zg, w0[l], w_up[l], a0[l], a_up[l], g_up[l],
                           k_k[l], k_a[l], r_k[l], ln_w[l], ln_b[l], w_o_rwkv[l])
        y_b = mla_branch(c_q, c_kv, k_rope, cos, sin, q_norm[l], w_uq[l], kv_norm[l],
                         w_ukv[l], w_o_mla[l])
        merged = jax.nn.sigmoid(gate_a) * y_a + jax.nn.sigmoid(gate_b) * y_b
        h = h + merged @ w_out[l]
        f = rms_norm(h, norm_ffn[l]) @ w_ff_up[l]
        h = h + jnp.square(jax.nn.relu(f)) @ w_ff_down[l]
    return rms_norm(h, norm_final)
```

```python
import functools
import math

import jax
import jax.numpy as jnp
from jax import lax
from jax.experimental import pallas as pl
from jax.experimental.pallas import tpu as pltpu

F32 = jnp.float32
BF16 = jnp.bfloat16

D_MODEL = 1024
NORM_EPS = 1e-6
RWKV_HEAD = 64
RWKV_HEADS = 8
RWKV_DIM = RWKV_HEADS * RWKV_HEAD
DECAY_LORA = 64
ICLR_LORA = 64
GATE_LORA = 160
GN_EPS = 64e-5
MLA_HEADS = 8
Q_LORA = 256
KV_LORA = 128
NOPE_DIM = 64
ROPE_DIM = 32
V_DIM = 64
ROPE_THETA = 10000.0
D_FF = 4 * D_MODEL

LANE = 128
CHUNK = 64
PAIR = 2 * RWKV_HEAD
HEAD_PAD = 128

RW_COLS = 3 * RWKV_DIM + 128 + 128 + 256
ML_COLS = Q_LORA + KV_LORA + 128 + 128
GT_COLS = 2 * D_MODEL
VMEM_LIMIT = 56 * 1024 * 1024


def _cparams(sem):
    return pltpu.CompilerParams(dimension_semantics=sem, vmem_limit_bytes=VMEM_LIMIT)


def _sigmoid(x):
    return 1.0 / (1.0 + jnp.exp(-x))


def _rms(x, g):
    ms = jnp.mean(x * x, axis=-1, keepdims=True)
    return x * lax.rsqrt(ms + NORM_EPS) * g


def _dot(a, b):
    return jnp.dot(a, b, preferred_element_type=F32)


def _dot_nt(a, b):
    return lax.dot_general(a, b, (((1,), (1,)), ((), ())), preferred_element_type=F32)


def _dot_tn(a, b):
    return lax.dot_general(a, b, (((0,), (0,)), ((), ())), preferred_element_type=F32)


def _inproj_kernel(x_ref, g_ref, w_ref, mu_ref, rw_ref, ml_ref, gt_ref, carry_ref,
                   *, tiles_per_seq, tm):
    i = pl.program_id(0)
    u = _rms(x_ref[...], g_ref[...]).astype(BF16)

    @pl.when(i % tiles_per_seq == 0)
    def _():
        carry_ref[...] = jnp.zeros_like(carry_ref)

    cw = 512
    row0 = lax.broadcasted_iota(jnp.int32, (tm, cw), 0) == 0
    for c in range(RW_COLS // cw):
        cs = slice(c * cw, (c + 1) * cw)
        z = _dot(u, w_ref[:, cs])
        prev = pltpu.roll(z, 1, 0)
        prev = jnp.where(row0, carry_ref[7:8, cs], prev)
        carry_ref[:, cs] = z[tm - 8:tm, :]
        rw_ref[:, cs] = z + (prev - z) * mu_ref[:, cs]
    ml_ref[...] = _dot(u, w_ref[:, RW_COLS:RW_COLS + ML_COLS])
    for c in range(GT_COLS // cw):
        gt_ref[:, c * cw:(c + 1) * cw] = _dot(
            u, w_ref[:, RW_COLS + ML_COLS + c * cw:RW_COLS + ML_COLS + (c + 1) * cw])


def _inproj(x2, g, w_a, mu_a, seq):
    t_tok = x2.shape[0]
    tm = min(512, seq)
    ncol = RW_COLS + ML_COLS + GT_COLS
    kern = functools.partial(_inproj_kernel, tiles_per_seq=seq // tm, tm=tm)
    return pl.pallas_call(
        kern,
        grid=(t_tok // tm,),
        in_specs=[
            pl.BlockSpec((tm, D_MODEL), lambda i: (i, 0)),
            pl.BlockSpec((1, D_MODEL), lambda i: (0, 0)),
            pl.BlockSpec((D_MODEL, ncol), lambda i: (0, 0)),
            pl.BlockSpec((1, RW_COLS), lambda i: (0, 0)),
        ],
        out_specs=[
            pl.BlockSpec((tm, RW_COLS), lambda i: (i, 0)),
            pl.BlockSpec((tm, ML_COLS), lambda i: (i, 0)),
            pl.BlockSpec((tm, GT_COLS), lambda i: (i, 0)),
        ],
        out_shape=[
            jax.ShapeDtypeStruct((t_tok, RW_COLS), F32),
            jax.ShapeDtypeStruct((t_tok, ML_COLS), F32),
            jax.ShapeDtypeStruct((t_tok, GT_COLS), F32),
        ],
        scratch_shapes=[pltpu.VMEM((8, RW_COLS), F32)],
        compiler_params=_cparams(("arbitrary",)),
        name="inproj",
    )(x2, g, w_a, mu_a)


def _split3(x):
    hi = x.astype(BF16)
    r1 = x - hi.astype(F32)
    mid = r1.astype(BF16)
    lo = (r1 - mid.astype(F32)).astype(BF16)
    return hi, mid, lo


def _mm(a, b):
    return jnp.dot(a, b, preferred_element_type=F32, precision=lax.Precision.HIGHEST)


def _mm_nt(a, b):
    return lax.dot_general(a, b, (((1,), (1,)), ((), ())), preferred_element_type=F32,
                           precision=lax.Precision.HIGHEST)


def _mm_tn(a, b):
    return lax.dot_general(a, b, (((0,), (0,)), ((), ())), preferred_element_type=F32,
                           precision=lax.Precision.HIGHEST)


def _rwkv_masks():
    n = 2 * CHUNK
    row = lax.broadcasted_iota(jnp.int32, (n, n), 0)
    col = lax.broadcasted_iota(jnp.int32, (n, n), 1)
    same = (row // CHUNK) == (col // CHUNK)
    strict = jnp.where(same & (row > col), 1.0, 0.0).astype(F32)
    incl = jnp.where(same & (row >= col), 1.0, 0.0).astype(F32)
    m16 = jnp.where((row // 16) == (col // 16), 1.0, 0.0).astype(F32)
    m32 = jnp.where(((row // 32) == (col // 32)) & ((row // 16) > (col // 16)), 1.0, 0.0).astype(F32)
    m64 = jnp.where((row // 32) > (col // 32), 1.0, 0.0).astype(F32)
    eye = jnp.where(row == col, 1.0, 0.0).astype(F32)
    headsel = (row // CHUNK) == (col // RWKV_HEAD)
    return strict, incl, m16, m32, m64, eye, headsel


def _unit_lower_inverse(a, m16, m32, m64, eye):
    d = a * m16
    d2 = _mm(d, d)
    d4 = _mm(d2, d2)
    d8 = _mm(d4, d4)
    t = eye + d
    t = t + _mm(t, d2)
    t = t + _mm(t, d4)
    t = t + _mm(t, d8)
    t = t + _mm(_mm(t, a * m32), t)
    t = t + _mm(_mm(t, a * m64), t)
    return t


def _rwkv_chunk(r, k, v, a, b, lw, h2, masks, tril):
    strict, incl, m16, m32, m64, eye, headsel = masks
    hi, mid, lo = _split3(lw)
    cs = _dot(tril, hi) + _dot(tril, mid) + _dot(tril, lo)
    g_in = jnp.exp(cs)
    g_ex = jnp.exp(cs - lw)
    g_inv = jnp.exp(-cs)
    g_last = jnp.exp(cs[CHUNK - 1:CHUNK, :])

    def stack(x):
        return jnp.where(headsel, jnp.concatenate([x, x], axis=0), 0.0)

    rt = stack(r * g_in)
    at = stack(a * g_ex)
    bt = stack(b * g_inv)
    kt = stack(k * g_inv)
    bh = bt * g_last
    kh = kt * g_last
    v2 = stack(v)

    a_ab = _mm_nt(at, bt) * strict
    a_ak = _mm_nt(at, kt) * strict
    a_rb = _mm_nt(rt, bt) * incl
    a_rk = _mm_nt(rt, kt) * incl
    t = _unit_lower_inverse(a_ab, m16, m32, m64, eye)
    w = _mm(t, at)
    u0 = _mm(t, _mm(a_ak, v2))
    qeff = rt + _mm(a_rb, w)
    y0 = _mm(a_rb, u0) + _mm(a_rk, v2)
    m = eye * g_last + _mm_tn(bh, w)
    n = _mm_tn(bh, u0) + _mm_tn(kh, v2)
    y2 = _mm(qeff, h2) + y0
    h_new = _mm(m, h2) + n
    y = y2[:CHUNK, :] + y2[CHUNK:, :]
    return y, h_new


def _rwkv_kernel(zr_ref, zk_ref, zv_ref, zw_ref, za_ref, zg_ref,
                 w0_ref, a0_ref, kk_ref, ka_ref, rk_ref, lnw_ref, lnb_ref,
                 wup_ref, aup_ref, gup_ref, y_ref, h_ref, *, tb):
    @pl.when(pl.program_id(2) == 0)
    def _():
        h_ref[...] = jnp.zeros_like(h_ref)

    lane = lax.broadcasted_iota(jnp.int32, (tb, PAIR), 1)
    head0 = lane < RWKV_HEAD

    def head_sum(x):
        s0 = jnp.sum(jnp.where(head0, x, 0.0), axis=-1, keepdims=True)
        s1 = jnp.sum(jnp.where(head0, 0.0, x), axis=-1, keepdims=True)
        return jnp.where(head0, s0, s1)

    zr = zr_ref[...]
    zk = zk_ref[...]
    zv = zv_ref[...]
    w_pre = w0_ref[...] + _dot(jnp.tanh(zw_ref[...]).astype(BF16), wup_ref[...])
    lw = -math.exp(-0.5) * _sigmoid(w_pre)
    iclr = _sigmoid(a0_ref[...] + _dot(za_ref[...].astype(BF16), aup_ref[...]))
    gate = _dot(_sigmoid(zg_ref[...]).astype(BF16), gup_ref[...])
    kk = zk * kk_ref[...]
    kk = kk / jnp.maximum(jnp.sqrt(head_sum(kk * kk)), 1e-12)
    k = zk * (1.0 + (iclr - 1.0) * ka_ref[...])
    a = -kk
    b = kk * iclr

    masks = _rwkv_masks()
    trow = lax.broadcasted_iota(jnp.int32, (CHUNK, CHUNK), 0)
    tcol = lax.broadcasted_iota(jnp.int32, (CHUNK, CHUNK), 1)
    tril = jnp.where(trow >= tcol, 1.0, 0.0).astype(BF16)

    h2 = h_ref[...]
    ys = []
    for c in range(tb // CHUNK):
        s = slice(c * CHUNK, (c + 1) * CHUNK)
        y, h2 = _rwkv_chunk(zr[s], k[s], zv[s], a[s], b[s], lw[s], h2, masks, tril)
        ys.append(y)
    h_ref[...] = h2
    y = jnp.concatenate(ys, axis=0)

    mean = head_sum(y) * (1.0 / RWKV_HEAD)
    yc = y - mean
    var = head_sum(yc * yc) * (1.0 / RWKV_HEAD)
    yn = yc * lax.rsqrt(var + GN_EPS) * lnw_ref[...] + lnb_ref[...]
    bonus = head_sum(zr * k * rk_ref[...]) * zv
    y_ref[...] = ((yn + bonus) * gate).astype(y_ref.dtype)


def _rwkv(rw, prm, wup, aup, gup, batch, seq):
    t_tok = rw.shape[0]
    tb = min(256, seq)
    nt = seq // tb
    npair = RWKV_DIM // PAIR

    def tok(col):
        return pl.BlockSpec((tb, PAIR), lambda b, p, t, col=col: (b * nt + t, col(p)))

    prm_spec = pl.BlockSpec((1, PAIR), lambda b, p, t: (0, p))
    in_specs = [
        tok(lambda p: p), tok(lambda p: npair + p), tok(lambda p: 2 * npair + p),
        tok(lambda p: 3 * npair), tok(lambda p: 3 * npair + 1),
        pl.BlockSpec((tb, 256), lambda b, p, t: (b * nt + t, (3 * RWKV_DIM + 256) // 256)),
    ] + [prm_spec] * 7 + [
        pl.BlockSpec((128, PAIR), lambda b, p, t: (0, p)),
        pl.BlockSpec((128, PAIR), lambda b, p, t: (0, p)),
        pl.BlockSpec((256, PAIR), lambda b, p, t: (0, p)),
    ]
    return pl.pallas_call(
        functools.partial(_rwkv_kernel, tb=tb),
        grid=(batch, npair, nt),
        in_specs=in_specs,
        out_specs=pl.BlockSpec((tb, PAIR), lambda b, p, t: (b * nt + t, p)),
        out_shape=jax.ShapeDtypeStruct((t_tok, RWKV_DIM), BF16),
        scratch_shapes=[pltpu.VMEM((PAIR, PAIR), F32)],
        compiler_params=_cparams(("arbitrary", "arbitrary", "arbitrary")),
        name="rwkv7",
    )(rw, rw, rw, rw, rw, rw, *prm, wup, aup, gup)


def _mla_prep_kernel(ml_ref, pos_ref, freq_ref, qn_ref, kvn_ref, wq_ref, wqr_ref, wk_ref, wvt_ref,
                     q_ref, k_ref, vt_ref, *, scale):
    ml = ml_ref[...]
    ang = pos_ref[...].astype(F32) * freq_ref[...]
    cosf = jnp.cos(ang)
    sinf = jnp.sin(ang)
    cq = _rms(ml[:, :Q_LORA], qn_ref[...]).astype(BF16)
    ckv = _rms(ml[:, Q_LORA:Q_LORA + KV_LORA], kvn_ref[...]).astype(BF16)
    kr = ml[:, Q_LORA + KV_LORA:Q_LORA + KV_LORA + 128]
    krr = ml[:, Q_LORA + KV_LORA + 128:Q_LORA + KV_LORA + 256]
    k_rope = kr * cosf + krr * sinf
    qf = _dot(cq, wq_ref[...])
    qr = _dot(cq, wqr_ref[...])
    kf = _dot(ckv, wk_ref[...])
    for h in range(MLA_HEADS):
        hs = slice(h * HEAD_PAD, (h + 1) * HEAD_PAD)
        q_ref[h] = ((qf[:, hs] * cosf + qr[:, hs] * sinf) * scale).astype(q_ref.dtype)
        k_ref[h] = (kf[:, hs] + k_rope).astype(k_ref.dtype)
        vt_ref[h] = _dot_nt(wvt_ref[h], ckv).astype(vt_ref.dtype)


def _mla_prep(ml, pos, freq, qn, kvn, wq, wqr, wk, wvt, batch, seq, tk):
    nk = seq // tk
    scale = (NOPE_DIM + ROPE_DIM) ** -0.5
    full = lambda shape: pl.BlockSpec(shape, lambda b, t: (0,) * len(shape))
    return pl.pallas_call(
        functools.partial(_mla_prep_kernel, scale=scale),
        grid=(batch, nk),
        in_specs=[
            pl.BlockSpec((tk, ML_COLS), lambda b, t: (b * nk + t, 0)),
            pl.BlockSpec((tk, 1), lambda b, t: (b * nk + t, 0)),
            full((1, HEAD_PAD)), full((1, Q_LORA)), full((1, KV_LORA)),
            full((Q_LORA, MLA_HEADS * HEAD_PAD)), full((Q_LORA, MLA_HEADS * HEAD_PAD)),
            full((KV_LORA, MLA_HEADS * HEAD_PAD)), full((MLA_HEADS, V_DIM, KV_LORA)),
        ],
        out_specs=[
            pl.BlockSpec((None, MLA_HEADS, None, tk, HEAD_PAD), lambda b, t: (b, 0, t, 0, 0)),
            pl.BlockSpec((None, MLA_HEADS, None, tk, HEAD_PAD), lambda b, t: (b, 0, t, 0, 0)),
            pl.BlockSpec((None, MLA_HEADS, None, V_DIM, tk), lambda b, t: (b, 0, t, 0, 0)),
        ],
        out_shape=[
            jax.ShapeDtypeStruct((batch, MLA_HEADS, nk, tk, HEAD_PAD), BF16),
            jax.ShapeDtypeStruct((batch, MLA_HEADS, nk, tk, HEAD_PAD), BF16),
            jax.ShapeDtypeStruct((batch, MLA_HEADS, nk, V_DIM, tk), BF16),
        ],
        compiler_params=_cparams(("arbitrary", "arbitrary")),
        name="mla_prep",
    )(ml, pos, freq, qn, kvn, wq, wqr, wk, wvt)


def _attn_kernel(q_ref, k_ref, vt_ref, o_ref, *, tk, heads):
    i = pl.program_id(2)
    krow = lax.broadcasted_iota(jnp.int32, (tk, tk), 0)
    qcol = lax.broadcasted_iota(jnp.int32, (tk, tk), 1)
    causal = krow <= qcol
    neg = jnp.finfo(F32).min
    outs = []
    for h in range(heads):
        q = q_ref[h, 0]

        def step(j, carry, masked):
            m, l, acc = carry
            s = _dot_nt(k_ref[h, j], q)
            if masked:
                s = jnp.where(causal, s, neg)
            m_new = jnp.maximum(m, jnp.max(s, axis=0, keepdims=True))
            alpha = jnp.exp(m - m_new)
            p = jnp.exp(s - m_new)
            l = alpha * l + jnp.sum(p, axis=0, keepdims=True)
            acc = alpha * acc + _dot(vt_ref[h, j], p.astype(BF16))
            return m_new, l, acc

        init = (jnp.full((1, tk), neg, F32), jnp.zeros((1, tk), F32), jnp.zeros((V_DIM, tk), F32))
        carry = lax.fori_loop(0, i, lambda j, c: step(j, c, False), init)
        m, l, acc = step(i, carry, True)
        outs.append((acc / l).T)
    o_ref[...] = jnp.concatenate(outs, axis=-1).astype(o_ref.dtype)


def _attn(q5, k5, vt5, batch, seq, tk):
    nk = seq // tk
    heads = 2
    ngrp = MLA_HEADS // heads
    return pl.pallas_call(
        functools.partial(_attn_kernel, tk=tk, heads=heads),
        grid=(batch, ngrp, nk),
        in_specs=[
            pl.BlockSpec((None, heads, 1, tk, HEAD_PAD), lambda b, g, i: (b, g, i, 0, 0)),
            pl.BlockSpec((None, heads, nk, tk, HEAD_PAD), lambda b, g, i: (b, g, 0, 0, 0)),
            pl.BlockSpec((None, heads, nk, V_DIM, tk), lambda b, g, i: (b, g, 0, 0, 0)),
        ],
        out_specs=pl.BlockSpec((tk, heads * V_DIM), lambda b, g, i: (b * nk + i, g)),
        out_shape=jax.ShapeDtypeStruct((batch * seq, MLA_HEADS * V_DIM), BF16),
        compiler_params=_cparams(("arbitrary", "arbitrary", "arbitrary")),
        name="mla_attn",
    )(q5, k5, vt5)


def _tail_kernel(x_ref, yr_ref, o_ref, gt_ref, wor_ref, wom_ref, wout_ref, nf_ref, wup_ref, wdn_ref,
                 nfin_ref, out_ref):
    y_a = _dot(yr_ref[...], wor_ref[...])
    y_b = _dot(o_ref[...], wom_ref[...])
    merged = _sigmoid(gt_ref[:, :D_MODEL]) * y_a + _sigmoid(gt_ref[:, D_MODEL:]) * y_b
    h = x_ref[...] + _dot(merged.astype(BF16), wout_ref[...])
    f_in = _rms(h, nf_ref[...]).astype(BF16)
    cw = 1024
    acc = h
    for c in range(D_FF // cw):
        f = _dot(f_in, wup_ref[:, c * cw:(c + 1) * cw])
        f = jnp.square(jnp.maximum(f, 0.0)).astype(BF16)
        acc = acc + _dot(f, wdn_ref[c * cw:(c + 1) * cw, :])
    out_ref[...] = _rms(acc, nfin_ref[...])


def _tail(x2, yr, o, gt, wor, wom, wout, nf, wup, wdn, nfin, seq):
    t_tok = x2.shape[0]
    tm = min(256, seq)
    full = lambda shape: pl.BlockSpec(shape, lambda i: (0,) * len(shape))
    return pl.pallas_call(
        _tail_kernel,
        grid=(t_tok // tm,),
        in_specs=[
            pl.BlockSpec((tm, D_MODEL), lambda i: (i, 0)),
            pl.BlockSpec((tm, RWKV_DIM), lambda i: (i, 0)),
            pl.BlockSpec((tm, MLA_HEADS * V_DIM), lambda i: (i, 0)),
            pl.BlockSpec((tm, GT_COLS), lambda i: (i, 0)),
            full((RWKV_DIM, D_MODEL)), full((MLA_HEADS * V_DIM, D_MODEL)), full((D_MODEL, D_MODEL)),
            full((1, D_MODEL)), full((D_MODEL, D_FF)), full((D_FF, D_MODEL)), full((1, D_MODEL)),
        ],
        out_specs=pl.BlockSpec((tm, D_MODEL), lambda i: (i, 0)),
        out_shape=jax.ShapeDtypeStruct((t_tok, D_MODEL), F32),
        compiler_params=_cparams(("arbitrary",)),
        name="merge_ffn",
    )(x2, yr, o, gt, wor, wom, wout, nf, wup, wdn, nfin)


def _padc(w, n):
    return jnp.pad(w, ((0, 0), (0, n - w.shape[1])))


def _padr(w, n):
    return jnp.pad(w, ((0, n - w.shape[0]), (0, 0)))


def _rot_half(w):
    half = w.shape[-1] // 2
    return jnp.concatenate([-w[..., half:], w[..., :half]], axis=-1)


def kernel(x, positions, norm_mix, w_in, mu_shift, w0, w_up, a0, a_up, g_up, k_k, k_a, r_k, ln_w, ln_b, w_o_rwkv, q_norm, w_uq, kv_norm, w_ukv, w_o_mla, w_out, norm_ffn, w_ff_up, w_ff_down, norm_final):
    batch, seq, _ = x.shape
    t_tok = batch * seq
    x2 = x.reshape(t_tok, D_MODEL)
    l = 0

    wi = w_in[l]
    o = 0
    w_r3 = wi[:, o:o + 3 * RWKV_DIM]; o += 3 * RWKV_DIM
    w_zw = wi[:, o:o + DECAY_LORA]; o += DECAY_LORA
    w_za = wi[:, o:o + ICLR_LORA]; o += ICLR_LORA
    w_zg = wi[:, o:o + GATE_LORA]; o += GATE_LORA
    w_cq = wi[:, o:o + Q_LORA]; o += Q_LORA
    w_ckv = wi[:, o:o + KV_LORA]; o += KV_LORA
    w_kr = wi[:, o:o + ROPE_DIM]; o += ROPE_DIM
    w_gate = wi[:, o:o + 2 * D_MODEL]
    zeros64 = jnp.zeros((D_MODEL, NOPE_DIM), wi.dtype)
    w_kr_p = _padc(jnp.concatenate([zeros64, w_kr], axis=1), 128)
    w_krr_p = _padc(jnp.concatenate([zeros64, _rot_half(w_kr)], axis=1), 128)
    w_a = jnp.concatenate([
        w_r3, _padc(w_zw, 128), _padc(w_za, 128), _padc(w_zg, 256),
        w_cq, w_ckv, w_kr_p, w_krr_p, w_gate], axis=1).astype(BF16)
    mu = mu_shift[l]
    o = 3 * RWKV_DIM
    mu_a = jnp.concatenate([
        mu[:o], jnp.pad(mu[o:o + 64], (0, 64)), jnp.pad(mu[o + 64:o + 128], (0, 64)),
        jnp.pad(mu[o + 128:], (0, 96))])[None, :]

    rw, ml, gt = _inproj(x2, norm_mix[l][None, :], w_a, mu_a, seq)

    prm = [p[None, :] for p in (w0[l], a0[l], k_k[l], k_a[l], r_k[l].reshape(-1), ln_w[l], ln_b[l])]
    yr = _rwkv(rw, prm, _padr(w_up[l], 128).astype(BF16), _padr(a_up[l], 128).astype(BF16),
               _padr(g_up[l], 256).astype(BF16), batch, seq)

    half = ROPE_DIM // 2
    inv_freq = 1.0 / (ROPE_THETA ** (jnp.arange(half, dtype=F32) * (2.0 / ROPE_DIM)))
    freq = jnp.concatenate([jnp.zeros((NOPE_DIM,), F32), inv_freq, inv_freq,
                            jnp.zeros((HEAD_PAD - NOPE_DIM - ROPE_DIM,), F32)])[None, :]
    wq = w_uq[l].reshape(Q_LORA, MLA_HEADS, NOPE_DIM + ROPE_DIM)
    wq_p = jnp.pad(wq, ((0, 0), (0, 0), (0, HEAD_PAD - NOPE_DIM - ROPE_DIM)))
    wq_rot = jnp.concatenate([jnp.zeros_like(wq[..., :NOPE_DIM]), _rot_half(wq[..., NOPE_DIM:])], axis=-1)
    wq_rot_p = jnp.pad(wq_rot, ((0, 0), (0, 0), (0, HEAD_PAD - NOPE_DIM - ROPE_DIM)))
    wkv = w_ukv[l].reshape(KV_LORA, MLA_HEADS, NOPE_DIM + V_DIM)
    wk_p = jnp.pad(wkv[..., :NOPE_DIM], ((0, 0), (0, 0), (0, HEAD_PAD - NOPE_DIM)))
    wvt = jnp.transpose(wkv[..., NOPE_DIM:], (1, 2, 0))
    tk = min(512, seq)
    q5, k5, vt5 = _mla_prep(
        ml, positions.reshape(t_tok, 1), freq, q_norm[l][None, :], kv_norm[l][None, :],
        wq_p.reshape(Q_LORA, -1).astype(BF16), wq_rot_p.reshape(Q_LORA, -1).astype(BF16),
        wk_p.reshape(KV_LORA, -1).astype(BF16), wvt.astype(BF16), batch, seq, tk)
    o_att = _attn(q5, k5, vt5, batch, seq, tk)

    out = _tail(x2, yr, o_att, gt, w_o_rwkv[l].astype(BF16), w_o_mla[l].astype(BF16),
                w_out[l].astype(BF16), norm_ffn[l][None, :], w_ff_up[l].astype(BF16),
                w_ff_down[l].astype(BF16), norm_final[None, :], seq)
    return out.reshape(batch, seq, D_MODEL)
```

```python
import functools
import math

import jax
import jax.numpy as jnp
from jax import lax
from jax.experimental import pallas as pl
from jax.experimental.pallas import tpu as pltpu

F32 = jnp.float32
BF16 = jnp.bfloat16

D_MODEL = 1024
NORM_EPS = 1e-6
RWKV_HEAD = 64
RWKV_HEADS = 8
RWKV_DIM = RWKV_HEADS * RWKV_HEAD
DECAY_LORA = 64
ICLR_LORA = 64
GATE_LORA = 160
GN_EPS = 64e-5
MLA_HEADS = 8
Q_LORA = 256
KV_LORA = 128
NOPE_DIM = 64
ROPE_DIM = 32
V_DIM = 64
ROPE_THETA = 10000.0
D_FF = 4 * D_MODEL

LANE = 128
CHUNK = 64
PAIR = 2 * RWKV_HEAD
HEAD_PAD = 128

RW_COLS = 3 * RWKV_DIM + 128 + 128 + 256
ML_COLS = Q_LORA + KV_LORA + 128 + 128
GT_COLS = 2 * D_MODEL
VMEM_LIMIT = 56 * 1024 * 1024


def _cparams(sem):
    return pltpu.CompilerParams(dimension_semantics=sem, vmem_limit_bytes=VMEM_LIMIT)


def _sigmoid(x):
    return 1.0 / (1.0 + jnp.exp(-x))


def _rms(x, g):
    ms = jnp.mean(x * x, axis=-1, keepdims=True)
    return x * lax.rsqrt(ms + NORM_EPS) * g


def _dot(a, b):
    return jnp.dot(a, b, preferred_element_type=F32)


def _dot_nt(a, b):
    return lax.dot_general(a, b, (((1,), (1,)), ((), ())), preferred_element_type=F32)


def _dot_tn(a, b):
    return lax.dot_general(a, b, (((0,), (0,)), ((), ())), preferred_element_type=F32)


def _inproj_kernel(x_ref, g_ref, w_ref, mu_ref, rw_ref, ml_ref, gt_ref, carry_ref,
                   *, tiles_per_seq, tm):
    i = pl.program_id(0)
    u = _rms(x_ref[...], g_ref[...]).astype(BF16)

    @pl.when(i % tiles_per_seq == 0)
    def _():
        carry_ref[...] = jnp.zeros_like(carry_ref)

    cw = 512
    row0 = lax.broadcasted_iota(jnp.int32, (tm, cw), 0) == 0
    for c in range(RW_COLS // cw):
        cs = slice(c * cw, (c + 1) * cw)
        z = _dot(u, w_ref[:, cs])
        prev = pltpu.roll(z, 1, 0)
        prev = jnp.where(row0, carry_ref[7:8, cs], prev)
        carry_ref[:, cs] = z[tm - 8:tm, :]
        rw_ref[:, cs] = z + (prev - z) * mu_ref[:, cs]
    ml_ref[...] = _dot(u, w_ref[:, RW_COLS:RW_COLS + ML_COLS])
    for c in range(GT_COLS // cw):
        gt_ref[:, c * cw:(c + 1) * cw] = _dot(
            u, w_ref[:, RW_COLS + ML_COLS + c * cw:RW_COLS + ML_COLS + (c + 1) * cw])


def _inproj(x2, g, w_a, mu_a, seq):
    t_tok = x2.shape[0]
    tm = min(512, seq)
    ncol = RW_COLS + ML_COLS + GT_COLS
    kern = functools.partial(_inproj_kernel, tiles_per_seq=seq // tm, tm=tm)
    return pl.pallas_call(
        kern,
        grid=(t_tok // tm,),
        in_specs=[
            pl.BlockSpec((tm, D_MODEL), lambda i: (i, 0)),
            pl.BlockSpec((1, D_MODEL), lambda i: (0, 0)),
            pl.BlockSpec((D_MODEL, ncol), lambda i: (0, 0)),
            pl.BlockSpec((1, RW_COLS), lambda i: (0, 0)),
        ],
        out_specs=[
            pl.BlockSpec((tm, RW_COLS), lambda i: (i, 0)),
            pl.BlockSpec((tm, ML_COLS), lambda i: (i, 0)),
            pl.BlockSpec((tm, GT_COLS), lambda i: (i, 0)),
        ],
        out_shape=[
            jax.ShapeDtypeStruct((t_tok, RW_COLS), F32),
            jax.ShapeDtypeStruct((t_tok, ML_COLS), F32),
            jax.ShapeDtypeStruct((t_tok, GT_COLS), F32),
        ],
        scratch_shapes=[pltpu.VMEM((8, RW_COLS), F32)],
        compiler_params=_cparams(("arbitrary",)),
        name="inproj",
    )(x2, g, w_a, mu_a)


def _split3(x):
    hi = x.astype(BF16)
    r1 = x - hi.astype(F32)
    mid = r1.astype(BF16)
    lo = (r1 - mid.astype(F32)).astype(BF16)
    return hi, mid, lo


def _rwkv_masks():
    n = 2 * CHUNK
    row = lax.broadcasted_iota(jnp.int32, (n, n), 0)
    col = lax.broadcasted_iota(jnp.int32, (n, n), 1)
    same = (row // CHUNK) == (col // CHUNK)
    strict = jnp.where(same & (row > col), 1.0, 0.0).astype(F32)
    incl = jnp.where(same & (row >= col), 1.0, 0.0).astype(F32)
    m16 = jnp.where((row // 16) == (col // 16), 1.0, 0.0).astype(F32)
    m32 = jnp.where(((row // 32) == (col // 32)) & ((row // 16) > (col // 16)), 1.0, 0.0).astype(F32)
    m64 = jnp.where((row // 32) > (col // 32), 1.0, 0.0).astype(F32)
    eye = jnp.where(row == col, 1.0, 0.0).astype(F32)
    headsel = (row // CHUNK) == (col // RWKV_HEAD)
    return strict, incl, m16, m32, m64, eye, headsel


def _unit_lower_inverse(a_list, m16, m32, m64, eye):
    n = 2 * CHUNK
    bf = lambda x: x.astype(BF16)
    ds = [bf(a * m16) for a in a_list]
    d2 = [_dot(d, d) for d in ds]
    ts = [eye + a * m16 for a in a_list]
    xs = [_dot(bf(jnp.concatenate([t, p], axis=0)), bf(p)) for t, p in zip(ts, d2)]
    ts = [t + x[:n] for t, x in zip(ts, xs)]
    d4 = [x[n:] for x in xs]
    xs = [_dot(bf(jnp.concatenate([t, p], axis=0)), bf(p)) for t, p in zip(ts, d4)]
    ts = [t + x[:n] for t, x in zip(ts, xs)]
    d8 = [x[n:] for x in xs]
    ts = [t + _dot(bf(t), bf(p)) for t, p in zip(ts, d8)]
    for msk in (m32, m64):
        tb = [bf(t) for t in ts]
        xs = [bf(_dot(t, bf(a * msk))) for t, a in zip(tb, a_list)]
        ts = [t + _dot(x, t16) for t, x, t16 in zip(ts, xs, tb)]
    return ts


def _rwkv_chunks(r, k, v, a, b, lw, h2, masks, tril, nchunk):
    strict, incl, m16, m32, m64, eye, headsel = masks
    n = 2 * CHUNK
    bf = lambda x: x.astype(BF16)
    rng = range(nchunk)
    sl = [slice(c * CHUNK, (c + 1) * CHUNK) for c in rng]

    def stack(x):
        return jnp.where(headsel, jnp.concatenate([x, x], axis=0), 0.0)

    cs = []
    for c in rng:
        hi, mid, lo = _split3(lw[sl[c]])
        cs.append(_dot(tril, hi) + _dot(tril, mid) + _dot(tril, lo))
    g_in = [jnp.exp(x) for x in cs]
    g_ex = [jnp.exp(x - lw[sl[c]]) for c, x in enumerate(cs)]
    g_inv = [jnp.exp(-x) for x in cs]
    g_last = [jnp.exp(x[CHUNK - 1:CHUNK, :]) for x in cs]

    rt = [stack(r[sl[c]] * g_in[c]) for c in rng]
    at = [bf(stack(a[sl[c]] * g_ex[c])) for c in rng]
    bt = [stack(b[sl[c]] * g_inv[c]) for c in rng]
    kt = [stack(k[sl[c]] * g_inv[c]) for c in rng]
    bh = [bf(bt[c] * g_last[c]) for c in rng]
    kh = [bf(kt[c] * g_last[c]) for c in rng]
    v2 = [bf(stack(v[sl[c]])) for c in rng]

    bk = [bf(jnp.concatenate([bt[c], kt[c]], axis=0)) for c in rng]
    ga = [_dot_nt(at[c], bk[c]) for c in rng]
    gr = [_dot_nt(bf(rt[c]), bk[c]) for c in rng]
    a_ab = [g[:, :n] * strict for g in ga]
    a_ak = [g[:, n:] * strict for g in ga]
    a_rb = [bf(g[:, :n] * incl) for g in gr]
    a_rk = [g[:, n:] * incl for g in gr]
    t = _unit_lower_inverse(a_ab, m16, m32, m64, eye)
    xv = [_dot(bf(jnp.concatenate([a_ak[c], a_rk[c]], axis=0)), v2[c]) for c in rng]
    wu = [bf(_dot(bf(t[c]), jnp.concatenate([at[c], bf(xv[c][:n])], axis=1))) for c in rng]
    qy = [_dot(a_rb[c], wu[c]) for c in rng]
    qeff = [rt[c] + qy[c][:, :n] for c in rng]
    y0 = [qy[c][:, n:] + xv[c][n:] for c in rng]
    mn = [_dot_tn(bh[c], wu[c]) for c in rng]
    m = [eye * g_last[c] + mn[c][:, :n] for c in rng]
    nn = [mn[c][:, n:] + _dot_tn(kh[c], v2[c]) for c in rng]
    qm = [bf(jnp.concatenate([qeff[c], m[c]], axis=0)) for c in rng]
    ys = []
    for c in rng:
        yh = _dot(qm[c], bf(h2))
        y2 = yh[:n] + y0[c]
        h2 = yh[n:] + nn[c]
        ys.append(y2[:CHUNK, :] + y2[CHUNK:, :])
    return jnp.concatenate(ys, axis=0), h2


def _rwkv_kernel(zr_ref, zk_ref, zv_ref, zw_ref, za_ref, zg_ref,
                 w0_ref, a0_ref, kk_ref, ka_ref, rk_ref, lnw_ref, lnb_ref,
                 wup_ref, aup_ref, gup_ref, y_ref, h_ref, *, tb):
    @pl.when(pl.program_id(2) == 0)
    def _():
        h_ref[...] = jnp.zeros_like(h_ref)

    lane = lax.broadcasted_iota(jnp.int32, (tb, PAIR), 1)
    head0 = lane < RWKV_HEAD

    def head_sum(x):
        s0 = jnp.sum(jnp.where(head0, x, 0.0), axis=-1, keepdims=True)
        s1 = jnp.sum(jnp.where(head0, 0.0, x), axis=-1, keepdims=True)
        return jnp.where(head0, s0, s1)

    zr = zr_ref[...]
    zk = zk_ref[...]
    zv = zv_ref[...]
    w_pre = w0_ref[...] + _dot(jnp.tanh(zw_ref[...]).astype(BF16), wup_ref[...])
    lw = -math.exp(-0.5) * _sigmoid(w_pre)
    iclr = _sigmoid(a0_ref[...] + _dot(za_ref[...].astype(BF16), aup_ref[...]))
    gate = _dot(_sigmoid(zg_ref[...]).astype(BF16), gup_ref[...])
    kk = zk * kk_ref[...]
    kk = kk / jnp.maximum(jnp.sqrt(head_sum(kk * kk)), 1e-12)
    k = zk * (1.0 + (iclr - 1.0) * ka_ref[...])
    a = -kk
    b = kk * iclr

    masks = _rwkv_masks()
    trow = lax.broadcasted_iota(jnp.int32, (CHUNK, CHUNK), 0)
    tcol = lax.broadcasted_iota(jnp.int32, (CHUNK, CHUNK), 1)
    tril = jnp.where(trow >= tcol, 1.0, 0.0).astype(BF16)

    y, h2 = _rwkv_chunks(zr, k, zv, a, b, lw, h_ref[...], masks, tril, tb // CHUNK)
    h_ref[...] = h2

    mean = head_sum(y) * (1.0 / RWKV_HEAD)
    yc = y - mean
    var = head_sum(yc * yc) * (1.0 / RWKV_HEAD)
    yn = yc * lax.rsqrt(var + GN_EPS) * lnw_ref[...] + lnb_ref[...]
    bonus = head_sum(zr * k * rk_ref[...]) * zv
    y_ref[...] = ((yn + bonus) * gate).astype(y_ref.dtype)


def _rwkv(rw, prm, wup, aup, gup, batch, seq):
    t_tok = rw.shape[0]
    tb = min(512, seq)
    nt = seq // tb
    npair = RWKV_DIM // PAIR

    def tok(col):
        return pl.BlockSpec((tb, PAIR), lambda b, p, t, col=col: (b * nt + t, col(p)))

    prm_spec = pl.BlockSpec((1, PAIR), lambda b, p, t: (0, p))
    in_specs = [
        tok(lambda p: p), tok(lambda p: npair + p), tok(lambda p: 2 * npair + p),
        tok(lambda p: 3 * npair), tok(lambda p: 3 * npair + 1),
        pl.BlockSpec((tb, 256), lambda b, p, t: (b * nt + t, (3 * RWKV_DIM + 256) // 256)),
    ] + [prm_spec] * 7 + [
        pl.BlockSpec((128, PAIR), lambda b, p, t: (0, p)),
        pl.BlockSpec((128, PAIR), lambda b, p, t: (0, p)),
        pl.BlockSpec((256, PAIR), lambda b, p, t: (0, p)),
    ]
    return pl.pallas_call(
        functools.partial(_rwkv_kernel, tb=tb),
        grid=(batch, npair, nt),
        in_specs=in_specs,
        out_specs=pl.BlockSpec((tb, PAIR), lambda b, p, t: (b * nt + t, p)),
        out_shape=jax.ShapeDtypeStruct((t_tok, RWKV_DIM), BF16),
        scratch_shapes=[pltpu.VMEM((PAIR, PAIR), F32)],
        compiler_params=_cparams(("arbitrary", "arbitrary", "arbitrary")),
        name="rwkv7",
    )(rw, rw, rw, rw, rw, rw, *prm, wup, aup, gup)


def _mla_prep_kernel(ml_ref, pos_ref, freq_ref, qn_ref, kvn_ref, wq_ref, wqr_ref, wk_ref, wvt_ref,
                     q_ref, k_ref, vt_ref, *, scale):
    ml = ml_ref[...]
    ang = pos_ref[...].astype(F32) * freq_ref[...]
    cosf = jnp.cos(ang)
    sinf = jnp.sin(ang)
    cq = _rms(ml[:, :Q_LORA], qn_ref[...]).astype(BF16)
    ckv = _rms(ml[:, Q_LORA:Q_LORA + KV_LORA], kvn_ref[...]).astype(BF16)
    kr = ml[:, Q_LORA + KV_LORA:Q_LORA + KV_LORA + 128]
    krr = ml[:, Q_LORA + KV_LORA + 128:Q_LORA + KV_LORA + 256]
    k_rope = kr * cosf + krr * sinf
    qf = _dot(cq, wq_ref[...])
    qr = _dot(cq, wqr_ref[...])
    kf = _dot(ckv, wk_ref[...])
    for h in range(MLA_HEADS):
        hs = slice(h * HEAD_PAD, (h + 1) * HEAD_PAD)
        q_ref[h] = ((qf[:, hs] * cosf + qr[:, hs] * sinf) * scale).astype(q_ref.dtype)
        k_ref[h] = (kf[:, hs] + k_rope).astype(k_ref.dtype)
        vt_ref[h] = _dot_nt(wvt_ref[h], ckv).astype(vt_ref.dtype)


def _mla_prep(ml, pos, freq, qn, kvn, wq, wqr, wk, wvt, batch, seq, tk):
    nk = seq // tk
    scale = (NOPE_DIM + ROPE_DIM) ** -0.5
    full = lambda shape: pl.BlockSpec(shape, lambda b, t: (0,) * len(shape))
    return pl.pallas_call(
        functools.partial(_mla_prep_kernel, scale=scale),
        grid=(batch, nk),
        in_specs=[
            pl.BlockSpec((tk, ML_COLS), lambda b, t: (b * nk + t, 0)),
            pl.BlockSpec((tk, 1), lambda b, t: (b * nk + t, 0)),
            full((1, HEAD_PAD)), full((1, Q_LORA)), full((1, KV_LORA)),
            full((Q_LORA, MLA_HEADS * HEAD_PAD)), full((Q_LORA, MLA_HEADS * HEAD_PAD)),
            full((KV_LORA, MLA_HEADS * HEAD_PAD)), full((MLA_HEADS, V_DIM, KV_LORA)),
        ],
        out_specs=[
            pl.BlockSpec((None, MLA_HEADS, None, tk, HEAD_PAD), lambda b, t: (b, 0, t, 0, 0)),
            pl.BlockSpec((None, MLA_HEADS, None, tk, HEAD_PAD), lambda b, t: (b, 0, t, 0, 0)),
            pl.BlockSpec((None, MLA_HEADS, None, V_DIM, tk), lambda b, t: (b, 0, t, 0, 0)),
        ],
        out_shape=[
            jax.ShapeDtypeStruct((batch, MLA_HEADS, nk, tk, HEAD_PAD), BF16),
            jax.ShapeDtypeStruct((batch, MLA_HEADS, nk, tk, HEAD_PAD), BF16),
            jax.ShapeDtypeStruct((batch, MLA_HEADS, nk, V_DIM, tk), BF16),
        ],
        compiler_params=_cparams(("arbitrary", "arbitrary")),
        name="mla_prep",
    )(ml, pos, freq, qn, kvn, wq, wqr, wk, wvt)


def _attn_kernel(q_ref, k_ref, vt_ref, o_ref, *, tk, heads):
    i = pl.program_id(2)
    krow = lax.broadcasted_iota(jnp.int32, (tk, tk), 0)
    qcol = lax.broadcasted_iota(jnp.int32, (tk, tk), 1)
    causal = krow <= qcol
    neg = jnp.finfo(F32).min
    outs = []
    for h in range(heads):
        q = q_ref[h, 0]

        def step(j, carry, masked):
            m, l, acc = carry
            s = _dot_nt(k_ref[h, j], q)
            if masked:
                s = jnp.where(causal, s, neg)
            m_new = jnp.maximum(m, jnp.max(s, axis=0, keepdims=True))
            alpha = jnp.exp(m - m_new)
            p = jnp.exp(s - m_new)
            l = alpha * l + jnp.sum(p, axis=0, keepdims=True)
            acc = alpha * acc + _dot(vt_ref[h, j], p.astype(BF16))
            return m_new, l, acc

        init = (jnp.full((1, tk), neg, F32), jnp.zeros((1, tk), F32), jnp.zeros((V_DIM, tk), F32))
        carry = lax.fori_loop(0, i, lambda j, c: step(j, c, False), init)
        m, l, acc = step(i, carry, True)
        outs.append((acc / l).T)
    o_ref[...] = jnp.concatenate(outs, axis=-1).astype(o_ref.dtype)


def _attn(q5, k5, vt5, batch, seq, tk):
    nk = seq // tk
    heads = 2
    ngrp = MLA_HEADS // heads
    return pl.pallas_call(
        functools.partial(_attn_kernel, tk=tk, heads=heads),
        grid=(batch, ngrp, nk),
        in_specs=[
            pl.BlockSpec((None, heads, 1, tk, HEAD_PAD), lambda b, g, i: (b, g, i, 0, 0)),
            pl.BlockSpec((None, heads, nk, tk, HEAD_PAD), lambda b, g, i: (b, g, 0, 0, 0)),
            pl.BlockSpec((None, heads, nk, V_DIM, tk), lambda b, g, i: (b, g, 0, 0, 0)),
        ],
        out_specs=pl.BlockSpec((tk, heads * V_DIM), lambda b, g, i: (b * nk + i, g)),
        out_shape=jax.ShapeDtypeStruct((batch * seq, MLA_HEADS * V_DIM), BF16),
        compiler_params=_cparams(("arbitrary", "arbitrary", "arbitrary")),
        name="mla_attn",
    )(q5, k5, vt5)


def _tail_kernel(x_ref, yr_ref, o_ref, gt_ref, wor_ref, wom_ref, wout_ref, nf_ref, wup_ref, wdn_ref,
                 nfin_ref, out_ref):
    y_a = _dot(yr_ref[...], wor_ref[...])
    y_b = _dot(o_ref[...], wom_ref[...])
    merged = _sigmoid(gt_ref[:, :D_MODEL]) * y_a + _sigmoid(gt_ref[:, D_MODEL:]) * y_b
    h = x_ref[...] + _dot(merged.astype(BF16), wout_ref[...])
    f_in = _rms(h, nf_ref[...]).astype(BF16)
    cw = 1024
    acc = h
    for c in range(D_FF // cw):
        f = _dot(f_in, wup_ref[:, c * cw:(c + 1) * cw])
        f = jnp.square(jnp.maximum(f, 0.0)).astype(BF16)
        acc = acc + _dot(f, wdn_ref[c * cw:(c + 1) * cw, :])
    out_ref[...] = _rms(acc, nfin_ref[...])


def _tail(x2, yr, o, gt, wor, wom, wout, nf, wup, wdn, nfin, seq):
    t_tok = x2.shape[0]
    tm = min(256, seq)
    full = lambda shape: pl.BlockSpec(shape, lambda i: (0,) * len(shape))
    return pl.pallas_call(
        _tail_kernel,
        grid=(t_tok // tm,),
        in_specs=[
            pl.BlockSpec((tm, D_MODEL), lambda i: (i, 0)),
            pl.BlockSpec((tm, RWKV_DIM), lambda i: (i, 0)),
            pl.BlockSpec((tm, MLA_HEADS * V_DIM), lambda i: (i, 0)),
            pl.BlockSpec((tm, GT_COLS), lambda i: (i, 0)),
            full((RWKV_DIM, D_MODEL)), full((MLA_HEADS * V_DIM, D_MODEL)), full((D_MODEL, D_MODEL)),
            full((1, D_MODEL)), full((D_MODEL, D_FF)), full((D_FF, D_MODEL)), full((1, D_MODEL)),
        ],
        out_specs=pl.BlockSpec((tm, D_MODEL), lambda i: (i, 0)),
        out_shape=jax.ShapeDtypeStruct((t_tok, D_MODEL), F32),
        compiler_params=_cparams(("arbitrary",)),
        name="merge_ffn",
    )(x2, yr, o, gt, wor, wom, wout, nf, wup, wdn, nfin)


def _padc(w, n):
    return jnp.pad(w, ((0, 0), (0, n - w.shape[1])))


def _padr(w, n):
    return jnp.pad(w, ((0, n - w.shape[0]), (0, 0)))


def _rot_half(w):
    half = w.shape[-1] // 2
    return jnp.concatenate([-w[..., half:], w[..., :half]], axis=-1)


def kernel(x, positions, norm_mix, w_in, mu_shift, w0, w_up, a0, a_up, g_up, k_k, k_a, r_k, ln_w, ln_b, w_o_rwkv, q_norm, w_uq, kv_norm, w_ukv, w_o_mla, w_out, norm_ffn, w_ff_up, w_ff_down, norm_final):
    batch, seq, _ = x.shape
    t_tok = batch * seq
    x2 = x.reshape(t_tok, D_MODEL)
    l = 0

    wi = w_in[l]
    o = 0
    w_r3 = wi[:, o:o + 3 * RWKV_DIM]; o += 3 * RWKV_DIM
    w_zw = wi[:, o:o + DECAY_LORA]; o += DECAY_LORA
    w_za = wi[:, o:o + ICLR_LORA]; o += ICLR_LORA
    w_zg = wi[:, o:o + GATE_LORA]; o += GATE_LORA
    w_cq = wi[:, o:o + Q_LORA]; o += Q_LORA
    w_ckv = wi[:, o:o + KV_LORA]; o += KV_LORA
    w_kr = wi[:, o:o + ROPE_DIM]; o += ROPE_DIM
    w_gate = wi[:, o:o + 2 * D_MODEL]
    zeros64 = jnp.zeros((D_MODEL, NOPE_DIM), wi.dtype)
    w_kr_p = _padc(jnp.concatenate([zeros64, w_kr], axis=1), 128)
    w_krr_p = _padc(jnp.concatenate([zeros64, _rot_half(w_kr)], axis=1), 128)
    w_a = jnp.concatenate([
        w_r3, _padc(w_zw, 128), _padc(w_za, 128), _padc(w_zg, 256),
        w_cq, w_ckv, w_kr_p, w_krr_p, w_gate], axis=1).astype(BF16)
    mu = mu_shift[l]
    o = 3 * RWKV_DIM
    mu_a = jnp.concatenate([
        mu[:o], jnp.pad(mu[o:o + 64], (0, 64)), jnp.pad(mu[o + 64:o + 128], (0, 64)),
        jnp.pad(mu[o + 128:], (0, 96))])[None, :]

    rw, ml, gt = _inproj(x2, norm_mix[l][None, :], w_a, mu_a, seq)

    prm = [p[None, :] for p in (w0[l], a0[l], k_k[l], k_a[l], r_k[l].reshape(-1), ln_w[l], ln_b[l])]
    yr = _rwkv(rw, prm, _padr(w_up[l], 128).astype(BF16), _padr(a_up[l], 128).astype(BF16),
               _padr(g_up[l], 256).astype(BF16), batch, seq)

    half = ROPE_DIM // 2
    inv_freq = 1.0 / (ROPE_THETA ** (jnp.arange(half, dtype=F32) * (2.0 / ROPE_DIM)))
    freq = jnp.concatenate([jnp.zeros((NOPE_DIM,), F32), inv_freq, inv_freq,
                            jnp.zeros((HEAD_PAD - NOPE_DIM - ROPE_DIM,), F32)])[None, :]
    wq = w_uq[l].reshape(Q_LORA, MLA_HEADS, NOPE_DIM + ROPE_DIM)
    wq_p = jnp.pad(wq, ((0, 0), (0, 0), (0, HEAD_PAD - NOPE_DIM - ROPE_DIM)))
    wq_rot = jnp.concatenate([jnp.zeros_like(wq[..., :NOPE_DIM]), _rot_half(wq[..., NOPE_DIM:])], axis=-1)
    wq_rot_p = jnp.pad(wq_rot, ((0, 0), (0, 0), (0, HEAD_PAD - NOPE_DIM - ROPE_DIM)))
    wkv = w_ukv[l].reshape(KV_LORA, MLA_HEADS, NOPE_DIM + V_DIM)
    wk_p = jnp.pad(wkv[..., :NOPE_DIM], ((0, 0), (0, 0), (0, HEAD_PAD - NOPE_DIM)))
    wvt = jnp.transpose(wkv[..., NOPE_DIM:], (1, 2, 0))
    tk = min(512, seq)
    q5, k5, vt5 = _mla_prep(
        ml, positions.reshape(t_tok, 1), freq, q_norm[l][None, :], kv_norm[l][None, :],
        wq_p.reshape(Q_LORA, -1).astype(BF16), wq_rot_p.reshape(Q_LORA, -1).astype(BF16),
        wk_p.reshape(KV_LORA, -1).astype(BF16), wvt.astype(BF16), batch, seq, tk)
    o_att = _attn(q5, k5, vt5, batch, seq, tk)

    out = _tail(x2, yr, o_att, gt, w_o_rwkv[l].astype(BF16), w_o_mla[l].astype(BF16),
                w_out[l].astype(BF16), norm_ffn[l][None, :], w_ff_up[l].astype(BF16),
                w_ff_down[l].astype(BF16), norm_final[None, :], seq)
    return out.reshape(batch, seq, D_MODEL)
```

```python
import functools
import math

import jax
import jax.numpy as jnp
from jax import lax
from jax.experimental import pallas as pl
from jax.experimental.pallas import tpu as pltpu

F32 = jnp.float32
BF16 = jnp.bfloat16

D_MODEL = 1024
NORM_EPS = 1e-6
RWKV_HEAD = 64
RWKV_HEADS = 8
RWKV_DIM = RWKV_HEADS * RWKV_HEAD
DECAY_LORA = 64
ICLR_LORA = 64
GATE_LORA = 160
GN_EPS = 64e-5
MLA_HEADS = 8
Q_LORA = 256
KV_LORA = 128
NOPE_DIM = 64
ROPE_DIM = 32
V_DIM = 64
ROPE_THETA = 10000.0
D_FF = 4 * D_MODEL

LANE = 128
CHUNK = 64
PAIR = 2 * RWKV_HEAD
HEAD_PAD = 128

RW_COLS = 3 * RWKV_DIM + 128 + 128 + 256
ML_COLS = Q_LORA + KV_LORA + 128 + 128
GT_COLS = 2 * D_MODEL
VMEM_LIMIT = 56 * 1024 * 1024


def _cparams(sem):
    return pltpu.CompilerParams(dimension_semantics=sem, vmem_limit_bytes=VMEM_LIMIT)


def _sigmoid(x):
    return 1.0 / (1.0 + jnp.exp(-x))


def _rms(x, g):
    ms = jnp.mean(x * x, axis=-1, keepdims=True)
    return x * lax.rsqrt(ms + NORM_EPS) * g


def _dot(a, b):
    return jnp.dot(a, b, preferred_element_type=F32)


def _dot_nt(a, b):
    return lax.dot_general(a, b, (((1,), (1,)), ((), ())), preferred_element_type=F32)


def _dot_tn(a, b):
    return lax.dot_general(a, b, (((0,), (0,)), ((), ())), preferred_element_type=F32)


def _inproj_kernel(x_ref, g_ref, w_ref, mu_ref, rw_ref, ml_ref, gt_ref, carry_ref,
                   *, tiles_per_seq, tm):
    i = pl.program_id(0)
    u = _rms(x_ref[...], g_ref[...]).astype(BF16)

    @pl.when(i % tiles_per_seq == 0)
    def _():
        carry_ref[...] = jnp.zeros_like(carry_ref)

    cw = 512
    row0 = lax.broadcasted_iota(jnp.int32, (tm, cw), 0) == 0
    for c in range(RW_COLS // cw):
        cs = slice(c * cw, (c + 1) * cw)
        z = _dot(u, w_ref[:, cs])
        prev = pltpu.roll(z, 1, 0)
        prev = jnp.where(row0, carry_ref[7:8, cs], prev)
        carry_ref[:, cs] = z[tm - 8:tm, :]
        rw_ref[:, cs] = z + (prev - z) * mu_ref[:, cs]
    ml_ref[...] = _dot(u, w_ref[:, RW_COLS:RW_COLS + ML_COLS])
    for c in range(GT_COLS // cw):
        gt_ref[:, c * cw:(c + 1) * cw] = _dot(
            u, w_ref[:, RW_COLS + ML_COLS + c * cw:RW_COLS + ML_COLS + (c + 1) * cw])


def _inproj(x2, g, w_a, mu_a, seq):
    t_tok = x2.shape[0]
    tm = min(512, seq)
    ncol = RW_COLS + ML_COLS + GT_COLS
    kern = functools.partial(_inproj_kernel, tiles_per_seq=seq // tm, tm=tm)
    return pl.pallas_call(
        kern,
        grid=(t_tok // tm,),
        in_specs=[
            pl.BlockSpec((tm, D_MODEL), lambda i: (i, 0)),
            pl.BlockSpec((1, D_MODEL), lambda i: (0, 0)),
            pl.BlockSpec((D_MODEL, ncol), lambda i: (0, 0)),
            pl.BlockSpec((1, RW_COLS), lambda i: (0, 0)),
        ],
        out_specs=[
            pl.BlockSpec((tm, RW_COLS), lambda i: (i, 0)),
            pl.BlockSpec((tm, ML_COLS), lambda i: (i, 0)),
            pl.BlockSpec((tm, GT_COLS), lambda i: (i, 0)),
        ],
        out_shape=[
            jax.ShapeDtypeStruct((t_tok, RW_COLS), F32),
            jax.ShapeDtypeStruct((t_tok, ML_COLS), F32),
            jax.ShapeDtypeStruct((t_tok, GT_COLS), F32),
        ],
        scratch_shapes=[pltpu.VMEM((8, RW_COLS), F32)],
        compiler_params=_cparams(("arbitrary",)),
        name="inproj",
    )(x2, g, w_a, mu_a)


def _split3(x):
    hi = x.astype(BF16)
    r1 = x - hi.astype(F32)
    mid = r1.astype(BF16)
    lo = (r1 - mid.astype(F32)).astype(BF16)
    return hi, mid, lo


def _rwkv_masks():
    n = 2 * CHUNK
    row = lax.broadcasted_iota(jnp.int32, (n, n), 0)
    col = lax.broadcasted_iota(jnp.int32, (n, n), 1)
    same = (row // CHUNK) == (col // CHUNK)
    strict = jnp.where(same & (row > col), 1.0, 0.0).astype(F32)
    incl = jnp.where(same & (row >= col), 1.0, 0.0).astype(F32)
    m16 = jnp.where((row // 16) == (col // 16), 1.0, 0.0).astype(F32)
    m32 = jnp.where(((row // 32) == (col // 32)) & ((row // 16) > (col // 16)), 1.0, 0.0).astype(F32)
    m64 = jnp.where((row // 32) > (col // 32), 1.0, 0.0).astype(F32)
    eye = jnp.where(row == col, 1.0, 0.0).astype(F32)
    headsel = (row // CHUNK) == (col // RWKV_HEAD)
    return strict, incl, m16, m32, m64, eye, headsel


def _unit_lower_inverse(a_list, m16, m32, m64, eye):
    n = 2 * CHUNK
    bf = lambda x: x.astype(BF16)
    ds = [bf(a * m16) for a in a_list]
    d2 = [_dot(d, d) for d in ds]
    ts = [eye + a * m16 for a in a_list]
    xs = [_dot(bf(jnp.concatenate([t, p], axis=0)), bf(p)) for t, p in zip(ts, d2)]
    ts = [t + x[:n] for t, x in zip(ts, xs)]
    d4 = [x[n:] for x in xs]
    xs = [_dot(bf(jnp.concatenate([t, p], axis=0)), bf(p)) for t, p in zip(ts, d4)]
    ts = [t + x[:n] for t, x in zip(ts, xs)]
    d8 = [x[n:] for x in xs]
    ts = [t + _dot(bf(t), bf(p)) for t, p in zip(ts, d8)]
    for msk in (m32, m64):
        tb = [bf(t) for t in ts]
        xs = [bf(_dot(t, bf(a * msk))) for t, a in zip(tb, a_list)]
        ts = [t + _dot(x, t16) for t, x, t16 in zip(ts, xs, tb)]
    return ts


def _rwkv_chunks(streams, h_list, masks, tril, nchunk):
    strict, incl, m16, m32, m64, eye, headsel = masks
    n = 2 * CHUNK
    bf = lambda x: x.astype(BF16)
    items = [(q, slice(c * CHUNK, (c + 1) * CHUNK)) for c in range(nchunk) for q in range(len(streams))]
    rng = range(len(items))
    col = lambda j: [streams[q][j][s] for q, s in items]
    r, k, v, a, b, lw = (col(j) for j in range(6))

    def stack(x):
        return jnp.where(headsel, jnp.concatenate([x, x], axis=0), 0.0)

    cs = []
    for i in rng:
        hi, mid, lo = _split3(lw[i])
        cs.append(_dot(tril, hi) + _dot(tril, mid) + _dot(tril, lo))
    g_in = [jnp.exp(cs[i]) for i in rng]
    g_ex = [jnp.exp(cs[i] - lw[i]) for i in rng]
    g_inv = [jnp.exp(-cs[i]) for i in rng]
    g_last = [jnp.exp(cs[i][CHUNK - 1:CHUNK, :]) for i in rng]

    rt = [stack(r[i] * g_in[i]) for i in rng]
    at = [bf(stack(a[i] * g_ex[i])) for i in rng]
    bt = [stack(b[i] * g_inv[i]) for i in rng]
    kt = [stack(k[i] * g_inv[i]) for i in rng]
    bh = [bf(bt[i] * g_last[i]) for i in rng]
    kh = [bf(kt[i] * g_last[i]) for i in rng]
    v2 = [bf(stack(v[i])) for i in rng]

    bk = [bf(jnp.concatenate([bt[i], kt[i]], axis=0)) for i in rng]
    ga = [_dot_nt(at[i], bk[i]) for i in rng]
    gr = [_dot_nt(bf(rt[i]), bk[i]) for i in rng]
    a_ab = [g[:, :n] * strict for g in ga]
    a_ak = [g[:, n:] * strict for g in ga]
    a_rb = [bf(g[:, :n] * incl) for g in gr]
    a_rk = [g[:, n:] * incl for g in gr]
    t = _unit_lower_inverse(a_ab, m16, m32, m64, eye)
    xv = [_dot(bf(jnp.concatenate([a_ak[i], a_rk[i]], axis=0)), v2[i]) for i in rng]
    wu = [bf(_dot(bf(t[i]), jnp.concatenate([at[i], bf(xv[i][:n])], axis=1))) for i in rng]
    qy = [_dot(a_rb[i], wu[i]) for i in rng]
    qeff = [rt[i] + qy[i][:, :n] for i in rng]
    y0 = [qy[i][:, n:] + xv[i][n:] for i in rng]
    mn = [_dot_tn(bh[i], wu[i]) for i in rng]
    m = [eye * g_last[i] + mn[i][:, :n] for i in rng]
    nn = [mn[i][:, n:] + _dot_tn(kh[i], v2[i]) for i in rng]
    mq = [bf(jnp.concatenate([m[i], qeff[i]], axis=0)) for i in rng]
    h_list = list(h_list)
    ys = [[] for _ in streams]
    for i in rng:
        q = items[i][0]
        hy = _dot(mq[i], bf(h_list[q]))
        h_list[q] = hy[:n] + nn[i]
        y2 = hy[n:] + y0[i]
        ys[q].append(y2[:CHUNK, :] + y2[CHUNK:, :])
    return [jnp.concatenate(y, axis=0) for y in ys], h_list


def _rwkv_kernel(zr_ref, zk_ref, zv_ref, zw_ref, za_ref, zg_ref,
                 w0_ref, a0_ref, kk_ref, ka_ref, rk_ref, lnw_ref, lnb_ref,
                 wup_ref, aup_ref, gup_ref, y_ref, h_ref, *, tb, npair):
    @pl.when(pl.program_id(2) == 0)
    def _():
        h_ref[...] = jnp.zeros_like(h_ref)

    lane = lax.broadcasted_iota(jnp.int32, (tb, PAIR), 1)
    head0 = lane < RWKV_HEAD

    def head_sum(x):
        s0 = jnp.sum(jnp.where(head0, x, 0.0), axis=-1, keepdims=True)
        s1 = jnp.sum(jnp.where(head0, 0.0, x), axis=-1, keepdims=True)
        return jnp.where(head0, s0, s1)

    tanh_zw = jnp.tanh(zw_ref[...]).astype(BF16)
    za = za_ref[...].astype(BF16)
    sig_zg = _sigmoid(zg_ref[...]).astype(BF16)
    streams, keep = [], []
    for q in range(npair):
        ls = slice(q * PAIR, (q + 1) * PAIR)
        zr = zr_ref[:, ls]
        zk = zk_ref[:, ls]
        zv = zv_ref[:, ls]
        w_pre = w0_ref[:, ls] + _dot(tanh_zw, wup_ref[:, ls])
        lw = -math.exp(-0.5) * _sigmoid(w_pre)
        iclr = _sigmoid(a0_ref[:, ls] + _dot(za, aup_ref[:, ls]))
        gate = _dot(sig_zg, gup_ref[:, ls])
        kk = zk * kk_ref[:, ls]
        kk = kk / jnp.maximum(jnp.sqrt(head_sum(kk * kk)), 1e-12)
        k = zk * (1.0 + (iclr - 1.0) * ka_ref[:, ls])
        streams.append((zr, k, zv, -kk, kk * iclr, lw))
        keep.append((zr, k, zv, gate))

    masks = _rwkv_masks()
    trow = lax.broadcasted_iota(jnp.int32, (CHUNK, CHUNK), 0)
    tcol = lax.broadcasted_iota(jnp.int32, (CHUNK, CHUNK), 1)
    tril = jnp.where(trow >= tcol, 1.0, 0.0).astype(BF16)

    ys, hs = _rwkv_chunks(streams, [h_ref[q] for q in range(npair)], masks, tril, tb // CHUNK)
    for q in range(npair):
        ls = slice(q * PAIR, (q + 1) * PAIR)
        h_ref[q] = hs[q]
        zr, k, zv, gate = keep[q]
        y = ys[q]
        mean = head_sum(y) * (1.0 / RWKV_HEAD)
        yc = y - mean
        var = head_sum(yc * yc) * (1.0 / RWKV_HEAD)
        yn = yc * lax.rsqrt(var + GN_EPS) * lnw_ref[:, ls] + lnb_ref[:, ls]
        bonus = head_sum(zr * k * rk_ref[:, ls]) * zv
        y_ref[:, ls] = ((yn + bonus) * gate).astype(y_ref.dtype)


def _rwkv(rw, prm, wup, aup, gup, batch, seq):
    t_tok = rw.shape[0]
    tb = min(512, seq)
    nt = seq // tb
    npair = 2
    wid = npair * PAIR
    ngrp = RWKV_DIM // wid

    def tok(base):
        return pl.BlockSpec((tb, wid), lambda b, p, t: (b * nt + t, base * ngrp + p))

    def lora(base, width):
        return pl.BlockSpec((tb, width), lambda b, p, t: (b * nt + t, base))

    prm_spec = pl.BlockSpec((1, wid), lambda b, p, t: (0, p))
    in_specs = [
        tok(0), tok(1), tok(2),
        lora(3 * RWKV_DIM // 128, 128), lora(3 * RWKV_DIM // 128 + 1, 128),
        lora((3 * RWKV_DIM + 256) // 256, 256),
    ] + [prm_spec] * 7 + [
        pl.BlockSpec((128, wid), lambda b, p, t: (0, p)),
        pl.BlockSpec((128, wid), lambda b, p, t: (0, p)),
        pl.BlockSpec((256, wid), lambda b, p, t: (0, p)),
    ]
    return pl.pallas_call(
        functools.partial(_rwkv_kernel, tb=tb, npair=npair),
        grid=(batch, ngrp, nt),
        in_specs=in_specs,
        out_specs=pl.BlockSpec((tb, wid), lambda b, p, t: (b * nt + t, p)),
        out_shape=jax.ShapeDtypeStruct((t_tok, RWKV_DIM), BF16),
        scratch_shapes=[pltpu.VMEM((npair, PAIR, PAIR), F32)],
        compiler_params=_cparams(("arbitrary", "arbitrary", "arbitrary")),
        name="rwkv7",
    )(rw, rw, rw, rw, rw, rw, *prm, wup, aup, gup)


def _mla_prep_kernel(ml_ref, pos_ref, freq_ref, qn_ref, kvn_ref, wq_ref, wqr_ref, wk_ref, wvt_ref,
                     q_ref, k_ref, vt_ref, *, scale):
    ml = ml_ref[...]
    ang = pos_ref[...].astype(F32) * freq_ref[...]
    cosf = jnp.cos(ang)
    sinf = jnp.sin(ang)
    cq = _rms(ml[:, :Q_LORA], qn_ref[...]).astype(BF16)
    ckv = _rms(ml[:, Q_LORA:Q_LORA + KV_LORA], kvn_ref[...]).astype(BF16)
    kr = ml[:, Q_LORA + KV_LORA:Q_LORA + KV_LORA + 128]
    krr = ml[:, Q_LORA + KV_LORA + 128:Q_LORA + KV_LORA + 256]
    k_rope = kr * cosf + krr * sinf
    qf = _dot(cq, wq_ref[...])
    qr = _dot(cq, wqr_ref[...])
    kf = _dot(ckv, wk_ref[...])
    for h in range(MLA_HEADS):
        hs = slice(h * HEAD_PAD, (h + 1) * HEAD_PAD)
        q_ref[h] = ((qf[:, hs] * cosf + qr[:, hs] * sinf) * scale).astype(q_ref.dtype)
        k_ref[h] = (kf[:, hs] + k_rope).astype(k_ref.dtype)
        vt_ref[h] = _dot_nt(wvt_ref[h], ckv).astype(vt_ref.dtype)


def _mla_prep(ml, pos, freq, qn, kvn, wq, wqr, wk, wvt, batch, seq, tk):
    nk = seq // tk
    scale = (NOPE_DIM + ROPE_DIM) ** -0.5 * math.log2(math.e)
    full = lambda shape: pl.BlockSpec(shape, lambda b, t: (0,) * len(shape))
    return pl.pallas_call(
        functools.partial(_mla_prep_kernel, scale=scale),
        grid=(batch, nk),
        in_specs=[
            pl.BlockSpec((tk, ML_COLS), lambda b, t: (b * nk + t, 0)),
            pl.BlockSpec((tk, 1), lambda b, t: (b * nk + t, 0)),
            full((1, HEAD_PAD)), full((1, Q_LORA)), full((1, KV_LORA)),
            full((Q_LORA, MLA_HEADS * HEAD_PAD)), full((Q_LORA, MLA_HEADS * HEAD_PAD)),
            full((KV_LORA, MLA_HEADS * HEAD_PAD)), full((MLA_HEADS, V_DIM, KV_LORA)),
        ],
        out_specs=[
            pl.BlockSpec((None, MLA_HEADS, None, tk, HEAD_PAD), lambda b, t: (b, 0, t, 0, 0)),
            pl.BlockSpec((None, MLA_HEADS, None, tk, HEAD_PAD), lambda b, t: (b, 0, t, 0, 0)),
            pl.BlockSpec((None, MLA_HEADS, None, V_DIM, tk), lambda b, t: (b, 0, t, 0, 0)),
        ],
        out_shape=[
            jax.ShapeDtypeStruct((batch, MLA_HEADS, nk, tk, HEAD_PAD), BF16),
            jax.ShapeDtypeStruct((batch, MLA_HEADS, nk, tk, HEAD_PAD), BF16),
            jax.ShapeDtypeStruct((batch, MLA_HEADS, nk, V_DIM, tk), BF16),
        ],
        compiler_params=_cparams(("arbitrary", "arbitrary")),
        name="mla_prep",
    )(ml, pos, freq, qn, kvn, wq, wqr, wk, wvt)


def _attn_kernel(q_ref, k_ref, vt_ref, o_ref, *, tk, heads):
    i = pl.program_id(2)
    krow = lax.broadcasted_iota(jnp.int32, (tk, tk), 0)
    qcol = lax.broadcasted_iota(jnp.int32, (tk, tk), 1)
    causal = krow <= qcol
    neg = jnp.finfo(F32).min
    hs = range(heads)
    qs = [q_ref[h, 0] for h in hs]

    def step(j, carry, masked):
        ms, ls, accs = carry
        s = [_dot_nt(k_ref[h, j], qs[h]) for h in hs]
        if masked:
            s = [jnp.where(causal, x, neg) for x in s]
        m_new = [jnp.maximum(ms[h], jnp.max(s[h], axis=0, keepdims=True)) for h in hs]
        alpha = [jnp.exp2(ms[h] - m_new[h]) for h in hs]
        p = [jnp.exp2(s[h] - m_new[h]) for h in hs]
        ls = [alpha[h] * ls[h] + jnp.sum(p[h], axis=0, keepdims=True) for h in hs]
        pv = [_dot(vt_ref[h, j], p[h].astype(BF16)) for h in hs]
        accs = [alpha[h] * accs[h] + pv[h] for h in hs]
        return tuple(m_new), tuple(ls), tuple(accs)

    init = (tuple(jnp.full((1, tk), neg, F32) for _ in hs),
            tuple(jnp.zeros((1, tk), F32) for _ in hs),
            tuple(jnp.zeros((V_DIM, tk), F32) for _ in hs))
    carry = lax.fori_loop(0, i, lambda j, c: step(j, c, False), init)
    _, ls, accs = step(i, carry, True)
    outs = [(accs[h] / ls[h]).T for h in hs]
    o_ref[...] = jnp.concatenate(outs, axis=-1).astype(o_ref.dtype)


def _attn(q5, k5, vt5, batch, seq, tk):
    nk = seq // tk
    heads = 4
    ngrp = MLA_HEADS // heads
    return pl.pallas_call(
        functools.partial(_attn_kernel, tk=tk, heads=heads),
        grid=(batch, ngrp, nk),
        in_specs=[
            pl.BlockSpec((None, heads, 1, tk, HEAD_PAD), lambda b, g, i: (b, g, i, 0, 0)),
            pl.BlockSpec((None, heads, nk, tk, HEAD_PAD), lambda b, g, i: (b, g, 0, 0, 0)),
            pl.BlockSpec((None, heads, nk, V_DIM, tk), lambda b, g, i: (b, g, 0, 0, 0)),
        ],
        out_specs=pl.BlockSpec((tk, heads * V_DIM), lambda b, g, i: (b * nk + i, g)),
        out_shape=jax.ShapeDtypeStruct((batch * seq, MLA_HEADS * V_DIM), BF16),
        compiler_params=_cparams(("arbitrary", "arbitrary", "arbitrary")),
        name="mla_attn",
    )(q5, k5, vt5)


def _tail_kernel(x_ref, yr_ref, o_ref, gt_ref, wor_ref, wom_ref, wout_ref, nf_ref, wup_ref, wdn_ref,
                 nfin_ref, out_ref):
    y_a = _dot(yr_ref[...], wor_ref[...])
    y_b = _dot(o_ref[...], wom_ref[...])
    merged = _sigmoid(gt_ref[:, :D_MODEL]) * y_a + _sigmoid(gt_ref[:, D_MODEL:]) * y_b
    h = x_ref[...] + _dot(merged.astype(BF16), wout_ref[...])
    f_in = _rms(h, nf_ref[...]).astype(BF16)
    cw = 1024
    acc = h
    for c in range(D_FF // cw):
        f = _dot(f_in, wup_ref[:, c * cw:(c + 1) * cw])
        f = jnp.square(jnp.maximum(f, 0.0)).astype(BF16)
        acc = acc + _dot(f, wdn_ref[c * cw:(c + 1) * cw, :])
    out_ref[...] = _rms(acc, nfin_ref[...])


def _tail(x2, yr, o, gt, wor, wom, wout, nf, wup, wdn, nfin, seq):
    t_tok = x2.shape[0]
    tm = min(256, seq)
    full = lambda shape: pl.BlockSpec(shape, lambda i: (0,) * len(shape))
    return pl.pallas_call(
        _tail_kernel,
        grid=(t_tok // tm,),
        in_specs=[
            pl.BlockSpec((tm, D_MODEL), lambda i: (i, 0)),
            pl.BlockSpec((tm, RWKV_DIM), lambda i: (i, 0)),
            pl.BlockSpec((tm, MLA_HEADS * V_DIM), lambda i: (i, 0)),
            pl.BlockSpec((tm, GT_COLS), lambda i: (i, 0)),
            full((RWKV_DIM, D_MODEL)), full((MLA_HEADS * V_DIM, D_MODEL)), full((D_MODEL, D_MODEL)),
            full((1, D_MODEL)), full((D_MODEL, D_FF)), full((D_FF, D_MODEL)), full((1, D_MODEL)),
        ],
        out_specs=pl.BlockSpec((tm, D_MODEL), lambda i: (i, 0)),
        out_shape=jax.ShapeDtypeStruct((t_tok, D_MODEL), F32),
        compiler_params=_cparams(("arbitrary",)),
        name="merge_ffn",
    )(x2, yr, o, gt, wor, wom, wout, nf, wup, wdn, nfin)


def _padc(w, n):
    return jnp.pad(w, ((0, 0), (0, n - w.shape[1])))


def _padr(w, n):
    return jnp.pad(w, ((0, n - w.shape[0]), (0, 0)))


def _rot_half(w):
    half = w.shape[-1] // 2
    return jnp.concatenate([-w[..., half:], w[..., :half]], axis=-1)


def kernel(x, positions, norm_mix, w_in, mu_shift, w0, w_up, a0, a_up, g_up, k_k, k_a, r_k, ln_w, ln_b, w_o_rwkv, q_norm, w_uq, kv_norm, w_ukv, w_o_mla, w_out, norm_ffn, w_ff_up, w_ff_down, norm_final):
    batch, seq, _ = x.shape
    t_tok = batch * seq
    x2 = x.reshape(t_tok, D_MODEL)
    l = 0

    wi = w_in[l]
    o = 0
    w_r3 = wi[:, o:o + 3 * RWKV_DIM]; o += 3 * RWKV_DIM
    w_zw = wi[:, o:o + DECAY_LORA]; o += DECAY_LORA
    w_za = wi[:, o:o + ICLR_LORA]; o += ICLR_LORA
    w_zg = wi[:, o:o + GATE_LORA]; o += GATE_LORA
    w_cq = wi[:, o:o + Q_LORA]; o += Q_LORA
    w_ckv = wi[:, o:o + KV_LORA]; o += KV_LORA
    w_kr = wi[:, o:o + ROPE_DIM]; o += ROPE_DIM
    w_gate = wi[:, o:o + 2 * D_MODEL]
    zeros64 = jnp.zeros((D_MODEL, NOPE_DIM), wi.dtype)
    w_kr_p = _padc(jnp.concatenate([zeros64, w_kr], axis=1), 128)
    w_krr_p = _padc(jnp.concatenate([zeros64, _rot_half(w_kr)], axis=1), 128)
    w_a = jnp.concatenate([
        w_r3, _padc(w_zw, 128), _padc(w_za, 128), _padc(w_zg, 256),
        w_cq, w_ckv, w_kr_p, w_krr_p, w_gate], axis=1).astype(BF16)
    mu = mu_shift[l]
    o = 3 * RWKV_DIM
    mu_a = jnp.concatenate([
        mu[:o], jnp.pad(mu[o:o + 64], (0, 64)), jnp.pad(mu[o + 64:o + 128], (0, 64)),
        jnp.pad(mu[o + 128:], (0, 96))])[None, :]

    rw, ml, gt = _inproj(x2, norm_mix[l][None, :], w_a, mu_a, seq)

    prm = [p[None, :] for p in (w0[l], a0[l], k_k[l], k_a[l], r_k[l].reshape(-1), ln_w[l], ln_b[l])]
    yr = _rwkv(rw, prm, _padr(w_up[l], 128).astype(BF16), _padr(a_up[l], 128).astype(BF16),
               _padr(g_up[l], 256).astype(BF16), batch, seq)

    half = ROPE_DIM // 2
    inv_freq = 1.0 / (ROPE_THETA ** (jnp.arange(half, dtype=F32) * (2.0 / ROPE_DIM)))
    freq = jnp.concatenate([jnp.zeros((NOPE_DIM,), F32), inv_freq, inv_freq,
                            jnp.zeros((HEAD_PAD - NOPE_DIM - ROPE_DIM,), F32)])[None, :]
    wq = w_uq[l].reshape(Q_LORA, MLA_HEADS, NOPE_DIM + ROPE_DIM)
    wq_p = jnp.pad(wq, ((0, 0), (0, 0), (0, HEAD_PAD - NOPE_DIM - ROPE_DIM)))
    wq_rot = jnp.concatenate([jnp.zeros_like(wq[..., :NOPE_DIM]), _rot_half(wq[..., NOPE_DIM:])], axis=-1)
    wq_rot_p = jnp.pad(wq_rot, ((0, 0), (0, 0), (0, HEAD_PAD - NOPE_DIM - ROPE_DIM)))
    wkv = w_ukv[l].reshape(KV_LORA, MLA_HEADS, NOPE_DIM + V_DIM)
    wk_p = jnp.pad(wkv[..., :NOPE_DIM], ((0, 0), (0, 0), (0, HEAD_PAD - NOPE_DIM)))
    wvt = jnp.transpose(wkv[..., NOPE_DIM:], (1, 2, 0))
    tk = min(512, seq)
    q5, k5, vt5 = _mla_prep(
        ml, positions.reshape(t_tok, 1), freq, q_norm[l][None, :], kv_norm[l][None, :],
        wq_p.reshape(Q_LORA, -1).astype(BF16), wq_rot_p.reshape(Q_LORA, -1).astype(BF16),
        wk_p.reshape(KV_LORA, -1).astype(BF16), wvt.astype(BF16), batch, seq, tk)
    o_att = _attn(q5, k5, vt5, batch, seq, tk)

    out = _tail(x2, yr, o_att, gt, w_o_rwkv[l].astype(BF16), w_o_mla[l].astype(BF16),
                w_out[l].astype(BF16), norm_ffn[l][None, :], w_ff_up[l].astype(BF16),
                w_ff_down[l].astype(BF16), norm_final[None, :], seq)
    return out.reshape(batch, seq, D_MODEL)
```

```python
import functools
import math

import jax
import jax.numpy as jnp
from jax import lax
from jax.experimental import pallas as pl
from jax.experimental.pallas import tpu as pltpu

F32 = jnp.float32
BF16 = jnp.bfloat16

D_MODEL = 1024
NORM_EPS = 1e-6
RWKV_HEAD = 64
RWKV_HEADS = 8
RWKV_DIM = RWKV_HEADS * RWKV_HEAD
DECAY_LORA = 64
ICLR_LORA = 64
GATE_LORA = 160
GN_EPS = 64e-5
MLA_HEADS = 8
Q_LORA = 256
KV_LORA = 128
NOPE_DIM = 64
ROPE_DIM = 32
V_DIM = 64
ROPE_THETA = 10000.0
D_FF = 4 * D_MODEL

LANE = 128
CHUNK = 64
PAIR = 2 * RWKV_HEAD
HEAD_PAD = 128

RW_COLS = 3 * RWKV_DIM + 128 + 128 + 256
ML_COLS = Q_LORA + KV_LORA + 128 + 128
GT_COLS = 2 * D_MODEL
VMEM_LIMIT = 56 * 1024 * 1024


def _cparams(sem):
    return pltpu.CompilerParams(dimension_semantics=sem, vmem_limit_bytes=VMEM_LIMIT)


def _sigmoid(x):
    return 1.0 / (1.0 + jnp.exp(-x))


def _rms(x, g):
    ms = jnp.mean(x * x, axis=-1, keepdims=True)
    return x * lax.rsqrt(ms + NORM_EPS) * g


def _dot(a, b):
    return jnp.dot(a, b, preferred_element_type=F32)


def _dot_nt(a, b):
    return lax.dot_general(a, b, (((1,), (1,)), ((), ())), preferred_element_type=F32)


def _dot_tn(a, b):
    return lax.dot_general(a, b, (((0,), (0,)), ((), ())), preferred_element_type=F32)


def _inproj_kernel(x_ref, g_ref, w_ref, mu_ref, rw_ref, ml_ref, gt_ref, carry_ref,
                   *, tiles_per_seq, tm):
    i = pl.program_id(0)
    u = _rms(x_ref[...], g_ref[...]).astype(BF16)

    @pl.when(i % tiles_per_seq == 0)
    def _():
        carry_ref[...] = jnp.zeros_like(carry_ref)

    cw = 512
    row0 = lax.broadcasted_iota(jnp.int32, (tm, cw), 0) == 0
    for c in range(RW_COLS // cw):
        cs = slice(c * cw, (c + 1) * cw)
        z = _dot(u, w_ref[:, cs])
        prev = pltpu.roll(z, 1, 0)
        prev = jnp.where(row0, carry_ref[7:8, cs], prev)
        carry_ref[:, cs] = z[tm - 8:tm, :]
        rw_ref[:, cs] = z + (prev - z) * mu_ref[:, cs]
    ml_ref[...] = _dot(u, w_ref[:, RW_COLS:RW_COLS + ML_COLS])
    for c in range(GT_COLS // cw):
        gt_ref[:, c * cw:(c + 1) * cw] = _dot(
            u, w_ref[:, RW_COLS + ML_COLS + c * cw:RW_COLS + ML_COLS + (c + 1) * cw])


def _inproj(x2, g, w_a, mu_a, seq):
    t_tok = x2.shape[0]
    tm = min(512, seq)
    ncol = RW_COLS + ML_COLS + GT_COLS
    kern = functools.partial(_inproj_kernel, tiles_per_seq=seq // tm, tm=tm)
    return pl.pallas_call(
        kern,
        grid=(t_tok // tm,),
        in_specs=[
            pl.BlockSpec((tm, D_MODEL), lambda i: (i, 0)),
            pl.BlockSpec((1, D_MODEL), lambda i: (0, 0)),
            pl.BlockSpec((D_MODEL, ncol), lambda i: (0, 0)),
            pl.BlockSpec((1, RW_COLS), lambda i: (0, 0)),
        ],
        out_specs=[
            pl.BlockSpec((tm, RW_COLS), lambda i: (i, 0)),
            pl.BlockSpec((tm, ML_COLS), lambda i: (i, 0)),
            pl.BlockSpec((tm, GT_COLS), lambda i: (i, 0)),
        ],
        out_shape=[
            jax.ShapeDtypeStruct((t_tok, RW_COLS), F32),
            jax.ShapeDtypeStruct((t_tok, ML_COLS), F32),
            jax.ShapeDtypeStruct((t_tok, GT_COLS), F32),
        ],
        scratch_shapes=[pltpu.VMEM((8, RW_COLS), F32)],
        compiler_params=_cparams(("arbitrary",)),
        name="inproj",
    )(x2, g, w_a, mu_a)


def _split3(x):
    hi = x.astype(BF16)
    r1 = x - hi.astype(F32)
    mid = r1.astype(BF16)
    lo = (r1 - mid.astype(F32)).astype(BF16)
    return hi, mid, lo


def _rwkv_masks():
    n = 2 * CHUNK
    row = lax.broadcasted_iota(jnp.int32, (n, n), 0)
    col = lax.broadcasted_iota(jnp.int32, (n, n), 1)
    same = (row // CHUNK) == (col // CHUNK)
    strict = jnp.where(same & (row > col), 1.0, 0.0).astype(F32)
    incl = jnp.where(same & (row >= col), 1.0, 0.0).astype(F32)
    m16 = jnp.where((row // 16) == (col // 16), 1.0, 0.0).astype(F32)
    m32 = jnp.where(((row // 32) == (col // 32)) & ((row // 16) > (col // 16)), 1.0, 0.0).astype(F32)
    m64 = jnp.where((row // 32) > (col // 32), 1.0, 0.0).astype(F32)
    eye = jnp.where(row == col, 1.0, 0.0).astype(F32)
    headsel = (row // CHUNK) == (col // RWKV_HEAD)
    return strict, incl, m16, m32, m64, eye, headsel


def _unit_lower_inverse(a_list, m16, m32, m64, eye):
    n = 2 * CHUNK
    bf = lambda x: x.astype(BF16)
    ds = [bf(a * m16) for a in a_list]
    d2 = [_dot(d, d) for d in ds]
    ts = [eye + a * m16 for a in a_list]
    xs = [_dot(bf(jnp.concatenate([t, p], axis=0)), bf(p)) for t, p in zip(ts, d2)]
    ts = [t + x[:n] for t, x in zip(ts, xs)]
    d4 = [x[n:] for x in xs]
    xs = [_dot(bf(jnp.concatenate([t, p], axis=0)), bf(p)) for t, p in zip(ts, d4)]
    ts = [t + x[:n] for t, x in zip(ts, xs)]
    d8 = [x[n:] for x in xs]
    ts = [t + _dot(bf(t), bf(p)) for t, p in zip(ts, d8)]
    for msk in (m32, m64):
        tb = [bf(t) for t in ts]
        xs = [bf(_dot(t, bf(a * msk))) for t, a in zip(tb, a_list)]
        ts = [t + _dot(x, t16) for t, x, t16 in zip(ts, xs, tb)]
    return ts


def _rwkv_chunks(streams, h_list, masks, tril, nchunk):
    strict, incl, m16, m32, m64, eye, headsel = masks
    n = 2 * CHUNK
    bf = lambda x: x.astype(BF16)
    items = [(q, slice(c * CHUNK, (c + 1) * CHUNK)) for c in range(nchunk) for q in range(len(streams))]
    rng = range(len(items))
    col = lambda j: [streams[q][j][s] for q, s in items]
    r, k, v, a, b, lw = (col(j) for j in range(6))

    def stack(x):
        return jnp.where(headsel, jnp.concatenate([x, x], axis=0), 0.0)

    cs = []
    for i in rng:
        hi, mid, lo = _split3(lw[i])
        cs.append(_dot(tril, hi) + _dot(tril, mid) + _dot(tril, lo))
    g_in = [jnp.exp(cs[i]) for i in rng]
    g_ex = [jnp.exp(cs[i] - lw[i]) for i in rng]
    g_inv = [jnp.exp(-cs[i]) for i in rng]
    g_last = [jnp.exp(cs[i][CHUNK - 1:CHUNK, :]) for i in rng]

    rt = [stack(r[i] * g_in[i]) for i in rng]
    at = [bf(stack(a[i] * g_ex[i])) for i in rng]
    bt = [stack(b[i] * g_inv[i]) for i in rng]
    kt = [stack(k[i] * g_inv[i]) for i in rng]
    bh = [bf(bt[i] * g_last[i]) for i in rng]
    kh = [bf(kt[i] * g_last[i]) for i in rng]
    v2 = [bf(stack(v[i])) for i in rng]

    bk = [bf(jnp.concatenate([bt[i], kt[i]], axis=0)) for i in rng]
    ga = [_dot_nt(at[i], bk[i]) for i in rng]
    gr = [_dot_nt(bf(rt[i]), bk[i]) for i in rng]
    a_ab = [g[:, :n] * strict for g in ga]
    a_ak = [g[:, n:] * strict for g in ga]
    a_rb = [bf(g[:, :n] * incl) for g in gr]
    a_rk = [g[:, n:] * incl for g in gr]
    t = _unit_lower_inverse(a_ab, m16, m32, m64, eye)
    xv = [_dot(bf(jnp.concatenate([a_ak[i], a_rk[i]], axis=0)), v2[i]) for i in rng]
    wu = [bf(_dot(bf(t[i]), jnp.concatenate([at[i], bf(xv[i][:n])], axis=1))) for i in rng]
    qy = [_dot(a_rb[i], wu[i]) for i in rng]
    qeff = [rt[i] + qy[i][:, :n] for i in rng]
    y0 = [qy[i][:, n:] + xv[i][n:] for i in rng]
    mn = [_dot_tn(bh[i], wu[i]) for i in rng]
    m = [eye * g_last[i] + mn[i][:, :n] for i in rng]
    nn = [mn[i][:, n:] + _dot_tn(kh[i], v2[i]) for i in rng]
    mq = [bf(jnp.concatenate([m[i], qeff[i]], axis=0)) for i in rng]
    h_list = list(h_list)
    ys = [[] for _ in streams]
    for i in rng:
        q = items[i][0]
        hy = _dot(mq[i], bf(h_list[q]))
        h_list[q] = hy[:n] + nn[i]
        y2 = hy[n:] + y0[i]
        ys[q].append(y2[:CHUNK, :] + y2[CHUNK:, :])
    return [jnp.concatenate(y, axis=0) for y in ys], h_list


def _rwkv_kernel(zr_ref, zk_ref, zv_ref, zw_ref, za_ref, zg_ref,
                 w0_ref, a0_ref, kk_ref, ka_ref, rk_ref, lnw_ref, lnb_ref,
                 wup_ref, aup_ref, gup_ref, y_ref, h_ref, *, tb, npair):
    @pl.when(pl.program_id(2) == 0)
    def _():
        h_ref[...] = jnp.zeros_like(h_ref)

    lane = lax.broadcasted_iota(jnp.int32, (tb, PAIR), 1)
    head0 = lane < RWKV_HEAD

    def head_sum(x):
        s0 = jnp.sum(jnp.where(head0, x, 0.0), axis=-1, keepdims=True)
        s1 = jnp.sum(jnp.where(head0, 0.0, x), axis=-1, keepdims=True)
        return jnp.where(head0, s0, s1)

    tanh_zw = jnp.tanh(zw_ref[...]).astype(BF16)
    za = za_ref[...].astype(BF16)
    sig_zg = _sigmoid(zg_ref[...]).astype(BF16)
    streams, keep = [], []
    for q in range(npair):
        ls = slice(q * PAIR, (q + 1) * PAIR)
        zr = zr_ref[:, ls]
        zk = zk_ref[:, ls]
        zv = zv_ref[:, ls]
        w_pre = w0_ref[:, ls] + _dot(tanh_zw, wup_ref[:, ls])
        lw = -math.exp(-0.5) * _sigmoid(w_pre)
        iclr = _sigmoid(a0_ref[:, ls] + _dot(za, aup_ref[:, ls]))
        gate = _dot(sig_zg, gup_ref[:, ls])
        kk = zk * kk_ref[:, ls]
        kk = kk / jnp.maximum(jnp.sqrt(head_sum(kk * kk)), 1e-12)
        k = zk * (1.0 + (iclr - 1.0) * ka_ref[:, ls])
        streams.append((zr, k, zv, -kk, kk * iclr, lw))
        keep.append((zr, k, zv, gate))

    masks = _rwkv_masks()
    trow = lax.broadcasted_iota(jnp.int32, (CHUNK, CHUNK), 0)
    tcol = lax.broadcasted_iota(jnp.int32, (CHUNK, CHUNK), 1)
    tril = jnp.where(trow >= tcol, 1.0, 0.0).astype(BF16)

    ys, hs = _rwkv_chunks(streams, [h_ref[q] for q in range(npair)], masks, tril, tb // CHUNK)
    for q in range(npair):
        ls = slice(q * PAIR, (q + 1) * PAIR)
        h_ref[q] = hs[q]
        zr, k, zv, gate = keep[q]
        y = ys[q]
        mean = head_sum(y) * (1.0 / RWKV_HEAD)
        yc = y - mean
        var = head_sum(yc * yc) * (1.0 / RWKV_HEAD)
        yn = yc * lax.rsqrt(var + GN_EPS) * lnw_ref[:, ls] + lnb_ref[:, ls]
        bonus = head_sum(zr * k * rk_ref[:, ls]) * zv
        y_ref[:, ls] = ((yn + bonus) * gate).astype(y_ref.dtype)


def _rwkv(rw, prm, wup, aup, gup, batch, seq):
    t_tok = rw.shape[0]
    tb = min(512, seq)
    nt = seq // tb
    npair = 2
    wid = npair * PAIR
    ngrp = RWKV_DIM // wid

    def tok(base):
        return pl.BlockSpec((tb, wid), lambda b, p, t: (b * nt + t, base * ngrp + p))

    def lora(base, width):
        return pl.BlockSpec((tb, width), lambda b, p, t: (b * nt + t, base))

    prm_spec = pl.BlockSpec((1, wid), lambda b, p, t: (0, p))
    in_specs = [
        tok(0), tok(1), tok(2),
        lora(3 * RWKV_DIM // 128, 128), lora(3 * RWKV_DIM // 128 + 1, 128),
        lora((3 * RWKV_DIM + 256) // 256, 256),
    ] + [prm_spec] * 7 + [
        pl.BlockSpec((128, wid), lambda b, p, t: (0, p)),
        pl.BlockSpec((128, wid), lambda b, p, t: (0, p)),
        pl.BlockSpec((256, wid), lambda b, p, t: (0, p)),
    ]
    return pl.pallas_call(
        functools.partial(_rwkv_kernel, tb=tb, npair=npair),
        grid=(batch, ngrp, nt),
        in_specs=in_specs,
        out_specs=pl.BlockSpec((tb, wid), lambda b, p, t: (b * nt + t, p)),
        out_shape=jax.ShapeDtypeStruct((t_tok, RWKV_DIM), BF16),
        scratch_shapes=[pltpu.VMEM((npair, PAIR, PAIR), F32)],
        compiler_params=_cparams(("arbitrary", "arbitrary", "arbitrary")),
        name="rwkv7",
    )(rw, rw, rw, rw, rw, rw, *prm, wup, aup, gup)


def _mla_prep_kernel(ml_ref, pos_ref, freq_ref, qn_ref, kvn_ref, wq_ref, wqr_ref, wk_ref, wvt_ref,
                     q_ref, k_ref, vt_ref, *, scale):
    ml = ml_ref[...]
    ang = pos_ref[...].astype(F32) * freq_ref[...]
    cosf = jnp.cos(ang)
    sinf = jnp.sin(ang)
    cq = _rms(ml[:, :Q_LORA], qn_ref[...]).astype(BF16)
    ckv = _rms(ml[:, Q_LORA:Q_LORA + KV_LORA], kvn_ref[...]).astype(BF16)
    kr = ml[:, Q_LORA + KV_LORA:Q_LORA + KV_LORA + 128]
    krr = ml[:, Q_LORA + KV_LORA + 128:Q_LORA + KV_LORA + 256]
    k_rope = kr * cosf + krr * sinf
    qf = _dot(cq, wq_ref[...])
    qr = _dot(cq, wqr_ref[...])
    kf = _dot(ckv, wk_ref[...])
    for h in range(MLA_HEADS):
        hs = slice(h * HEAD_PAD, (h + 1) * HEAD_PAD)
        q_ref[h] = ((qf[:, hs] * cosf + qr[:, hs] * sinf) * scale).astype(q_ref.dtype)
        k_ref[h] = (kf[:, hs] + k_rope).astype(k_ref.dtype)
        vt_ref[h] = _dot_nt(wvt_ref[h], ckv).astype(vt_ref.dtype)


def _mla_prep(ml, pos, freq, qn, kvn, wq, wqr, wk, wvt, batch, seq, tk):
    nk = seq // tk
    scale = (NOPE_DIM + ROPE_DIM) ** -0.5 * math.log2(math.e)
    full = lambda shape: pl.BlockSpec(shape, lambda b, t: (0,) * len(shape))
    return pl.pallas_call(
        functools.partial(_mla_prep_kernel, scale=scale),
        grid=(batch, nk),
        in_specs=[
            pl.BlockSpec((tk, ML_COLS), lambda b, t: (b * nk + t, 0)),
            pl.BlockSpec((tk, 1), lambda b, t: (b * nk + t, 0)),
            full((1, HEAD_PAD)), full((1, Q_LORA)), full((1, KV_LORA)),
            full((Q_LORA, MLA_HEADS * HEAD_PAD)), full((Q_LORA, MLA_HEADS * HEAD_PAD)),
            full((KV_LORA, MLA_HEADS * HEAD_PAD)), full((MLA_HEADS, V_DIM, KV_LORA)),
        ],
        out_specs=[
            pl.BlockSpec((None, MLA_HEADS, None, tk, HEAD_PAD), lambda b, t: (b, 0, t, 0, 0)),
            pl.BlockSpec((None, MLA_HEADS, None, tk, HEAD_PAD), lambda b, t: (b, 0, t, 0, 0)),
            pl.BlockSpec((None, MLA_HEADS, None, V_DIM, tk), lambda b, t: (b, 0, t, 0, 0)),
        ],
        out_shape=[
            jax.ShapeDtypeStruct((batch, MLA_HEADS, nk, tk, HEAD_PAD), BF16),
            jax.ShapeDtypeStruct((batch, MLA_HEADS, nk, tk, HEAD_PAD), BF16),
            jax.ShapeDtypeStruct((batch, MLA_HEADS, nk, V_DIM, tk), BF16),
        ],
        compiler_params=_cparams(("arbitrary", "arbitrary")),
        name="mla_prep",
    )(ml, pos, freq, qn, kvn, wq, wqr, wk, wvt)


def _attn_kernel(q_ref, k_ref, vt_ref, o_ref, *, tk, heads, ks, ws, ahead, dlag, behind):
    i = pl.program_id(2)
    krow = lax.broadcasted_iota(jnp.int32, (ks, ws), 0)
    qcol = lax.broadcasted_iota(jnp.int32, (ks, ws), 1)
    neg = jnp.finfo(F32).min
    nsub, nstrip = tk // ks, tk // ws
    items = [(h, g) for h in range(heads) for g in range(nstrip)]
    rng = range(len(items))
    qs = [q_ref[h, 0, g * ws:(g + 1) * ws, :] for h, g in items]

    def step(j, carry, masked):
        m = [carry[3 * i] for i in rng]
        l = [carry[3 * i + 1] for i in rng]
        acc = [carry[3 * i + 2] for i in rng]
        units = [(sub, i) for sub in range(nsub) for i in rng
                 if not (masked and sub * ks > items[i][1] * ws + ws - 1)]
        s_val, p_val, pv_val, al_val = {}, {}, {}, {}
        for t in range(len(units) + ahead + dlag + behind):
            if t < len(units):
                sub, i = units[t]
                s_val[t] = _dot_nt(k_ref[items[i][0], j, sub * ks:(sub + 1) * ks, :], qs[i])
            u = t - ahead
            if 0 <= u < len(units):
                sub, i = units[u]
                s = s_val.pop(u)
                if masked and sub * ks + ks - 1 > items[i][1] * ws:
                    s = jnp.where(krow + (sub * ks - items[i][1] * ws) <= qcol, s, neg)
                m_new = jnp.maximum(m[i], jnp.max(s, axis=0, keepdims=True))
                al_val[u] = jnp.exp2(m[i] - m_new)
                p = jnp.exp2(s - m_new)
                l[i] = al_val[u] * l[i] + jnp.sum(p, axis=0, keepdims=True)
                m[i] = m_new
                p_val[u] = p.astype(BF16)
            d = u - dlag
            if 0 <= d < len(units):
                sub, i = units[d]
                pv_val[d] = _dot(vt_ref[items[i][0], j, :, sub * ks:(sub + 1) * ks], p_val.pop(d))
            w = d - behind
            if 0 <= w < len(units):
                i = units[w][1]
                acc[i] = al_val.pop(w) * acc[i] + pv_val.pop(w)
        return tuple(x for i in rng for x in (m[i], l[i], acc[i]))

    init = tuple(x for _ in rng for x in (jnp.full((1, ws), neg, F32), jnp.zeros((1, ws), F32),
                                          jnp.zeros((V_DIM, ws), F32)))
    carry = lax.fori_loop(0, i, lambda j, c: step(j, c, False), init)
    final = step(i, carry, True)
    outs = []
    for h in range(heads):
        acc = jnp.concatenate([final[3 * (h * nstrip + g) + 2] / final[3 * (h * nstrip + g) + 1]
                               for g in range(nstrip)], axis=1)
        outs.append(acc.T)
    o_ref[...] = jnp.concatenate(outs, axis=-1).astype(o_ref.dtype)


def _attn(q5, k5, vt5, batch, seq, tk):
    nk = seq // tk
    heads = 8
    ngrp = MLA_HEADS // heads
    return pl.pallas_call(
        functools.partial(_attn_kernel, tk=tk, heads=heads, ks=min(256, tk), ws=min(256, tk), ahead=3, dlag=2, behind=2),
        grid=(batch, ngrp, nk),
        in_specs=[
            pl.BlockSpec((None, heads, 1, tk, HEAD_PAD), lambda b, g, i: (b, g, i, 0, 0)),
            pl.BlockSpec((None, heads, nk, tk, HEAD_PAD), lambda b, g, i: (b, g, 0, 0, 0)),
            pl.BlockSpec((None, heads, nk, V_DIM, tk), lambda b, g, i: (b, g, 0, 0, 0)),
        ],
        out_specs=pl.BlockSpec((tk, heads * V_DIM), lambda b, g, i: (b * nk + i, g)),
        out_shape=jax.ShapeDtypeStruct((batch * seq, MLA_HEADS * V_DIM), BF16),
        compiler_params=_cparams(("arbitrary", "arbitrary", "arbitrary")),
        name="mla_attn",
    )(q5, k5, vt5)


def _tail_kernel(x_ref, yr_ref, o_ref, gt_ref, wor_ref, wom_ref, wout_ref, nf_ref, wup_ref, wdn_ref,
                 nfin_ref, out_ref):
    y_a = _dot(yr_ref[...], wor_ref[...])
    y_b = _dot(o_ref[...], wom_ref[...])
    merged = _sigmoid(gt_ref[:, :D_MODEL]) * y_a + _sigmoid(gt_ref[:, D_MODEL:]) * y_b
    h = x_ref[...] + _dot(merged.astype(BF16), wout_ref[...])
    f_in = _rms(h, nf_ref[...]).astype(BF16)
    cw = 1024
    acc = h
    for c in range(D_FF // cw):
        f = _dot(f_in, wup_ref[:, c * cw:(c + 1) * cw])
        f = jnp.square(jnp.maximum(f, 0.0)).astype(BF16)
        acc = acc + _dot(f, wdn_ref[c * cw:(c + 1) * cw, :])
    out_ref[...] = _rms(acc, nfin_ref[...])


def _tail(x2, yr, o, gt, wor, wom, wout, nf, wup, wdn, nfin, seq):
    t_tok = x2.shape[0]
    tm = min(256, seq)
    full = lambda shape: pl.BlockSpec(shape, lambda i: (0,) * len(shape))
    return pl.pallas_call(
        _tail_kernel,
        grid=(t_tok // tm,),
        in_specs=[
            pl.BlockSpec((tm, D_MODEL), lambda i: (i, 0)),
            pl.BlockSpec((tm, RWKV_DIM), lambda i: (i, 0)),
            pl.BlockSpec((tm, MLA_HEADS * V_DIM), lambda i: (i, 0)),
            pl.BlockSpec((tm, GT_COLS), lambda i: (i, 0)),
            full((RWKV_DIM, D_MODEL)), full((MLA_HEADS * V_DIM, D_MODEL)), full((D_MODEL, D_MODEL)),
            full((1, D_MODEL)), full((D_MODEL, D_FF)), full((D_FF, D_MODEL)), full((1, D_MODEL)),
        ],
        out_specs=pl.BlockSpec((tm, D_MODEL), lambda i: (i, 0)),
        out_shape=jax.ShapeDtypeStruct((t_tok, D_MODEL), F32),
        compiler_params=_cparams(("arbitrary",)),
        name="merge_ffn",
    )(x2, yr, o, gt, wor, wom, wout, nf, wup, wdn, nfin)


def _padc(w, n):
    return jnp.pad(w, ((0, 0), (0, n - w.shape[1])))


def _padr(w, n):
    return jnp.pad(w, ((0, n - w.shape[0]), (0, 0)))


def _rot_half(w):
    half = w.shape[-1] // 2
    return jnp.concatenate([-w[..., half:], w[..., :half]], axis=-1)


def kernel(x, positions, norm_mix, w_in, mu_shift, w0, w_up, a0, a_up, g_up, k_k, k_a, r_k, ln_w, ln_b, w_o_rwkv, q_norm, w_uq, kv_norm, w_ukv, w_o_mla, w_out, norm_ffn, w_ff_up, w_ff_down, norm_final):
    batch, seq, _ = x.shape
    t_tok = batch * seq
    x2 = x.reshape(t_tok, D_MODEL)
    l = 0

    wi = w_in[l]
    o = 0
    w_r3 = wi[:, o:o + 3 * RWKV_DIM]; o += 3 * RWKV_DIM
    w_zw = wi[:, o:o + DECAY_LORA]; o += DECAY_LORA
    w_za = wi[:, o:o + ICLR_LORA]; o += ICLR_LORA
    w_zg = wi[:, o:o + GATE_LORA]; o += GATE_LORA
    w_cq = wi[:, o:o + Q_LORA]; o += Q_LORA
    w_ckv = wi[:, o:o + KV_LORA]; o += KV_LORA
    w_kr = wi[:, o:o + ROPE_DIM]; o += ROPE_DIM
    w_gate = wi[:, o:o + 2 * D_MODEL]
    zeros64 = jnp.zeros((D_MODEL, NOPE_DIM), wi.dtype)
    w_kr_p = _padc(jnp.concatenate([zeros64, w_kr], axis=1), 128)
    w_krr_p = _padc(jnp.concatenate([zeros64, _rot_half(w_kr)], axis=1), 128)
    w_a = jnp.concatenate([
        w_r3, _padc(w_zw, 128), _padc(w_za, 128), _padc(w_zg, 256),
        w_cq, w_ckv, w_kr_p, w_krr_p, w_gate], axis=1).astype(BF16)
    mu = mu_shift[l]
    o = 3 * RWKV_DIM
    mu_a = jnp.concatenate([
        mu[:o], jnp.pad(mu[o:o + 64], (0, 64)), jnp.pad(mu[o + 64:o + 128], (0, 64)),
        jnp.pad(mu[o + 128:], (0, 96))])[None, :]

    rw, ml, gt = _inproj(x2, norm_mix[l][None, :], w_a, mu_a, seq)

    prm = [p[None, :] for p in (w0[l], a0[l], k_k[l], k_a[l], r_k[l].reshape(-1), ln_w[l], ln_b[l])]
    yr = _rwkv(rw, prm, _padr(w_up[l], 128).astype(BF16), _padr(a_up[l], 128).astype(BF16),
               _padr(g_up[l], 256).astype(BF16), batch, seq)

    half = ROPE_DIM // 2
    inv_freq = 1.0 / (ROPE_THETA ** (jnp.arange(half, dtype=F32) * (2.0 / ROPE_DIM)))
    freq = jnp.concatenate([jnp.zeros((NOPE_DIM,), F32), inv_freq, inv_freq,
                            jnp.zeros((HEAD_PAD - NOPE_DIM - ROPE_DIM,), F32)])[None, :]
    wq = w_uq[l].reshape(Q_LORA, MLA_HEADS, NOPE_DIM + ROPE_DIM)
    wq_p = jnp.pad(wq, ((0, 0), (0, 0), (0, HEAD_PAD - NOPE_DIM - ROPE_DIM)))
    wq_rot = jnp.concatenate([jnp.zeros_like(wq[..., :NOPE_DIM]), _rot_half(wq[..., NOPE_DIM:])], axis=-1)
    wq_rot_p = jnp.pad(wq_rot, ((0, 0), (0, 0), (0, HEAD_PAD - NOPE_DIM - ROPE_DIM)))
    wkv = w_ukv[l].reshape(KV_LORA, MLA_HEADS, NOPE_DIM + V_DIM)
    wk_p = jnp.pad(wkv[..., :NOPE_DIM], ((0, 0), (0, 0), (0, HEAD_PAD - NOPE_DIM)))
    wvt = jnp.transpose(wkv[..., NOPE_DIM:], (1, 2, 0))
    tk = min(512, seq)
    q5, k5, vt5 = _mla_prep(
        ml, positions.reshape(t_tok, 1), freq, q_norm[l][None, :], kv_norm[l][None, :],
        wq_p.reshape(Q_LORA, -1).astype(BF16), wq_rot_p.reshape(Q_LORA, -1).astype(BF16),
        wk_p.reshape(KV_LORA, -1).astype(BF16), wvt.astype(BF16), batch, seq, tk)
    o_att = _attn(q5, k5, vt5, batch, seq, tk)

    out = _tail(x2, yr, o_att, gt, w_o_rwkv[l].astype(BF16), w_o_mla[l].astype(BF16),
                w_out[l].astype(BF16), norm_ffn[l][None, :], w_ff_up[l].astype(BF16),
                w_ff_down[l].astype(BF16), norm_final[None, :], seq)
    return out.reshape(batch, seq, D_MODEL)
```

```python
import functools
import math

import jax
import jax.numpy as jnp
from jax import lax
from jax.experimental import pallas as pl
from jax.experimental.pallas import tpu as pltpu

F32 = jnp.float32
BF16 = jnp.bfloat16

D_MODEL = 1024
NORM_EPS = 1e-6
RWKV_HEAD = 64
RWKV_HEADS = 8
RWKV_DIM = RWKV_HEADS * RWKV_HEAD
DECAY_LORA = 64
ICLR_LORA = 64
GATE_LORA = 160
GN_EPS = 64e-5
MLA_HEADS = 8
Q_LORA = 256
KV_LORA = 128
NOPE_DIM = 64
ROPE_DIM = 32
V_DIM = 64
ROPE_THETA = 10000.0
D_FF = 4 * D_MODEL

LANE = 128
CHUNK = 64
PAIR = 2 * RWKV_HEAD
HEAD_PAD = 128
V_EXT = V_DIM + 16

RW_COLS = 3 * RWKV_DIM + 128 + 128 + 256
ML_COLS = Q_LORA + KV_LORA + 128 + 128
GT_COLS = 2 * D_MODEL
VMEM_LIMIT = 56 * 1024 * 1024


def _cparams(sem):
    return pltpu.CompilerParams(dimension_semantics=sem, vmem_limit_bytes=VMEM_LIMIT)


def _sigmoid(x):
    return 1.0 / (1.0 + jnp.exp(-x))


def _rms(x, g):
    ms = jnp.mean(x * x, axis=-1, keepdims=True)
    return x * lax.rsqrt(ms + NORM_EPS) * g


def _dot(a, b):
    return jnp.dot(a, b, preferred_element_type=F32)


def _dot_nt(a, b):
    return lax.dot_general(a, b, (((1,), (1,)), ((), ())), preferred_element_type=F32)


def _dot_tn(a, b):
    return lax.dot_general(a, b, (((0,), (0,)), ((), ())), preferred_element_type=F32)


def _inproj_kernel(x_ref, g_ref, w_ref, mu_ref, rw_ref, ml_ref, gt_ref, carry_ref,
                   *, tiles_per_seq, tm):
    i = pl.program_id(0)
    u = _rms(x_ref[...], g_ref[...]).astype(BF16)

    @pl.when(i % tiles_per_seq == 0)
    def _():
        carry_ref[...] = jnp.zeros_like(carry_ref)

    cw = 512
    row0 = lax.broadcasted_iota(jnp.int32, (tm, cw), 0) == 0
    for c in range(RW_COLS // cw):
        cs = slice(c * cw, (c + 1) * cw)
        z = _dot(u, w_ref[:, cs])
        prev = pltpu.roll(z, 1, 0)
        prev = jnp.where(row0, carry_ref[7:8, cs], prev)
        carry_ref[:, cs] = z[tm - 8:tm, :]
        rw_ref[:, cs] = z + (prev - z) * mu_ref[:, cs]
    ml_ref[...] = _dot(u, w_ref[:, RW_COLS:RW_COLS + ML_COLS])
    for c in range(GT_COLS // cw):
        gt_ref[:, c * cw:(c + 1) * cw] = _dot(
            u, w_ref[:, RW_COLS + ML_COLS + c * cw:RW_COLS + ML_COLS + (c + 1) * cw])


def _inproj(x2, g, w_a, mu_a, seq):
    t_tok = x2.shape[0]
    tm = min(512, seq)
    ncol = RW_COLS + ML_COLS + GT_COLS
    kern = functools.partial(_inproj_kernel, tiles_per_seq=seq // tm, tm=tm)
    return pl.pallas_call(
        kern,
        grid=(t_tok // tm,),
        in_specs=[
            pl.BlockSpec((tm, D_MODEL), lambda i: (i, 0)),
            pl.BlockSpec((1, D_MODEL), lambda i: (0, 0)),
            pl.BlockSpec((D_MODEL, ncol), lambda i: (0, 0)),
            pl.BlockSpec((1, RW_COLS), lambda i: (0, 0)),
        ],
        out_specs=[
            pl.BlockSpec((tm, RW_COLS), lambda i: (i, 0)),
            pl.BlockSpec((tm, ML_COLS), lambda i: (i, 0)),
            pl.BlockSpec((tm, GT_COLS), lambda i: (i, 0)),
        ],
        out_shape=[
            jax.ShapeDtypeStruct((t_tok, RW_COLS), F32),
            jax.ShapeDtypeStruct((t_tok, ML_COLS), F32),
            jax.ShapeDtypeStruct((t_tok, GT_COLS), F32),
        ],
        scratch_shapes=[pltpu.VMEM((8, RW_COLS), F32)],
        compiler_params=_cparams(("arbitrary",)),
        name="inproj",
    )(x2, g, w_a, mu_a)


def _split3(x):
    hi = x.astype(BF16)
    r1 = x - hi.astype(F32)
    mid = r1.astype(BF16)
    lo = (r1 - mid.astype(F32)).astype(BF16)
    return hi, mid, lo


def _rwkv_masks():
    n = 2 * CHUNK
    row = lax.broadcasted_iota(jnp.int32, (n, n), 0)
    col = lax.broadcasted_iota(jnp.int32, (n, n), 1)
    same = (row // CHUNK) == (col // CHUNK)
    strict = jnp.where(same & (row > col), 1.0, 0.0).astype(F32)
    incl = jnp.where(same & (row >= col), 1.0, 0.0).astype(F32)
    m16 = jnp.where((row // 16) == (col // 16), 1.0, 0.0).astype(F32)
    m32 = jnp.where(((row // 32) == (col // 32)) & ((row // 16) > (col // 16)), 1.0, 0.0).astype(F32)
    m64 = jnp.where((row // 32) > (col // 32), 1.0, 0.0).astype(F32)
    eye = jnp.where(row == col, 1.0, 0.0).astype(F32)
    headsel = (row // CHUNK) == (col // RWKV_HEAD)
    return strict, incl, m16, m32, m64, eye, headsel


def _unit_lower_inverse(a_list, m16, m32, m64, eye):
    n = 2 * CHUNK
    bf = lambda x: x.astype(BF16)
    ds = [bf(a * m16) for a in a_list]
    d2 = [_dot(d, d) for d in ds]
    ts = [eye + a * m16 for a in a_list]
    xs = [_dot(bf(jnp.concatenate([t, p], axis=0)), bf(p)) for t, p in zip(ts, d2)]
    ts = [t + x[:n] for t, x in zip(ts, xs)]
    d4 = [x[n:] for x in xs]
    xs = [_dot(bf(jnp.concatenate([t, p], axis=0)), bf(p)) for t, p in zip(ts, d4)]
    ts = [t + x[:n] for t, x in zip(ts, xs)]
    d8 = [x[n:] for x in xs]
    ts = [t + _dot(bf(t), bf(p)) for t, p in zip(ts, d8)]
    for msk in (m32, m64):
        tb = [bf(t) for t in ts]
        xs = [bf(_dot(t, bf(a * msk))) for t, a in zip(tb, a_list)]
        ts = [t + _dot(x, t16) for t, x, t16 in zip(ts, xs, tb)]
    return ts


def _rwkv_chunks(streams, h_list, masks, tril, nchunk):
    strict, incl, m16, m32, m64, eye, headsel = masks
    n = 2 * CHUNK
    bf = lambda x: x.astype(BF16)
    items = [(q, slice(c * CHUNK, (c + 1) * CHUNK)) for c in range(nchunk) for q in range(len(streams))]
    rng = range(len(items))
    col = lambda j: [streams[q][j][s] for q, s in items]
    r, k, v, a, b, lw = (col(j) for j in range(6))

    def stack(x):
        return jnp.where(headsel, jnp.concatenate([x, x], axis=0), 0.0)

    cs = []
    for i in rng:
        hi, mid, lo = _split3(lw[i])
        cs.append(_dot(tril, hi) + _dot(tril, mid) + _dot(tril, lo))
    g_in = [jnp.exp(cs[i]) for i in rng]
    g_ex = [jnp.exp(cs[i] - lw[i]) for i in rng]
    g_inv = [jnp.exp(-cs[i]) for i in rng]
    g_last = [jnp.exp(cs[i][CHUNK - 1:CHUNK, :]) for i in rng]

    rt = [stack(r[i] * g_in[i]) for i in rng]
    at = [bf(stack(a[i] * g_ex[i])) for i in rng]
    bt = [stack(b[i] * g_inv[i]) for i in rng]
    kt = [stack(k[i] * g_inv[i]) for i in rng]
    bh = [bf(bt[i] * g_last[i]) for i in rng]
    kh = [bf(kt[i] * g_last[i]) for i in rng]
    v2 = [bf(stack(v[i])) for i in rng]

    bk = [bf(jnp.concatenate([bt[i], kt[i]], axis=0)) for i in rng]
    ga = [_dot_nt(at[i], bk[i]) for i in rng]
    gr = [_dot_nt(bf(rt[i]), bk[i]) for i in rng]
    a_ab = [g[:, :n] * strict for g in ga]
    a_ak = [g[:, n:] * strict for g in ga]
    a_rb = [bf(g[:, :n] * incl) for g in gr]
    a_rk = [g[:, n:] * incl for g in gr]
    t = _unit_lower_inverse(a_ab, m16, m32, m64, eye)
    xv = [_dot(bf(jnp.concatenate([a_ak[i], a_rk[i]], axis=0)), v2[i]) for i in rng]
    wu = [bf(_dot(bf(t[i]), jnp.concatenate([at[i], bf(xv[i][:n])], axis=1))) for i in rng]
    qy = [_dot(a_rb[i], wu[i]) for i in rng]
    qeff = [rt[i] + qy[i][:, :n] for i in rng]
    y0 = [qy[i][:, n:] + xv[i][n:] for i in rng]
    mn = [_dot_tn(bh[i], wu[i]) for i in rng]
    m = [eye * g_last[i] + mn[i][:, :n] for i in rng]
    nn = [mn[i][:, n:] + _dot_tn(kh[i], v2[i]) for i in rng]
    mq = [bf(jnp.concatenate([m[i], qeff[i]], axis=0)) for i in rng]
    h_list = list(h_list)
    ys = [[] for _ in streams]
    for i in rng:
        q = items[i][0]
        hy = _dot(mq[i], bf(h_list[q]))
        h_list[q] = hy[:n] + nn[i]
        y2 = hy[n:] + y0[i]
        ys[q].append(y2[:CHUNK, :] + y2[CHUNK:, :])
    return [jnp.concatenate(y, axis=0) for y in ys], h_list


def _rwkv_kernel(zr_ref, zk_ref, zv_ref, zw_ref, za_ref, zg_ref,
                 w0_ref, a0_ref, kk_ref, ka_ref, rk_ref, lnw_ref, lnb_ref,
                 wup_ref, aup_ref, gup_ref, y_ref, h_ref, *, tb, npair):
    @pl.when(pl.program_id(2) == 0)
    def _():
        h_ref[...] = jnp.zeros_like(h_ref)

    lane = lax.broadcasted_iota(jnp.int32, (tb, PAIR), 1)
    head0 = lane < RWKV_HEAD

    def head_sum(x):
        s0 = jnp.sum(jnp.where(head0, x, 0.0), axis=-1, keepdims=True)
        s1 = jnp.sum(jnp.where(head0, 0.0, x), axis=-1, keepdims=True)
        return jnp.where(head0, s0, s1)

    tanh_zw = jnp.tanh(zw_ref[...]).astype(BF16)
    za = za_ref[...].astype(BF16)
    sig_zg = _sigmoid(zg_ref[...]).astype(BF16)
    streams, keep = [], []
    for q in range(npair):
        ls = slice(q * PAIR, (q + 1) * PAIR)
        zr = zr_ref[:, ls]
        zk = zk_ref[:, ls]
        zv = zv_ref[:, ls]
        w_pre = w0_ref[:, ls] + _dot(tanh_zw, wup_ref[:, ls])
        lw = -math.exp(-0.5) * _sigmoid(w_pre)
        iclr = _sigmoid(a0_ref[:, ls] + _dot(za, aup_ref[:, ls]))
        gate = _dot(sig_zg, gup_ref[:, ls])
        kk = zk * kk_ref[:, ls]
        kk = kk / jnp.maximum(jnp.sqrt(head_sum(kk * kk)), 1e-12)
        k = zk * (1.0 + (iclr - 1.0) * ka_ref[:, ls])
        streams.append((zr, k, zv, -kk, kk * iclr, lw))
        keep.append((zr, k, zv, gate))

    masks = _rwkv_masks()
    trow = lax.broadcasted_iota(jnp.int32, (CHUNK, CHUNK), 0)
    tcol = lax.broadcasted_iota(jnp.int32, (CHUNK, CHUNK), 1)
    tril = jnp.where(trow >= tcol, 1.0, 0.0).astype(BF16)

    ys, hs = _rwkv_chunks(streams, [h_ref[q] for q in range(npair)], masks, tril, tb // CHUNK)
    for q in range(npair):
        ls = slice(q * PAIR, (q + 1) * PAIR)
        h_ref[q] = hs[q]
        zr, k, zv, gate = keep[q]
        y = ys[q]
        mean = head_sum(y) * (1.0 / RWKV_HEAD)
        yc = y - mean
        var = head_sum(yc * yc) * (1.0 / RWKV_HEAD)
        yn = yc * lax.rsqrt(var + GN_EPS) * lnw_ref[:, ls] + lnb_ref[:, ls]
        bonus = head_sum(zr * k * rk_ref[:, ls]) * zv
        y_ref[:, ls] = ((yn + bonus) * gate).astype(y_ref.dtype)


def _rwkv(rw, prm, wup, aup, gup, batch, seq):
    t_tok = rw.shape[0]
    tb = min(256, seq)
    nt = seq // tb
    npair = 4
    wid = npair * PAIR
    ngrp = RWKV_DIM // wid

    def tok(base):
        return pl.BlockSpec((tb, wid), lambda b, p, t: (b * nt + t, base * ngrp + p))

    def lora(base, width):
        return pl.BlockSpec((tb, width), lambda b, p, t: (b * nt + t, base))

    prm_spec = pl.BlockSpec((1, wid), lambda b, p, t: (0, p))
    in_specs = [
        tok(0), tok(1), tok(2),
        lora(3 * RWKV_DIM // 128, 128), lora(3 * RWKV_DIM // 128 + 1, 128),
        lora((3 * RWKV_DIM + 256) // 256, 256),
    ] + [prm_spec] * 7 + [
        pl.BlockSpec((128, wid), lambda b, p, t: (0, p)),
        pl.BlockSpec((128, wid), lambda b, p, t: (0, p)),
        pl.BlockSpec((256, wid), lambda b, p, t: (0, p)),
    ]
    return pl.pallas_call(
        functools.partial(_rwkv_kernel, tb=tb, npair=npair),
        grid=(batch, ngrp, nt),
        in_specs=in_specs,
        out_specs=pl.BlockSpec((tb, wid), lambda b, p, t: (b * nt + t, p)),
        out_shape=jax.ShapeDtypeStruct((t_tok, RWKV_DIM), BF16),
        scratch_shapes=[pltpu.VMEM((npair, PAIR, PAIR), F32)],
        compiler_params=_cparams(("arbitrary", "arbitrary", "arbitrary")),
        name="rwkv7",
    )(rw, rw, rw, rw, rw, rw, *prm, wup, aup, gup)


def _mla_prep_kernel(ml_ref, pos_ref, freq_ref, qn_ref, kvn_ref, wq_ref, wqr_ref, wk_ref, wvt_ref,
                     q_ref, k_ref, vt_ref, *, scale, tk):
    ml = ml_ref[...]
    nblk = LANE // (ROPE_DIM // 2)
    rows = tk // nblk
    lane = lax.broadcasted_iota(jnp.int32, (rows, LANE), 1)
    pos = pos_ref[...].astype(F32)
    pos8 = jnp.zeros((rows, LANE), F32)
    for e in range(nblk):
        pos8 = jnp.where(lane // (ROPE_DIM // 2) == e, pos[e * rows:(e + 1) * rows], pos8)
    ang8 = pos8 * freq_ref[...]
    cos8 = jnp.cos(ang8)
    sin8 = jnp.sin(ang8)
    lo = (lane >= NOPE_DIM) & (lane < NOPE_DIM + ROPE_DIM // 2)
    hi = (lane >= NOPE_DIM + ROPE_DIM // 2) & (lane < NOPE_DIM + ROPE_DIM)

    def expand(t8, fill):
        blocks = []
        for e in range(nblk):
            a = pltpu.roll(t8, (NOPE_DIM - e * (ROPE_DIM // 2)) % LANE, 1)
            b = pltpu.roll(t8, (NOPE_DIM + ROPE_DIM // 2 - e * (ROPE_DIM // 2)) % LANE, 1)
            blocks.append(jnp.where(lo, a, jnp.where(hi, b, fill)))
        return jnp.concatenate(blocks, axis=0)

    cosf = expand(cos8, 1.0)
    sinf = expand(sin8, 0.0)
    cq = _rms(ml[:, :Q_LORA], qn_ref[...]).astype(BF16)
    ckv = _rms(ml[:, Q_LORA:Q_LORA + KV_LORA], kvn_ref[...]).astype(BF16)
    kr = ml[:, Q_LORA + KV_LORA:Q_LORA + KV_LORA + 128]
    krr = ml[:, Q_LORA + KV_LORA + 128:Q_LORA + KV_LORA + 256]
    k_rope = kr * cosf + krr * sinf
    qf = _dot(cq, wq_ref[...])
    qr = _dot(cq, wqr_ref[...])
    kf = _dot(ckv, wk_ref[...])
    for h in range(MLA_HEADS):
        hs = slice(h * HEAD_PAD, (h + 1) * HEAD_PAD)
        q_ref[h] = ((qf[:, hs] * cosf + qr[:, hs] * sinf) * scale).astype(q_ref.dtype)
        k_ref[h] = (kf[:, hs] + k_rope).astype(k_ref.dtype)
        vt_ref[h, :V_DIM] = _dot_nt(wvt_ref[h], ckv).astype(vt_ref.dtype)
        vt_ref[h, V_DIM:] = jnp.ones((V_EXT - V_DIM, tk), vt_ref.dtype)


def _mla_prep(ml, pos, freq, qn, kvn, wq, wqr, wk, wvt, batch, seq, tk):
    nk = seq // tk
    scale = (NOPE_DIM + ROPE_DIM) ** -0.5 * math.log2(math.e)
    full = lambda shape: pl.BlockSpec(shape, lambda b, t: (0,) * len(shape))
    return pl.pallas_call(
        functools.partial(_mla_prep_kernel, scale=scale, tk=tk),
        grid=(batch, nk),
        in_specs=[
            pl.BlockSpec((tk, ML_COLS), lambda b, t: (b * nk + t, 0)),
            pl.BlockSpec((tk, 1), lambda b, t: (b * nk + t, 0)),
            full((1, HEAD_PAD)), full((1, Q_LORA)), full((1, KV_LORA)),
            full((Q_LORA, MLA_HEADS * HEAD_PAD)), full((Q_LORA, MLA_HEADS * HEAD_PAD)),
            full((KV_LORA, MLA_HEADS * HEAD_PAD)), full((MLA_HEADS, V_DIM, KV_LORA)),
        ],
        out_specs=[
            pl.BlockSpec((None, MLA_HEADS, None, tk, HEAD_PAD), lambda b, t: (b, 0, t, 0, 0)),
            pl.BlockSpec((None, MLA_HEADS, None, tk, HEAD_PAD), lambda b, t: (b, 0, t, 0, 0)),
            pl.BlockSpec((None, MLA_HEADS, None, V_EXT, tk), lambda b, t: (b, 0, t, 0, 0)),
        ],
        out_shape=[
            jax.ShapeDtypeStruct((batch, MLA_HEADS, nk, tk, HEAD_PAD), BF16),
            jax.ShapeDtypeStruct((batch, MLA_HEADS, nk, tk, HEAD_PAD), BF16),
            jax.ShapeDtypeStruct((batch, MLA_HEADS, nk, V_EXT, tk), BF16),
        ],
        compiler_params=_cparams(("arbitrary", "arbitrary")),
        name="mla_prep",
    )(ml, pos, freq, qn, kvn, wq, wqr, wk, wvt)


def _attn_kernel(q_ref, k_ref, vt_ref, o_ref, *, tk, heads, ks, ws, ahead, dlag, behind):
    i = pl.program_id(2)
    krow = lax.broadcasted_iota(jnp.int32, (ks, ws), 0)
    qcol = lax.broadcasted_iota(jnp.int32, (ks, ws), 1)
    neg = jnp.finfo(F32).min
    nsub, nstrip = tk // ks, tk // ws
    items = [(h, g) for h in range(heads) for g in range(nstrip)]
    rng = range(len(items))
    qs = [q_ref[h, 0, g * ws:(g + 1) * ws, :] for h, g in items]

    def step(j, carry, masked):
        m = [carry[2 * i] for i in rng]
        acc = [carry[2 * i + 1] for i in rng]
        units = [(sub, i) for sub in range(nsub) for i in rng
                 if not (masked and sub * ks > items[i][1] * ws + ws - 1)]
        s_val, p_val, pv_val, al_val = {}, {}, {}, {}
        for t in range(len(units) + ahead + dlag + behind):
            if t < len(units):
                sub, i = units[t]
                s_val[t] = _dot_nt(k_ref[items[i][0], j, sub * ks:(sub + 1) * ks, :], qs[i])
            u = t - ahead
            if 0 <= u < len(units):
                sub, i = units[u]
                s = s_val.pop(u)
                if masked and sub * ks + ks - 1 > items[i][1] * ws:
                    s = jnp.where(krow + (sub * ks - items[i][1] * ws) <= qcol, s, neg)
                m_new = jnp.maximum(m[i], jnp.max(s, axis=0, keepdims=True))
                al_val[u] = jnp.exp2(m[i] - m_new)
                m[i] = m_new
                p_val[u] = jnp.exp2(s - m_new).astype(BF16)
            d = u - dlag
            if 0 <= d < len(units):
                sub, i = units[d]
                pv_val[d] = _dot(vt_ref[items[i][0], j, :, sub * ks:(sub + 1) * ks], p_val.pop(d))
            w = d - behind
            if 0 <= w < len(units):
                i = units[w][1]
                acc[i] = al_val.pop(w) * acc[i] + pv_val.pop(w)
        return tuple(x for i in rng for x in (m[i], acc[i]))

    init = tuple(x for _ in rng for x in (jnp.full((1, ws), neg, F32), jnp.zeros((V_EXT, ws), F32)))
    carry = lax.fori_loop(0, i, lambda j, c: step(j, c, False), init)
    final = step(i, carry, True)
    outs = []
    for h in range(heads):
        accs = [final[2 * (h * nstrip + g) + 1] for g in range(nstrip)]
        acc = jnp.concatenate([a[:V_DIM] / a[V_DIM:V_DIM + 1] for a in accs], axis=1)
        outs.append(acc.T)
    o_ref[...] = jnp.concatenate(outs, axis=-1).astype(o_ref.dtype)


def _attn(q5, k5, vt5, batch, seq, tk):
    nk = seq // tk
    heads = 8
    ngrp = MLA_HEADS // heads
    return pl.pallas_call(
        functools.partial(_attn_kernel, tk=tk, heads=heads, ks=min(256, tk), ws=min(256, tk), ahead=3, dlag=2, behind=2),
        grid=(batch, ngrp, nk),
        in_specs=[
            pl.BlockSpec((None, heads, 1, tk, HEAD_PAD), lambda b, g, i: (b, g, i, 0, 0)),
            pl.BlockSpec((None, heads, nk, tk, HEAD_PAD), lambda b, g, i: (b, g, 0, 0, 0)),
            pl.BlockSpec((None, heads, nk, V_EXT, tk), lambda b, g, i: (b, g, 0, 0, 0)),
        ],
        out_specs=pl.BlockSpec((tk, heads * V_DIM), lambda b, g, i: (b * nk + i, g)),
        out_shape=jax.ShapeDtypeStruct((batch * seq, MLA_HEADS * V_DIM), BF16),
        compiler_params=_cparams(("arbitrary", "arbitrary", "arbitrary")),
        name="mla_attn",
    )(q5, k5, vt5)


def _tail_kernel(x_ref, yr_ref, o_ref, gt_ref, wor_ref, wom_ref, wout_ref, nf_ref, wup_ref, wdn_ref,
                 nfin_ref, out_ref):
    y_a = _dot(yr_ref[...], wor_ref[...])
    y_b = _dot(o_ref[...], wom_ref[...])
    merged = _sigmoid(gt_ref[:, :D_MODEL]) * y_a + _sigmoid(gt_ref[:, D_MODEL:]) * y_b
    h = x_ref[...] + _dot(merged.astype(BF16), wout_ref[...])
    f_in = _rms(h, nf_ref[...]).astype(BF16)
    cw = 1024
    acc = h
    for c in range(D_FF // cw):
        f = _dot(f_in, wup_ref[:, c * cw:(c + 1) * cw])
        f = jnp.square(jnp.maximum(f, 0.0)).astype(BF16)
        acc = acc + _dot(f, wdn_ref[c * cw:(c + 1) * cw, :])
    out_ref[...] = _rms(acc, nfin_ref[...])


def _tail(x2, yr, o, gt, wor, wom, wout, nf, wup, wdn, nfin, seq):
    t_tok = x2.shape[0]
    tm = min(256, seq)
    full = lambda shape: pl.BlockSpec(shape, lambda i: (0,) * len(shape))
    return pl.pallas_call(
        _tail_kernel,
        grid=(t_tok // tm,),
        in_specs=[
            pl.BlockSpec((tm, D_MODEL), lambda i: (i, 0)),
            pl.BlockSpec((tm, RWKV_DIM), lambda i: (i, 0)),
            pl.BlockSpec((tm, MLA_HEADS * V_DIM), lambda i: (i, 0)),
            pl.BlockSpec((tm, GT_COLS), lambda i: (i, 0)),
            full((RWKV_DIM, D_MODEL)), full((MLA_HEADS * V_DIM, D_MODEL)), full((D_MODEL, D_MODEL)),
            full((1, D_MODEL)), full((D_MODEL, D_FF)), full((D_FF, D_MODEL)), full((1, D_MODEL)),
        ],
        out_specs=pl.BlockSpec((tm, D_MODEL), lambda i: (i, 0)),
        out_shape=jax.ShapeDtypeStruct((t_tok, D_MODEL), F32),
        compiler_params=_cparams(("arbitrary",)),
        name="merge_ffn",
    )(x2, yr, o, gt, wor, wom, wout, nf, wup, wdn, nfin)


def _padc(w, n):
    return jnp.pad(w, ((0, 0), (0, n - w.shape[1])))


def _padr(w, n):
    return jnp.pad(w, ((0, n - w.shape[0]), (0, 0)))


def _rot_half(w):
    half = w.shape[-1] // 2
    return jnp.concatenate([-w[..., half:], w[..., :half]], axis=-1)


def kernel(x, positions, norm_mix, w_in, mu_shift, w0, w_up, a0, a_up, g_up, k_k, k_a, r_k, ln_w, ln_b, w_o_rwkv, q_norm, w_uq, kv_norm, w_ukv, w_o_mla, w_out, norm_ffn, w_ff_up, w_ff_down, norm_final):
    batch, seq, _ = x.shape
    t_tok = batch * seq
    x2 = x.reshape(t_tok, D_MODEL)
    l = 0

    wi = w_in[l]
    o = 0
    w_r3 = wi[:, o:o + 3 * RWKV_DIM]; o += 3 * RWKV_DIM
    w_zw = wi[:, o:o + DECAY_LORA]; o += DECAY_LORA
    w_za = wi[:, o:o + ICLR_LORA]; o += ICLR_LORA
    w_zg = wi[:, o:o + GATE_LORA]; o += GATE_LORA
    w_cq = wi[:, o:o + Q_LORA]; o += Q_LORA
    w_ckv = wi[:, o:o + KV_LORA]; o += KV_LORA
    w_kr = wi[:, o:o + ROPE_DIM]; o += ROPE_DIM
    w_gate = wi[:, o:o + 2 * D_MODEL]
    zeros64 = jnp.zeros((D_MODEL, NOPE_DIM), wi.dtype)
    w_kr_p = _padc(jnp.concatenate([zeros64, w_kr], axis=1), 128)
    w_krr_p = _padc(jnp.concatenate([zeros64, _rot_half(w_kr)], axis=1), 128)
    w_a = jnp.concatenate([
        w_r3, _padc(w_zw, 128), _padc(w_za, 128), _padc(w_zg, 256),
        w_cq, w_ckv, w_kr_p, w_krr_p, w_gate], axis=1).astype(BF16)
    mu = mu_shift[l]
    o = 3 * RWKV_DIM
    mu_a = jnp.concatenate([
        mu[:o], jnp.pad(mu[o:o + 64], (0, 64)), jnp.pad(mu[o + 64:o + 128], (0, 64)),
        jnp.pad(mu[o + 128:], (0, 96))])[None, :]

    rw, ml, gt = _inproj(x2, norm_mix[l][None, :], w_a, mu_a, seq)

    prm = [p[None, :] for p in (w0[l], a0[l], k_k[l], k_a[l], r_k[l].reshape(-1), ln_w[l], ln_b[l])]
    yr = _rwkv(rw, prm, _padr(w_up[l], 128).astype(BF16), _padr(a_up[l], 128).astype(BF16),
               _padr(g_up[l], 256).astype(BF16), batch, seq)

    half = ROPE_DIM // 2
    inv_freq = 1.0 / (ROPE_THETA ** (jnp.arange(half, dtype=F32) * (2.0 / ROPE_DIM)))
    freq = jnp.tile(inv_freq, LANE // half)[None, :]
    wq = w_uq[l].reshape(Q_LORA, MLA_HEADS, NOPE_DIM + ROPE_DIM)
    wq_p = jnp.pad(wq, ((0, 0), (0, 0), (0, HEAD_PAD - NOPE_DIM - ROPE_DIM)))
    wq_rot = jnp.concatenate([jnp.zeros_like(wq[..., :NOPE_DIM]), _rot_half(wq[..., NOPE_DIM:])], axis=-1)
    wq_rot_p = jnp.pad(wq_rot, ((0, 0), (0, 0), (0, HEAD_PAD - NOPE_DIM - ROPE_DIM)))
    wkv = w_ukv[l].reshape(KV_LORA, MLA_HEADS, NOPE_DIM + V_DIM)
    wk_p = jnp.pad(wkv[..., :NOPE_DIM], ((0, 0), (0, 0), (0, HEAD_PAD - NOPE_DIM)))
    wvt = jnp.transpose(wkv[..., NOPE_DIM:], (1, 2, 0))
    tk = min(512, seq)
    q5, k5, vt5 = _mla_prep(
        ml, positions.reshape(t_tok, 1), freq, q_norm[l][None, :], kv_norm[l][None, :],
        wq_p.reshape(Q_LORA, -1).astype(BF16), wq_rot_p.reshape(Q_LORA, -1).astype(BF16),
        wk_p.reshape(KV_LORA, -1).astype(BF16), wvt.astype(BF16), batch, seq, tk)
    o_att = _attn(q5, k5, vt5, batch, seq, tk)

    out = _tail(x2, yr, o_att, gt, w_o_rwkv[l].astype(BF16), w_o_mla[l].astype(BF16),
                w_out[l].astype(BF16), norm_ffn[l][None, :], w_ff_up[l].astype(BF16),
                w_ff_down[l].astype(BF16), norm_final[None, :], seq)
    return out.reshape(batch, seq, D_MODEL)
```

```python
import functools
import math

import jax
import jax.numpy as jnp
from jax import lax
from jax.experimental import pallas as pl
from jax.experimental.pallas import tpu as pltpu

F32 = jnp.float32
BF16 = jnp.bfloat16

D_MODEL = 1024
NORM_EPS = 1e-6
RWKV_HEAD = 64
RWKV_HEADS = 8
RWKV_DIM = RWKV_HEADS * RWKV_HEAD
DECAY_LORA = 64
ICLR_LORA = 64
GATE_LORA = 160
GN_EPS = 64e-5
MLA_HEADS = 8
Q_LORA = 256
KV_LORA = 128
NOPE_DIM = 64
ROPE_DIM = 32
V_DIM = 64
ROPE_THETA = 10000.0
D_FF = 4 * D_MODEL

LANE = 128
CHUNK = 64
PAIR = 2 * RWKV_HEAD
HEAD_PAD = 128
V_EXT = V_DIM + 16

RW_COLS = 3 * RWKV_DIM + 128 + 128 + 256
ML_COLS = Q_LORA + KV_LORA + 128 + 128
GT_COLS = 2 * D_MODEL
VMEM_LIMIT = 56 * 1024 * 1024


def _cparams(sem):
    return pltpu.CompilerParams(dimension_semantics=sem, vmem_limit_bytes=VMEM_LIMIT)


def _sigmoid(x):
    return 1.0 / (1.0 + jnp.exp(-x))


def _rms(x, g):
    ms = jnp.mean(x * x, axis=-1, keepdims=True)
    return x * lax.rsqrt(ms + NORM_EPS) * g


def _dot(a, b):
    return jnp.dot(a, b, preferred_element_type=F32)


def _dot_nt(a, b):
    return lax.dot_general(a, b, (((1,), (1,)), ((), ())), preferred_element_type=F32)


def _dot_tn(a, b):
    return lax.dot_general(a, b, (((0,), (0,)), ((), ())), preferred_element_type=F32)


def _inproj_kernel(x_ref, g_ref, w_ref, mu_ref, rw_ref, ml_ref, gt_ref, carry_ref,
                   *, tiles_per_seq, tm):
    i = pl.program_id(0)
    u = _rms(x_ref[...], g_ref[...]).astype(BF16)

    @pl.when(i % tiles_per_seq == 0)
    def _():
        carry_ref[...] = jnp.zeros_like(carry_ref)

    cw = 512
    row0 = lax.broadcasted_iota(jnp.int32, (tm, cw), 0) == 0
    for c in range(RW_COLS // cw):
        cs = slice(c * cw, (c + 1) * cw)
        z = _dot(u, w_ref[:, cs])
        prev = pltpu.roll(z, 1, 0)
        prev = jnp.where(row0, carry_ref[7:8, cs], prev)
        carry_ref[:, cs] = z[tm - 8:tm, :]
        rw_ref[:, cs] = z + (prev - z) * mu_ref[:, cs]
    ml_ref[...] = _dot(u, w_ref[:, RW_COLS:RW_COLS + ML_COLS])
    for c in range(GT_COLS // cw):
        gt_ref[:, c * cw:(c + 1) * cw] = _dot(
            u, w_ref[:, RW_COLS + ML_COLS + c * cw:RW_COLS + ML_COLS + (c + 1) * cw])


def _inproj(x2, g, w_a, mu_a, seq):
    t_tok = x2.shape[0]
    tm = min(512, seq)
    ncol = RW_COLS + ML_COLS + GT_COLS
    kern = functools.partial(_inproj_kernel, tiles_per_seq=seq // tm, tm=tm)
    return pl.pallas_call(
        kern,
        grid=(t_tok // tm,),
        in_specs=[
            pl.BlockSpec((tm, D_MODEL), lambda i: (i, 0)),
            pl.BlockSpec((1, D_MODEL), lambda i: (0, 0)),
            pl.BlockSpec((D_MODEL, ncol), lambda i: (0, 0)),
            pl.BlockSpec((1, RW_COLS), lambda i: (0, 0)),
        ],
        out_specs=[
            pl.BlockSpec((tm, RW_COLS), lambda i: (i, 0)),
            pl.BlockSpec((tm, ML_COLS), lambda i: (i, 0)),
            pl.BlockSpec((tm, GT_COLS), lambda i: (i, 0)),
        ],
        out_shape=[
            jax.ShapeDtypeStruct((t_tok, RW_COLS), F32),
            jax.ShapeDtypeStruct((t_tok, ML_COLS), F32),
            jax.ShapeDtypeStruct((t_tok, GT_COLS), F32),
        ],
        scratch_shapes=[pltpu.VMEM((8, RW_COLS), F32)],
        compiler_params=_cparams(("arbitrary",)),
        name="inproj",
    )(x2, g, w_a, mu_a)


def _split3(x):
    hi = x.astype(BF16)
    r1 = x - hi.astype(F32)
    mid = r1.astype(BF16)
    lo = (r1 - mid.astype(F32)).astype(BF16)
    return hi, mid, lo


def _rwkv_masks():
    n = 2 * CHUNK
    row = lax.broadcasted_iota(jnp.int32, (n, n), 0)
    col = lax.broadcasted_iota(jnp.int32, (n, n), 1)
    same = (row // CHUNK) == (col // CHUNK)
    strict = jnp.where(same & (row > col), 1.0, 0.0).astype(F32)
    incl = jnp.where(same & (row >= col), 1.0, 0.0).astype(F32)
    m16 = jnp.where((row // 16) == (col // 16), 1.0, 0.0).astype(F32)
    m32 = jnp.where(((row // 32) == (col // 32)) & ((row // 16) > (col // 16)), 1.0, 0.0).astype(F32)
    m64 = jnp.where((row // 32) > (col // 32), 1.0, 0.0).astype(F32)
    eye = jnp.where(row == col, 1.0, 0.0).astype(F32)
    headsel = (row // CHUNK) == (col // RWKV_HEAD)
    return strict, incl, m16, m32, m64, eye, headsel


def _rows(x, blk):
    return jnp.concatenate([x[s:s + blk] for s in range(blk, x.shape[0], 2 * blk)], axis=0)


def _merge_rows(x, odd, blk):
    parts = []
    for j, s in enumerate(range(0, x.shape[0], 2 * blk)):
        parts += [x[s:s + blk], odd[j * blk:(j + 1) * blk]]
    return jnp.concatenate(parts, axis=0)


def _rwkv_wave(refs, pairs, nchunk, consts, head_sum):
    (zr_ref, zk_ref, zv_ref, w0_ref, a0_ref, kk_ref, ka_ref, rk_ref, lnw_ref, lnb_ref,
     wup_ref, aup_ref, gup_ref, y_ref, h_ref, tanh_zw, za, sig_zg) = refs
    strict, incl, m16, m32, m64, eye, headsel, tril = consts
    n = 2 * CHUNK
    bf = lambda x: x.astype(BF16)
    items = [(q, slice(c * CHUNK, (c + 1) * CHUNK)) for c in range(nchunk) for q in range(len(pairs))]
    rng = range(len(items))
    c = {}

    def stack(x):
        return jnp.where(headsel, jnp.concatenate([x, x], axis=0), 0.0)

    def pre():
        c["keep"] = []
        streams = []
        for p in pairs:
            ls = slice(p * PAIR, (p + 1) * PAIR)
            zr, zk, zv = zr_ref[:, ls], zk_ref[:, ls], zv_ref[:, ls]
            w_pre = w0_ref[:, ls] + _dot(tanh_zw, wup_ref[:, ls])
            lw = -math.exp(-0.5) * _sigmoid(w_pre)
            iclr = _sigmoid(a0_ref[:, ls] + _dot(za, aup_ref[:, ls]))
            gate = _dot(sig_zg, gup_ref[:, ls])
            kk = zk * kk_ref[:, ls]
            kk = kk / jnp.maximum(jnp.sqrt(head_sum(kk * kk)), 1e-12)
            k = zk * (1.0 + (iclr - 1.0) * ka_ref[:, ls])
            streams.append((zr, k, zv, -kk, kk * iclr, lw))
            c["keep"].append((zr, k, zv, gate))
        for j, name in enumerate(("r", "k", "v", "a", "b", "lw")):
            c[name] = [streams[q][j][s] for q, s in items]

    def cumsum():
        c["cs"] = []
        for i in rng:
            hi, mid, lo = _split3(c["lw"][i])
            c["cs"].append(_dot(tril, hi) + _dot(tril, mid) + _dot(tril, lo))

    def scale():
        cs, lw = c["cs"], c["lw"]
        g_in = [jnp.exp(cs[i]) for i in rng]
        g_ex = [jnp.exp(cs[i] - lw[i]) for i in rng]
        g_inv = [jnp.exp(-cs[i]) for i in rng]
        c["g_last"] = [jnp.exp(cs[i][CHUNK - 1:CHUNK, :]) for i in rng]
        c["rt"] = [stack(c["r"][i] * g_in[i]) for i in rng]
        c["at"] = [bf(stack(c["a"][i] * g_ex[i])) for i in rng]
        bt = [stack(c["b"][i] * g_inv[i]) for i in rng]
        kt = [stack(c["k"][i] * g_inv[i]) for i in rng]
        c["bh"] = [bf(bt[i] * c["g_last"][i]) for i in rng]
        c["kh"] = [bf(kt[i] * c["g_last"][i]) for i in rng]
        c["v2"] = [bf(stack(c["v"][i])) for i in rng]
        c["bk"] = [bf(jnp.concatenate([bt[i], kt[i]], axis=0)) for i in rng]

    def gram():
        ga = [_dot_nt(c["at"][i], c["bk"][i]) for i in rng]
        gr = [_dot_nt(bf(c["rt"][i]), c["bk"][i]) for i in rng]
        c["a_ab"] = [g[:, :n] * strict for g in ga]
        c["a_kr"] = [bf(jnp.concatenate([ga[i][:, n:] * strict, gr[i][:, n:] * incl], axis=0)) for i in rng]
        c["a_rb"] = [bf(g[:, :n] * incl) for g in gr]

    def inv0():
        c["d"] = [bf(a * m16) for a in c["a_ab"]]
        c["t"] = [eye + a * m16 for a in c["a_ab"]]
        c["p"] = [_dot(d, d) for d in c["d"]]

    def inv_double():
        xs = [_dot(bf(jnp.concatenate([t, p], axis=0)), bf(p)) for t, p in zip(c["t"], c["p"])]
        c["t"] = [t + x[:n] for t, x in zip(c["t"], xs)]
        c["p"] = [x[n:] for x in xs]

    def inv_last():
        c["t"] = [t + _dot(bf(t), bf(p)) for t, p in zip(c["t"], c["p"])]

    def level_a(msk, blk):
        def f():
            c["tb"] = [bf(t) for t in c["t"]]
            c["x"] = [bf(_dot(_rows(tb, blk), bf(a * msk))) for tb, a in zip(c["tb"], c["a_ab"])]
        return f

    def level_b(blk):
        def f():
            c["t"] = [_merge_rows(t, _rows(t, blk) + _dot(x, tb), blk)
                      for t, x, tb in zip(c["t"], c["x"], c["tb"])]
        return f

    def apply_v():
        c["xv"] = [_dot(c["a_kr"][i], c["v2"][i]) for i in rng]

    def apply_t():
        c["wu"] = [bf(_dot(bf(c["t"][i]), jnp.concatenate([c["at"][i], bf(c["xv"][i][:n])], axis=1)))
                   for i in rng]

    def apply_rb():
        c["qy"] = [_dot(c["a_rb"][i], c["wu"][i]) for i in rng]
        c["mn"] = [_dot_tn(c["bh"][i], c["wu"][i]) for i in rng]

    def assemble():
        kv = [_dot_tn(c["kh"][i], c["v2"][i]) for i in rng]
        qeff = [c["rt"][i] + c["qy"][i][:, :n] for i in rng]
        c["y0"] = [c["qy"][i][:, n:] + c["xv"][i][n:] for i in rng]
        m = [eye * c["g_last"][i] + c["mn"][i][:, :n] for i in rng]
        c["nn"] = [c["mn"][i][:, n:] + kv[i] for i in rng]
        c["mq"] = [bf(jnp.concatenate([m[i], qeff[i]], axis=0)) for i in rng]

    def chain():
        hs = [h_ref[p] for p in pairs]
        ys = [[] for _ in pairs]
        for i in rng:
            q = items[i][0]
            hy = _dot(c["mq"][i], bf(hs[q]))
            hs[q] = hy[:n] + c["nn"][i]
            y2 = hy[n:] + c["y0"][i]
            ys[q].append(y2[:CHUNK, :] + y2[CHUNK:, :])
        for q, p in enumerate(pairs):
            h_ref[p] = hs[q]
        c["ys"] = [jnp.concatenate(y, axis=0) for y in ys]

    def post():
        for q, p in enumerate(pairs):
            ls = slice(p * PAIR, (p + 1) * PAIR)
            zr, k, zv, gate = c["keep"][q]
            y = c["ys"][q]
            mean = head_sum(y) * (1.0 / RWKV_HEAD)
            yc = y - mean
            var = head_sum(yc * yc) * (1.0 / RWKV_HEAD)
            yn = yc * lax.rsqrt(var + GN_EPS) * lnw_ref[:, ls] + lnb_ref[:, ls]
            bonus = head_sum(zr * k * rk_ref[:, ls]) * zv
            y_ref[:, ls] = ((yn + bonus) * gate).astype(y_ref.dtype)

    return [pre, cumsum, scale, gram, inv0, inv_double, inv_double, inv_last,
            level_a(m32, 16), level_b(16), level_a(m64, 32), level_b(32),
            apply_v, apply_t, apply_rb, assemble, chain, post]


def _rwkv_kernel(zr_ref, zk_ref, zv_ref, zw_ref, za_ref, zg_ref,
                 w0_ref, a0_ref, kk_ref, ka_ref, rk_ref, lnw_ref, lnb_ref,
                 wup_ref, aup_ref, gup_ref, y_ref, h_ref, *, tb, npair, lag):
    @pl.when(pl.program_id(2) == 0)
    def _():
        h_ref[...] = jnp.zeros_like(h_ref)

    lane = lax.broadcasted_iota(jnp.int32, (tb, PAIR), 1)
    head0 = lane < RWKV_HEAD

    def head_sum(x):
        s0 = jnp.sum(jnp.where(head0, x, 0.0), axis=-1, keepdims=True)
        s1 = jnp.sum(jnp.where(head0, 0.0, x), axis=-1, keepdims=True)
        return jnp.where(head0, s0, s1)

    trow = lax.broadcasted_iota(jnp.int32, (CHUNK, CHUNK), 0)
    tcol = lax.broadcasted_iota(jnp.int32, (CHUNK, CHUNK), 1)
    tril = jnp.where(trow >= tcol, 1.0, 0.0).astype(BF16)
    consts = _rwkv_masks() + (tril,)
    refs = (zr_ref, zk_ref, zv_ref, w0_ref, a0_ref, kk_ref, ka_ref, rk_ref, lnw_ref, lnb_ref,
            wup_ref, aup_ref, gup_ref, y_ref, h_ref,
            jnp.tanh(zw_ref[...]).astype(BF16), za_ref[...].astype(BF16),
            _sigmoid(zg_ref[...]).astype(BF16))
    for stage in _rwkv_wave(refs, list(range(npair)), tb // CHUNK, consts, head_sum):
        stage()


def _rwkv(rw, prm, wup, aup, gup, batch, seq):
    t_tok = rw.shape[0]
    tb = min(256, seq)
    nt = seq // tb
    npair = 4
    wid = npair * PAIR
    ngrp = RWKV_DIM // wid

    def tok(base):
        return pl.BlockSpec((tb, wid), lambda b, p, t: (b * nt + t, base * ngrp + p))

    def lora(base, width):
        return pl.BlockSpec((tb, width), lambda b, p, t: (b * nt + t, base))

    prm_spec = pl.BlockSpec((1, wid), lambda b, p, t: (0, p))
    in_specs = [
        tok(0), tok(1), tok(2),
        lora(3 * RWKV_DIM // 128, 128), lora(3 * RWKV_DIM // 128 + 1, 128),
        lora((3 * RWKV_DIM + 256) // 256, 256),
    ] + [prm_spec] * 7 + [
        pl.BlockSpec((128, wid), lambda b, p, t: (0, p)),
        pl.BlockSpec((128, wid), lambda b, p, t: (0, p)),
        pl.BlockSpec((256, wid), lambda b, p, t: (0, p)),
    ]
    return pl.pallas_call(
        functools.partial(_rwkv_kernel, tb=tb, npair=npair, lag=0),
        grid=(batch, ngrp, nt),
        in_specs=in_specs,
        out_specs=pl.BlockSpec((tb, wid), lambda b, p, t: (b * nt + t, p)),
        out_shape=jax.ShapeDtypeStruct((t_tok, RWKV_DIM), BF16),
        scratch_shapes=[pltpu.VMEM((npair, PAIR, PAIR), F32)],
        compiler_params=_cparams(("arbitrary", "arbitrary", "arbitrary")),
        name="rwkv7",
    )(rw, rw, rw, rw, rw, rw, *prm, wup, aup, gup)


def _mla_prep_kernel(ml_ref, pos_ref, freq_ref, qn_ref, kvn_ref, wq_ref, wqr_ref, wk_ref, wvt_ref,
                     q_ref, k_ref, vt_ref, *, scale, tk):
    ml = ml_ref[...]
    nblk = LANE // (ROPE_DIM // 2)
    rows = tk // nblk
    lane = lax.broadcasted_iota(jnp.int32, (rows, LANE), 1)
    pos = pos_ref[...].astype(F32)
    pos8 = jnp.zeros((rows, LANE), F32)
    for e in range(nblk):
        pos8 = jnp.where(lane // (ROPE_DIM // 2) == e, pos[e * rows:(e + 1) * rows], pos8)
    ang8 = pos8 * freq_ref[...]
    cos8 = jnp.cos(ang8)
    sin8 = jnp.sin(ang8)
    lo = (lane >= NOPE_DIM) & (lane < NOPE_DIM + ROPE_DIM // 2)
    hi = (lane >= NOPE_DIM + ROPE_DIM // 2) & (lane < NOPE_DIM + ROPE_DIM)

    def expand(t8, fill):
        blocks = []
        for e in range(nblk):
            a = pltpu.roll(t8, (NOPE_DIM - e * (ROPE_DIM // 2)) % LANE, 1)
            b = pltpu.roll(t8, (NOPE_DIM + ROPE_DIM // 2 - e * (ROPE_DIM // 2)) % LANE, 1)
            blocks.append(jnp.where(lo, a, jnp.where(hi, b, fill)))
        return jnp.concatenate(blocks, axis=0)

    cosf = expand(cos8, 1.0)
    sinf = expand(sin8, 0.0)
    cq = _rms(ml[:, :Q_LORA], qn_ref[...]).astype(BF16)
    ckv = _rms(ml[:, Q_LORA:Q_LORA + KV_LORA], kvn_ref[...]).astype(BF16)
    kr = ml[:, Q_LORA + KV_LORA:Q_LORA + KV_LORA + 128]
    krr = ml[:, Q_LORA + KV_LORA + 128:Q_LORA + KV_LORA + 256]
    k_rope = kr * cosf + krr * sinf
    qf = _dot(cq, wq_ref[...])
    qr = _dot(cq, wqr_ref[...])
    kf = _dot(ckv, wk_ref[...])
    for h in range(MLA_HEADS):
        hs = slice(h * HEAD_PAD, (h + 1) * HEAD_PAD)
        q_ref[h] = ((qf[:, hs] * cosf + qr[:, hs] * sinf) * scale).astype(q_ref.dtype)
        k_ref[h] = (kf[:, hs] + k_rope).astype(k_ref.dtype)
        vt_ref[h, :V_DIM] = _dot_nt(wvt_ref[h], ckv).astype(vt_ref.dtype)
        vt_ref[h, V_DIM:] = jnp.ones((V_EXT - V_DIM, tk), vt_ref.dtype)


def _mla_prep(ml, pos, freq, qn, kvn, wq, wqr, wk, wvt, batch, seq, tk):
    nk = seq // tk
    scale = (NOPE_DIM + ROPE_DIM) ** -0.5 * math.log2(math.e)
    full = lambda shape: pl.BlockSpec(shape, lambda b, t: (0,) * len(shape))
    return pl.pallas_call(
        functools.partial(_mla_prep_kernel, scale=scale, tk=tk),
        grid=(batch, nk),
        in_specs=[
            pl.BlockSpec((tk, ML_COLS), lambda b, t: (b * nk + t, 0)),
            pl.BlockSpec((tk, 1), lambda b, t: (b * nk + t, 0)),
            full((1, HEAD_PAD)), full((1, Q_LORA)), full((1, KV_LORA)),
            full((Q_LORA, MLA_HEADS * HEAD_PAD)), full((Q_LORA, MLA_HEADS * HEAD_PAD)),
            full((KV_LORA, MLA_HEADS * HEAD_PAD)), full((MLA_HEADS, V_DIM, KV_LORA)),
        ],
        out_specs=[
            pl.BlockSpec((None, MLA_HEADS, None, tk, HEAD_PAD), lambda b, t: (b, 0, t, 0, 0)),
            pl.BlockSpec((None, MLA_HEADS, None, tk, HEAD_PAD), lambda b, t: (b, 0, t, 0, 0)),
            pl.BlockSpec((None, MLA_HEADS, None, V_EXT, tk), lambda b, t: (b, 0, t, 0, 0)),
        ],
        out_shape=[
            jax.ShapeDtypeStruct((batch, MLA_HEADS, nk, tk, HEAD_PAD), BF16),
            jax.ShapeDtypeStruct((batch, MLA_HEADS, nk, tk, HEAD_PAD), BF16),
            jax.ShapeDtypeStruct((batch, MLA_HEADS, nk, V_EXT, tk), BF16),
        ],
        compiler_params=_cparams(("arbitrary", "arbitrary")),
        name="mla_prep",
    )(ml, pos, freq, qn, kvn, wq, wqr, wk, wvt)


def _attn_kernel(q_ref, k_ref, vt_ref, o_ref, m_ref, acc_ref, s_ref, p_ref, al_ref,
                 *, tk, heads, ks, ws, ahead, dlag, behind):
    i = pl.program_id(2)
    krow = lax.broadcasted_iota(jnp.int32, (ks, ws), 0)
    qcol = lax.broadcasted_iota(jnp.int32, (ks, ws), 1)
    neg = jnp.finfo(F32).min
    nsub, nstrip = tk // ks, tk // ws
    items = [(h, g) for h in range(heads) for g in range(nstrip)]
    rng = range(len(items))
    qs = [q_ref[h, 0, g * ws:(g + 1) * ws, :] for h, g in items]

    full_units = [(sub, i) for sub in range(nsub) for i in rng]
    diag_units = [(sub, i) for sub, i in full_units if not sub * ks > items[i][1] * ws + ws - 1]
    lag = dlag + behind
    tail_units = full_units[len(full_units) - lag:]
    assert diag_units[:ahead] == full_units[:ahead] and len(diag_units) > ahead + lag

    def score(j, unit):
        sub, i = unit
        return _dot_nt(k_ref[items[i][0], j, sub * ks:(sub + 1) * ks, :], qs[i])

    def pv(j, unit, p):
        sub, i = unit
        return _dot(vt_ref[items[i][0], j, :, sub * ks:(sub + 1) * ks], p)

    def step(j, masked):
        units = diag_units if masked else full_units
        nu = len(units)
        keep = 0 if masked else lag
        jprev = jnp.maximum(j - 1, 0)
        s_val, p_val, pv_val, al_val, pv_old = {}, {}, {}, {}, {}
        for t in range(nu + (lag if masked else 0)):
            if t + ahead < nu:
                s_val[t + ahead] = score(j, units[t + ahead])
            if t < nu:
                sub, i = units[t]
                s = s_val.pop(t) if t >= ahead else s_ref[t]
                if masked and sub * ks + ks - 1 > items[i][1] * ws:
                    s = jnp.where(krow + (sub * ks - items[i][1] * ws) <= qcol, s, neg)
                m_old = m_ref[i]
                m_new = jnp.maximum(m_old, jnp.max(s, axis=0, keepdims=True))
                al_val[t] = jnp.exp2(m_old - m_new)
                m_ref[i] = m_new
                p_val[t] = jnp.exp2(s - m_new).astype(BF16)
                if not masked and t + ahead >= nu:
                    s_ref[t + ahead - nu] = score(j + 1, full_units[t + ahead - nu])
            if t < lag:
                pv_old[t] = pv(jprev, tail_units[t], p_ref[t])
            d = t - dlag
            if 0 <= d < nu - keep:
                pv_val[d] = pv(j, units[d], p_val.pop(d))
            x = t - behind
            if 0 <= x < lag:
                i = tail_units[x][1]
                acc_ref[i] = al_ref[x] * acc_ref[i] + pv_old.pop(x)
            w = d - behind
            if 0 <= w < nu - keep:
                i = units[w][1]
                acc_ref[i] = al_val.pop(w) * acc_ref[i] + pv_val.pop(w)
        if not masked:
            for x in range(lag):
                p_ref[x] = p_val[nu - lag + x]
                al_ref[x] = al_val[nu - lag + x]

    m_ref[...] = jnp.full(m_ref.shape, neg, F32)
    acc_ref[...] = jnp.zeros(acc_ref.shape, F32)
    p_ref[...] = jnp.zeros(p_ref.shape, BF16)
    al_ref[...] = jnp.ones(al_ref.shape, F32)
    for t in range(ahead):
        s_ref[t] = score(0, full_units[t])

    def body(j, c):
        step(j, False)
        return c

    lax.fori_loop(0, i, body, 0)
    step(i, True)
    outs = []
    for h in range(heads):
        accs = [acc_ref[h * nstrip + g] for g in range(nstrip)]
        acc = jnp.concatenate([a[:V_DIM] / a[V_DIM:V_DIM + 1] for a in accs], axis=1)
        outs.append(acc.T)
    o_ref[...] = jnp.concatenate(outs, axis=-1).astype(o_ref.dtype)


def _attn(q5, k5, vt5, batch, seq, tk):
    nk = seq // tk
    heads = 8
    ngrp = MLA_HEADS // heads
    ks, ws = min(256, tk), min(256, tk)
    ahead, dlag, behind = 3, 2, 2
    nitem = heads * (tk // ws)
    return pl.pallas_call(
        functools.partial(_attn_kernel, tk=tk, heads=heads, ks=ks, ws=ws, ahead=ahead, dlag=dlag, behind=behind),
        grid=(batch, ngrp, nk),
        in_specs=[
            pl.BlockSpec((None, heads, 1, tk, HEAD_PAD), lambda b, g, i: (b, g, i, 0, 0)),
            pl.BlockSpec((None, heads, nk, tk, HEAD_PAD), lambda b, g, i: (b, g, 0, 0, 0)),
            pl.BlockSpec((None, heads, nk, V_EXT, tk), lambda b, g, i: (b, g, 0, 0, 0)),
        ],
        out_specs=pl.BlockSpec((tk, heads * V_DIM), lambda b, g, i: (b * nk + i, g)),
        out_shape=jax.ShapeDtypeStruct((batch * seq, MLA_HEADS * V_DIM), BF16),
        scratch_shapes=[
            pltpu.VMEM((nitem, 1, ws), F32),
            pltpu.VMEM((nitem, V_EXT, ws), F32),
            pltpu.VMEM((ahead, ks, ws), F32),
            pltpu.VMEM((dlag + behind, ks, ws), BF16),
            pltpu.VMEM((dlag + behind, 1, ws), F32),
        ],
        compiler_params=_cparams(("arbitrary", "arbitrary", "arbitrary")),
        name="mla_attn",
    )(q5, k5, vt5)


def _tail_kernel(x_ref, yr_ref, o_ref, gt_ref, wor_ref, wom_ref, wout_ref, nf_ref, wup_ref, wdn_ref,
                 nfin_ref, out_ref):
    y_a = _dot(yr_ref[...], wor_ref[...])
    y_b = _dot(o_ref[...], wom_ref[...])
    merged = _sigmoid(gt_ref[:, :D_MODEL]) * y_a + _sigmoid(gt_ref[:, D_MODEL:]) * y_b
    h = x_ref[...] + _dot(merged.astype(BF16), wout_ref[...])
    f_in = _rms(h, nf_ref[...]).astype(BF16)
    cw = 1024
    acc = h
    for c in range(D_FF // cw):
        f = _dot(f_in, wup_ref[:, c * cw:(c + 1) * cw])
        f = jnp.square(jnp.maximum(f, 0.0)).astype(BF16)
        acc = acc + _dot(f, wdn_ref[c * cw:(c + 1) * cw, :])
    out_ref[...] = _rms(acc, nfin_ref[...])


def _tail(x2, yr, o, gt, wor, wom, wout, nf, wup, wdn, nfin, seq):
    t_tok = x2.shape[0]
    tm = min(512, seq)
    full = lambda shape: pl.BlockSpec(shape, lambda i: (0,) * len(shape), pipeline_mode=pl.Buffered(1))
    return pl.pallas_call(
        _tail_kernel,
        grid=(t_tok // tm,),
        in_specs=[
            pl.BlockSpec((tm, D_MODEL), lambda i: (i, 0)),
            pl.BlockSpec((tm, RWKV_DIM), lambda i: (i, 0)),
            pl.BlockSpec((tm, MLA_HEADS * V_DIM), lambda i: (i, 0)),
            pl.BlockSpec((tm, GT_COLS), lambda i: (i, 0)),
            full((RWKV_DIM, D_MODEL)), full((MLA_HEADS * V_DIM, D_MODEL)), full((D_MODEL, D_MODEL)),
            full((1, D_MODEL)), full((D_MODEL, D_FF)), full((D_FF, D_MODEL)), full((1, D_MODEL)),
        ],
        out_specs=pl.BlockSpec((tm, D_MODEL), lambda i: (i, 0)),
        out_shape=jax.ShapeDtypeStruct((t_tok, D_MODEL), F32),
        compiler_params=_cparams(("arbitrary",)),
        name="merge_ffn",
    )(x2, yr, o, gt, wor, wom, wout, nf, wup, wdn, nfin)


def _padc(w, n):
    return jnp.pad(w, ((0, 0), (0, n - w.shape[1])))


def _padr(w, n):
    return jnp.pad(w, ((0, n - w.shape[0]), (0, 0)))


def _rot_half(w):
    half = w.shape[-1] // 2
    return jnp.concatenate([-w[..., half:], w[..., :half]], axis=-1)


def kernel(x, positions, norm_mix, w_in, mu_shift, w0, w_up, a0, a_up, g_up, k_k, k_a, r_k, ln_w, ln_b, w_o_rwkv, q_norm, w_uq, kv_norm, w_ukv, w_o_mla, w_out, norm_ffn, w_ff_up, w_ff_down, norm_final):
    batch, seq, _ = x.shape
    t_tok = batch * seq
    x2 = x.reshape(t_tok, D_MODEL)
    l = 0

    wi = w_in[l]
    o = 0
    w_r3 = wi[:, o:o + 3 * RWKV_DIM]; o += 3 * RWKV_DIM
    w_zw = wi[:, o:o + DECAY_LORA]; o += DECAY_LORA
    w_za = wi[:, o:o + ICLR_LORA]; o += ICLR_LORA
    w_zg = wi[:, o:o + GATE_LORA]; o += GATE_LORA
    w_cq = wi[:, o:o + Q_LORA]; o += Q_LORA
    w_ckv = wi[:, o:o + KV_LORA]; o += KV_LORA
    w_kr = wi[:, o:o + ROPE_DIM]; o += ROPE_DIM
    w_gate = wi[:, o:o + 2 * D_MODEL]
    zeros64 = jnp.zeros((D_MODEL, NOPE_DIM), wi.dtype)
    w_kr_p = _padc(jnp.concatenate([zeros64, w_kr], axis=1), 128)
    w_krr_p = _padc(jnp.concatenate([zeros64, _rot_half(w_kr)], axis=1), 128)
    w_a = jnp.concatenate([
        w_r3, _padc(w_zw, 128), _padc(w_za, 128), _padc(w_zg, 256),
        w_cq, w_ckv, w_kr_p, w_krr_p, w_gate], axis=1).astype(BF16)
    mu = mu_shift[l]
    o = 3 * RWKV_DIM
    mu_a = jnp.concatenate([
        mu[:o], jnp.pad(mu[o:o + 64], (0, 64)), jnp.pad(mu[o + 64:o + 128], (0, 64)),
        jnp.pad(mu[o + 128:], (0, 96))])[None, :]

    rw, ml, gt = _inproj(x2, norm_mix[l][None, :], w_a, mu_a, seq)

    prm = [p[None, :] for p in (w0[l], a0[l], k_k[l], k_a[l], r_k[l].reshape(-1), ln_w[l], ln_b[l])]
    yr = _rwkv(rw, prm, _padr(w_up[l], 128).astype(BF16), _padr(a_up[l], 128).astype(BF16),
               _padr(g_up[l], 256).astype(BF16), batch, seq)

    half = ROPE_DIM // 2
    inv_freq = 1.0 / (ROPE_THETA ** (jnp.arange(half, dtype=F32) * (2.0 / ROPE_DIM)))
    freq = jnp.tile(inv_freq, LANE // half)[None, :]
    wq = w_uq[l].reshape(Q_LORA, MLA_HEADS, NOPE_DIM + ROPE_DIM)
    wq_p = jnp.pad(wq, ((0, 0), (0, 0), (0, HEAD_PAD - NOPE_DIM - ROPE_DIM)))
    wq_rot = jnp.concatenate([jnp.zeros_like(wq[..., :NOPE_DIM]), _rot_half(wq[..., NOPE_DIM:])], axis=-1)
    wq_rot_p = jnp.pad(wq_rot, ((0, 0), (0, 0), (0, HEAD_PAD - NOPE_DIM - ROPE_DIM)))
    wkv = w_ukv[l].reshape(KV_LORA, MLA_HEADS, NOPE_DIM + V_DIM)
    wk_p = jnp.pad(wkv[..., :NOPE_DIM], ((0, 0), (0, 0), (0, HEAD_PAD - NOPE_DIM)))
    wvt = jnp.transpose(wkv[..., NOPE_DIM:], (1, 2, 0))
    tk = min(512, seq)
    q5, k5, vt5 = _mla_prep(
        ml, positions.reshape(t_tok, 1), freq, q_norm[l][None, :], kv_norm[l][None, :],
        wq_p.reshape(Q_LORA, -1).astype(BF16), wq_rot_p.reshape(Q_LORA, -1).astype(BF16),
        wk_p.reshape(KV_LORA, -1).astype(BF16), wvt.astype(BF16), batch, seq, tk)
    o_att = _attn(q5, k5, vt5, batch, seq, tk)

    out = _tail(x2, yr, o_att, gt, w_o_rwkv[l].astype(BF16), w_o_mla[l].astype(BF16),
                w_out[l].astype(BF16), norm_ffn[l][None, :], w_ff_up[l].astype(BF16),
                w_ff_down[l].astype(BF16), norm_final[None, :], seq)
    return out.reshape(batch, seq, D_MODEL)
```

```python
import functools
import math

import jax
import jax.numpy as jnp
from jax import lax
from jax.experimental import pallas as pl
from jax.experimental.pallas import tpu as pltpu

F32 = jnp.float32
BF16 = jnp.bfloat16

D_MODEL = 1024
NORM_EPS = 1e-6
RWKV_HEAD = 64
RWKV_HEADS = 8
RWKV_DIM = RWKV_HEADS * RWKV_HEAD
DECAY_LORA = 64
ICLR_LORA = 64
GATE_LORA = 160
GN_EPS = 64e-5
MLA_HEADS = 8
Q_LORA = 256
KV_LORA = 128
NOPE_DIM = 64
ROPE_DIM = 32
V_DIM = 64
ROPE_THETA = 10000.0
D_FF = 4 * D_MODEL

LANE = 128
CHUNK = 64
PAIR = 2 * RWKV_HEAD
HEAD_PAD = 128
V_EXT = V_DIM + 16

RW_COLS = 3 * RWKV_DIM + 128 + 128 + 256
ML_COLS = Q_LORA + KV_LORA + 128 + 128
GT_COLS = 2 * D_MODEL
VMEM_LIMIT = 56 * 1024 * 1024


def _cparams(sem):
    return pltpu.CompilerParams(dimension_semantics=sem, vmem_limit_bytes=VMEM_LIMIT)


def _sigmoid(x):
    return 1.0 / (1.0 + jnp.exp(-x))


def _rms(x, g):
    ms = jnp.mean(x * x, axis=-1, keepdims=True)
    return x * lax.rsqrt(ms + NORM_EPS) * g


def _dot(a, b):
    return jnp.dot(a, b, preferred_element_type=F32)


def _dot_nt(a, b):
    return lax.dot_general(a, b, (((1,), (1,)), ((), ())), preferred_element_type=F32)


def _dot_tn(a, b):
    return lax.dot_general(a, b, (((0,), (0,)), ((), ())), preferred_element_type=F32)


def _inproj_kernel(x_ref, g_ref, w_ref, mu_ref, rw_ref, ml_ref, gt_ref, carry_ref,
                   *, tiles_per_seq, tm):
    i = pl.program_id(0)
    u = _rms(x_ref[...], g_ref[...]).astype(BF16)

    @pl.when(i % tiles_per_seq == 0)
    def _():
        carry_ref[...] = jnp.zeros_like(carry_ref)

    cw = 512
    row0 = lax.broadcasted_iota(jnp.int32, (tm, cw), 0) == 0
    for c in range(RW_COLS // cw):
        cs = slice(c * cw, (c + 1) * cw)
        z = _dot(u, w_ref[:, cs])
        prev = pltpu.roll(z, 1, 0)
        prev = jnp.where(row0, carry_ref[7:8, cs], prev)
        carry_ref[:, cs] = z[tm - 8:tm, :]
        rw_ref[:, cs] = z + (prev - z) * mu_ref[:, cs]
    ml_ref[...] = _dot(u, w_ref[:, RW_COLS:RW_COLS + ML_COLS])
    for c in range(GT_COLS // cw):
        gt_ref[:, c * cw:(c + 1) * cw] = _dot(
            u, w_ref[:, RW_COLS + ML_COLS + c * cw:RW_COLS + ML_COLS + (c + 1) * cw])


def _inproj(x2, g, w_a, mu_a, seq):
    t_tok = x2.shape[0]
    tm = min(512, seq)
    ncol = RW_COLS + ML_COLS + GT_COLS
    kern = functools.partial(_inproj_kernel, tiles_per_seq=seq // tm, tm=tm)
    return pl.pallas_call(
        kern,
        grid=(t_tok // tm,),
        in_specs=[
            pl.BlockSpec((tm, D_MODEL), lambda i: (i, 0)),
            pl.BlockSpec((1, D_MODEL), lambda i: (0, 0)),
            pl.BlockSpec((D_MODEL, ncol), lambda i: (0, 0)),
            pl.BlockSpec((1, RW_COLS), lambda i: (0, 0)),
        ],
        out_specs=[
            pl.BlockSpec((tm, RW_COLS), lambda i: (i, 0)),
            pl.BlockSpec((tm, ML_COLS), lambda i: (i, 0)),
            pl.BlockSpec((tm, GT_COLS), lambda i: (i, 0)),
        ],
        out_shape=[
            jax.ShapeDtypeStruct((t_tok, RW_COLS), F32),
            jax.ShapeDtypeStruct((t_tok, ML_COLS), F32),
            jax.ShapeDtypeStruct((t_tok, GT_COLS), F32),
        ],
        scratch_shapes=[pltpu.VMEM((8, RW_COLS), F32)],
        compiler_params=_cparams(("arbitrary",)),
        name="inproj",
    )(x2, g, w_a, mu_a)


def _split3(x):
    hi = x.astype(BF16)
    r1 = x - hi.astype(F32)
    mid = r1.astype(BF16)
    lo = (r1 - mid.astype(F32)).astype(BF16)
    return hi, mid, lo


def _rwkv_masks():
    n = 2 * CHUNK
    row = lax.broadcasted_iota(jnp.int32, (n, n), 0)
    col = lax.broadcasted_iota(jnp.int32, (n, n), 1)
    same = (row // CHUNK) == (col // CHUNK)
    strict = jnp.where(same & (row > col), 1.0, 0.0).astype(F32)
    incl = jnp.where(same & (row >= col), 1.0, 0.0).astype(F32)
    m16 = jnp.where((row // 16) == (col // 16), 1.0, 0.0).astype(F32)
    m32 = jnp.where(((row // 32) == (col // 32)) & ((row // 16) > (col // 16)), 1.0, 0.0).astype(F32)
    m64 = jnp.where((row // 32) > (col // 32), 1.0, 0.0).astype(F32)
    eye = jnp.where(row == col, 1.0, 0.0).astype(F32)
    headsel = (row // CHUNK) == (col // RWKV_HEAD)
    return strict, incl, m16, m32, m64, eye, headsel


def _rows(x, blk):
    return jnp.concatenate([x[s:s + blk] for s in range(blk, x.shape[0], 2 * blk)], axis=0)


def _merge_rows(x, odd, blk):
    parts = []
    for j, s in enumerate(range(0, x.shape[0], 2 * blk)):
        parts += [x[s:s + blk], odd[j * blk:(j + 1) * blk]]
    return jnp.concatenate(parts, axis=0)


def _rwkv_wave(refs, pairs, nchunk, consts, head_sum):
    (zr_ref, zk_ref, zv_ref, w0_ref, a0_ref, kk_ref, ka_ref, rk_ref, lnw_ref, lnb_ref,
     wup_ref, aup_ref, gup_ref, y_ref, h_ref, tanh_zw, za, sig_zg) = refs
    strict, incl, m16, m32, m64, eye, headsel, tril = consts
    n = 2 * CHUNK
    bf = lambda x: x.astype(BF16)
    items = [(q, slice(c * CHUNK, (c + 1) * CHUNK)) for c in range(nchunk) for q in range(len(pairs))]
    rng = range(len(items))
    c = {}

    def stack(x):
        return jnp.where(headsel, jnp.concatenate([x, x], axis=0), 0.0)

    def pre():
        c["keep"] = []
        streams = []
        for p in pairs:
            ls = slice(p * PAIR, (p + 1) * PAIR)
            zr, zk, zv = zr_ref[:, ls], zk_ref[:, ls], zv_ref[:, ls]
            w_pre = w0_ref[:, ls] + _dot(tanh_zw, wup_ref[:, ls])
            lw = -math.exp(-0.5) * _sigmoid(w_pre)
            iclr = _sigmoid(a0_ref[:, ls] + _dot(za, aup_ref[:, ls]))
            gate = _dot(sig_zg, gup_ref[:, ls])
            kk = zk * kk_ref[:, ls]
            kk = kk / jnp.maximum(jnp.sqrt(head_sum(kk * kk)), 1e-12)
            k = zk * (1.0 + (iclr - 1.0) * ka_ref[:, ls])
            streams.append((zr, k, zv, -kk, kk * iclr, lw))
            c["keep"].append((zr, k, zv, gate))
        for j, name in enumerate(("r", "k", "v", "a", "b", "lw")):
            c[name] = [streams[q][j][s] for q, s in items]

    def cumsum():
        c["cs"] = []
        for i in rng:
            hi, mid, lo = _split3(c["lw"][i])
            c["cs"].append(_dot(tril, hi) + _dot(tril, mid) + _dot(tril, lo))

    def scale():
        cs, lw = c["cs"], c["lw"]
        g_in = [jnp.exp(cs[i]) for i in rng]
        g_ex = [jnp.exp(cs[i] - lw[i]) for i in rng]
        g_inv = [jnp.exp(-cs[i]) for i in rng]
        c["g_last"] = [jnp.exp(cs[i][CHUNK - 1:CHUNK, :]) for i in rng]
        c["rt"] = [stack(c["r"][i] * g_in[i]) for i in rng]
        c["at"] = [bf(stack(c["a"][i] * g_ex[i])) for i in rng]
        bt = [stack(c["b"][i] * g_inv[i]) for i in rng]
        kt = [stack(c["k"][i] * g_inv[i]) for i in rng]
        c["bh"] = [bf(bt[i] * c["g_last"][i]) for i in rng]
        c["kh"] = [bf(kt[i] * c["g_last"][i]) for i in rng]
        c["v2"] = [bf(stack(c["v"][i])) for i in rng]
        c["bk"] = [bf(jnp.concatenate([bt[i], kt[i]], axis=0)) for i in rng]

    def gram():
        ga = [_dot_nt(c["at"][i], c["bk"][i]) for i in rng]
        gr = [_dot_nt(bf(c["rt"][i]), c["bk"][i]) for i in rng]
        c["a_ab"] = [g[:, :n] * strict for g in ga]
        c["a_kr"] = [bf(jnp.concatenate([ga[i][:, n:] * strict, gr[i][:, n:] * incl], axis=0)) for i in rng]
        c["a_rb"] = [bf(g[:, :n] * incl) for g in gr]

    def inv0():
        c["d"] = [bf(a * m16) for a in c["a_ab"]]
        c["t"] = [eye + a * m16 for a in c["a_ab"]]
        c["p"] = [_dot(d, d) for d in c["d"]]

    def inv_double():
        xs = [_dot(bf(jnp.concatenate([t, p], axis=0)), bf(p)) for t, p in zip(c["t"], c["p"])]
        c["t"] = [t + x[:n] for t, x in zip(c["t"], xs)]
        c["p"] = [x[n:] for x in xs]

    def inv_last():
        c["t"] = [t + _dot(bf(t), bf(p)) for t, p in zip(c["t"], c["p"])]

    def level_a(msk, blk):
        def f():
            c["tb"] = [bf(t) for t in c["t"]]
            c["x"] = [bf(_dot(_rows(tb, blk), bf(a * msk))) for tb, a in zip(c["tb"], c["a_ab"])]
        return f

    def level_b(blk):
        def f():
            c["t"] = [_merge_rows(t, _rows(t, blk) + _dot(x, tb), blk)
                      for t, x, tb in zip(c["t"], c["x"], c["tb"])]
        return f

    def apply_v():
        c["xv"] = [_dot(c["a_kr"][i], c["v2"][i]) for i in rng]

    def apply_t():
        c["wu"] = [bf(_dot(bf(c["t"][i]), jnp.concatenate([c["at"][i], bf(c["xv"][i][:n])], axis=1)))
                   for i in rng]

    def apply_rb():
        c["qy"] = [_dot(c["a_rb"][i], c["wu"][i]) for i in rng]
        c["mn"] = [_dot_tn(c["bh"][i], c["wu"][i]) for i in rng]

    def assemble():
        kv = [_dot_tn(c["kh"][i], c["v2"][i]) for i in rng]
        qeff = [c["rt"][i] + c["qy"][i][:, :n] for i in rng]
        c["y0"] = [c["qy"][i][:, n:] + c["xv"][i][n:] for i in rng]
        m = [eye * c["g_last"][i] + c["mn"][i][:, :n] for i in rng]
        c["nn"] = [c["mn"][i][:, n:] + kv[i] for i in rng]
        c["mq"] = [bf(jnp.concatenate([m[i], qeff[i]], axis=0)) for i in rng]

    def chain():
        hs = [h_ref[p] for p in pairs]
        ys = [[] for _ in pairs]
        for i in rng:
            q = items[i][0]
            hy = _dot(c["mq"][i], bf(hs[q]))
            hs[q] = hy[:n] + c["nn"][i]
            y2 = hy[n:] + c["y0"][i]
            ys[q].append(y2[:CHUNK, :] + y2[CHUNK:, :])
        for q, p in enumerate(pairs):
            h_ref[p] = hs[q]
        c["ys"] = [jnp.concatenate(y, axis=0) for y in ys]

    def post():
        for q, p in enumerate(pairs):
            ls = slice(p * PAIR, (p + 1) * PAIR)
            zr, k, zv, gate = c["keep"][q]
            y = c["ys"][q]
            mean = head_sum(y) * (1.0 / RWKV_HEAD)
            yc = y - mean
            var = head_sum(yc * yc) * (1.0 / RWKV_HEAD)
            yn = yc * lax.rsqrt(var + GN_EPS) * lnw_ref[:, ls] + lnb_ref[:, ls]
            bonus = head_sum(zr * k * rk_ref[:, ls]) * zv
            y_ref[:, ls] = ((yn + bonus) * gate).astype(y_ref.dtype)

    return [pre, cumsum, scale, gram, inv0, inv_double, inv_double, inv_last,
            level_a(m32, 16), level_b(16), level_a(m64, 32), level_b(32),
            apply_v, apply_t, apply_rb, assemble, chain, post]


def _rwkv_kernel(zr_ref, zk_ref, zv_ref, zw_ref, za_ref, zg_ref,
                 w0_ref, a0_ref, kk_ref, ka_ref, rk_ref, lnw_ref, lnb_ref,
                 wup_ref, aup_ref, gup_ref, y_ref, h_ref, *, tb, npair, lag):
    @pl.when(pl.program_id(2) == 0)
    def _():
        h_ref[...] = jnp.zeros_like(h_ref)

    lane = lax.broadcasted_iota(jnp.int32, (tb, PAIR), 1)
    head0 = lane < RWKV_HEAD

    def head_sum(x):
        s0 = jnp.sum(jnp.where(head0, x, 0.0), axis=-1, keepdims=True)
        s1 = jnp.sum(jnp.where(head0, 0.0, x), axis=-1, keepdims=True)
        return jnp.where(head0, s0, s1)

    trow = lax.broadcasted_iota(jnp.int32, (CHUNK, CHUNK), 0)
    tcol = lax.broadcasted_iota(jnp.int32, (CHUNK, CHUNK), 1)
    tril = jnp.where(trow >= tcol, 1.0, 0.0).astype(BF16)
    consts = _rwkv_masks() + (tril,)
    refs = (zr_ref, zk_ref, zv_ref, w0_ref, a0_ref, kk_ref, ka_ref, rk_ref, lnw_ref, lnb_ref,
            wup_ref, aup_ref, gup_ref, y_ref, h_ref,
            jnp.tanh(zw_ref[...]).astype(BF16), za_ref[...].astype(BF16),
            _sigmoid(zg_ref[...]).astype(BF16))
    for stage in _rwkv_wave(refs, list(range(npair)), tb // CHUNK, consts, head_sum):
        stage()


def _rwkv(rw, prm, wup, aup, gup, batch, seq):
    t_tok = rw.shape[0]
    tb = min(256, seq)
    nt = seq // tb
    npair = 4
    wid = npair * PAIR
    ngrp = RWKV_DIM // wid

    def tok(base):
        return pl.BlockSpec((tb, wid), lambda b, p, t: (b * nt + t, base * ngrp + p))

    def lora(base, width):
        return pl.BlockSpec((tb, width), lambda b, p, t: (b * nt + t, base))

    prm_spec = pl.BlockSpec((1, wid), lambda b, p, t: (0, p))
    in_specs = [
        tok(0), tok(1), tok(2),
        lora(3 * RWKV_DIM // 128, 128), lora(3 * RWKV_DIM // 128 + 1, 128),
        lora((3 * RWKV_DIM + 256) // 256, 256),
    ] + [prm_spec] * 7 + [
        pl.BlockSpec((128, wid), lambda b, p, t: (0, p)),
        pl.BlockSpec((128, wid), lambda b, p, t: (0, p)),
        pl.BlockSpec((256, wid), lambda b, p, t: (0, p)),
    ]
    return pl.pallas_call(
        functools.partial(_rwkv_kernel, tb=tb, npair=npair, lag=0),
        grid=(batch, ngrp, nt),
        in_specs=in_specs,
        out_specs=pl.BlockSpec((tb, wid), lambda b, p, t: (b * nt + t, p)),
        out_shape=jax.ShapeDtypeStruct((t_tok, RWKV_DIM), BF16),
        scratch_shapes=[pltpu.VMEM((npair, PAIR, PAIR), F32)],
        compiler_params=_cparams(("arbitrary", "arbitrary", "arbitrary")),
        name="rwkv7",
    )(rw, rw, rw, rw, rw, rw, *prm, wup, aup, gup)


def _mla_prep_kernel(ml_ref, pos_ref, posrow_ref, freq_ref, fcol_ref, qn_ref, kvn_ref, wqt_ref, wqrt_ref, wk_ref, wvt_ref,
                     q_ref, k_ref, vt_ref, *, tk):
    ml = ml_ref[...]
    nblk = LANE // (ROPE_DIM // 2)
    rows = tk // nblk
    lane = lax.broadcasted_iota(jnp.int32, (rows, LANE), 1)
    pos = pos_ref[...].astype(F32)
    pos8 = jnp.zeros((rows, LANE), F32)
    for e in range(nblk):
        pos8 = jnp.where(lane // (ROPE_DIM // 2) == e, pos[e * rows:(e + 1) * rows], pos8)
    ang8 = pos8 * freq_ref[...]
    cos8 = jnp.cos(ang8)
    sin8 = jnp.sin(ang8)
    lo = (lane >= NOPE_DIM) & (lane < NOPE_DIM + ROPE_DIM // 2)
    hi = (lane >= NOPE_DIM + ROPE_DIM // 2) & (lane < NOPE_DIM + ROPE_DIM)

    def expand(t8, fill):
        blocks = []
        for e in range(nblk):
            a = pltpu.roll(t8, (NOPE_DIM - e * (ROPE_DIM // 2)) % LANE, 1)
            b = pltpu.roll(t8, (NOPE_DIM + ROPE_DIM // 2 - e * (ROPE_DIM // 2)) % LANE, 1)
            blocks.append(jnp.where(lo, a, jnp.where(hi, b, fill)))
        return jnp.concatenate(blocks, axis=0)

    cosf = expand(cos8, 1.0)
    sinf = expand(sin8, 0.0)
    cq = _rms(ml[:, :Q_LORA], qn_ref[...]).astype(BF16)
    ckv = _rms(ml[:, Q_LORA:Q_LORA + KV_LORA], kvn_ref[...]).astype(BF16)
    kr = ml[:, Q_LORA + KV_LORA:Q_LORA + KV_LORA + 128]
    krr = ml[:, Q_LORA + KV_LORA + 128:Q_LORA + KV_LORA + 256]
    k_rope = kr * cosf + krr * sinf
    ang_t = fcol_ref[...] * posrow_ref[...].astype(F32)
    cos_t = jnp.cos(ang_t)
    sin_t = jnp.sin(ang_t)
    rope = slice(NOPE_DIM, NOPE_DIM + ROPE_DIM)
    kf = _dot(ckv, wk_ref[...])
    qf_all = _dot_nt(wqt_ref[...], cq)
    qr_all = _dot_nt(wqrt_ref[...], cq)
    for h in range(MLA_HEADS):
        hs = slice(h * HEAD_PAD, (h + 1) * HEAD_PAD)
        qf_t = qf_all[hs]
        qr_t = qr_all[h * ROPE_DIM:(h + 1) * ROPE_DIM]
        q_t = jnp.concatenate([qf_t[:NOPE_DIM], qf_t[rope] * cos_t + qr_t * sin_t,
                               qf_t[NOPE_DIM + ROPE_DIM:]], axis=0)
        q_ref[h] = q_t.astype(q_ref.dtype)
        k_ref[h] = (kf[:, hs] + k_rope).astype(k_ref.dtype)
        vt_ref[h, :V_DIM] = _dot_nt(wvt_ref[h], ckv).astype(vt_ref.dtype)
        vt_ref[h, V_DIM:] = jnp.ones((V_EXT - V_DIM, tk), vt_ref.dtype)


def _mla_prep(ml, pos, posrow, freq, fcol, qn, kvn, wqt, wqrt, wk, wvt, batch, seq, tk):
    nk = seq // tk
    full = lambda shape: pl.BlockSpec(shape, lambda b, t: (0,) * len(shape))
    return pl.pallas_call(
        functools.partial(_mla_prep_kernel, tk=tk),
        grid=(batch, nk),
        in_specs=[
            pl.BlockSpec((tk, ML_COLS), lambda b, t: (b * nk + t, 0)),
            pl.BlockSpec((tk, 1), lambda b, t: (b * nk + t, 0)),
            pl.BlockSpec((None, 1, tk), lambda b, t: (b * nk + t, 0, 0)),
            full((1, HEAD_PAD)), full((ROPE_DIM, 1)), full((1, Q_LORA)), full((1, KV_LORA)),
            full((MLA_HEADS * HEAD_PAD, Q_LORA)), full((MLA_HEADS * ROPE_DIM, Q_LORA)),
            full((KV_LORA, MLA_HEADS * HEAD_PAD)), full((MLA_HEADS, V_DIM, KV_LORA)),
        ],
        out_specs=[
            pl.BlockSpec((None, MLA_HEADS, None, HEAD_PAD, tk), lambda b, t: (b, 0, t, 0, 0)),
            pl.BlockSpec((None, MLA_HEADS, None, tk, HEAD_PAD), lambda b, t: (b, 0, t, 0, 0)),
            pl.BlockSpec((None, MLA_HEADS, None, V_EXT, tk), lambda b, t: (b, 0, t, 0, 0)),
        ],
        out_shape=[
            jax.ShapeDtypeStruct((batch, MLA_HEADS, nk, HEAD_PAD, tk), BF16),
            jax.ShapeDtypeStruct((batch, MLA_HEADS, nk, tk, HEAD_PAD), BF16),
            jax.ShapeDtypeStruct((batch, MLA_HEADS, nk, V_EXT, tk), BF16),
        ],
        compiler_params=_cparams(("arbitrary", "arbitrary")),
        name="mla_prep",
    )(ml, pos, posrow, freq, fcol, qn, kvn, wqt, wqrt, wk, wvt)


def _attn_kernel(q_ref, k_ref, vt_ref, o_ref, m_ref, acc_ref, s_ref, p_ref, al_ref,
                 *, tk, heads, ks, ws, ahead, dlag, behind):
    i = pl.program_id(2)
    krow = lax.broadcasted_iota(jnp.int32, (ks, ws), 0)
    qcol = lax.broadcasted_iota(jnp.int32, (ks, ws), 1)
    neg = jnp.finfo(F32).min
    nsub, nstrip = tk // ks, tk // ws
    items = [(h, g) for h in range(heads) for g in range(nstrip)]
    rng = range(len(items))
    qs = [q_ref[h, 0, :, g * ws:(g + 1) * ws] for h, g in items]

    full_units = [(sub, i) for sub in range(nsub) for i in rng]
    diag_units = [(sub, i) for sub, i in full_units if not sub * ks > items[i][1] * ws + ws - 1]
    lag = dlag + behind
    tail_units = full_units[len(full_units) - lag:]
    assert diag_units[:ahead] == full_units[:ahead] and len(diag_units) > ahead + lag

    def score(j, unit):
        sub, i = unit
        return _dot(k_ref[items[i][0], j, sub * ks:(sub + 1) * ks, :], qs[i])

    def pv(j, unit, p):
        sub, i = unit
        return _dot(vt_ref[items[i][0], j, :, sub * ks:(sub + 1) * ks], p)

    def step(j, masked):
        units = diag_units if masked else full_units
        nu = len(units)
        keep = 0 if masked else lag
        jprev = jnp.maximum(j - 1, 0)
        s_val, p_val, pv_val, al_val, pv_old = {}, {}, {}, {}, {}
        for t in range(nu + (lag if masked else 0)):
            if t + ahead < nu:
                s_val[t + ahead] = score(j, units[t + ahead])
            if t < nu:
                sub, i = units[t]
                s = s_val.pop(t) if t >= ahead else s_ref[t]
                if masked and sub * ks + ks - 1 > items[i][1] * ws:
                    s = jnp.where(krow + (sub * ks - items[i][1] * ws) <= qcol, s, neg)
                m_old = m_ref[i]
                m_new = jnp.maximum(m_old, jnp.max(s, axis=0, keepdims=True))
                al_val[t] = jnp.exp2(m_old - m_new)
                m_ref[i] = m_new
                p_val[t] = jnp.exp2(s - m_new).astype(BF16)
                if not masked and t + ahead >= nu:
                    s_ref[t + ahead - nu] = score(j + 1, full_units[t + ahead - nu])
            if t < lag:
                pv_old[t] = pv(jprev, tail_units[t], p_ref[t])
            d = t - dlag
            if 0 <= d < nu - keep:
                pv_val[d] = pv(j, units[d], p_val.pop(d))
            x = t - behind
            if 0 <= x < lag:
                i = tail_units[x][1]
                acc_ref[i] = al_ref[x] * acc_ref[i] + pv_old.pop(x)
            w = d - behind
            if 0 <= w < nu - keep:
                i = units[w][1]
                acc_ref[i] = al_val.pop(w) * acc_ref[i] + pv_val.pop(w)
        if not masked:
            for x in range(lag):
                p_ref[x] = p_val[nu - lag + x]
                al_ref[x] = al_val[nu - lag + x]

    m_ref[...] = jnp.full(m_ref.shape, neg, F32)
    acc_ref[...] = jnp.zeros(acc_ref.shape, F32)
    p_ref[...] = jnp.zeros(p_ref.shape, BF16)
    al_ref[...] = jnp.ones(al_ref.shape, F32)
    for t in range(ahead):
        s_ref[t] = score(0, full_units[t])

    def body(j, c):
        step(j, False)
        return c

    lax.fori_loop(0, i, body, 0)
    step(i, True)
    outs = []
    for h in range(heads):
        accs = [acc_ref[h * nstrip + g] for g in range(nstrip)]
        acc = jnp.concatenate([a[:V_DIM] / a[V_DIM:V_DIM + 1] for a in accs], axis=1)
        outs.append(acc.T)
    o_ref[...] = jnp.concatenate(outs, axis=-1).astype(o_ref.dtype)


def _attn(q5, k5, vt5, batch, seq, tk):
    nk = seq // tk
    heads = 8
    ngrp = MLA_HEADS // heads
    ks, ws = min(256, tk), min(256, tk)
    ahead, dlag, behind = 3, 2, 2
    nitem = heads * (tk // ws)
    return pl.pallas_call(
        functools.partial(_attn_kernel, tk=tk, heads=heads, ks=ks, ws=ws, ahead=ahead, dlag=dlag, behind=behind),
        grid=(batch, ngrp, nk),
        in_specs=[
            pl.BlockSpec((None, heads, 1, HEAD_PAD, tk), lambda b, g, i: (b, g, i, 0, 0)),
            pl.BlockSpec((None, heads, nk, tk, HEAD_PAD), lambda b, g, i: (b, g, 0, 0, 0)),
            pl.BlockSpec((None, heads, nk, V_EXT, tk), lambda b, g, i: (b, g, 0, 0, 0)),
        ],
        out_specs=pl.BlockSpec((tk, heads * V_DIM), lambda b, g, i: (b * nk + i, g)),
        out_shape=jax.ShapeDtypeStruct((batch * seq, MLA_HEADS * V_DIM), BF16),
        scratch_shapes=[
            pltpu.VMEM((nitem, 1, ws), F32),
            pltpu.VMEM((nitem, V_EXT, ws), F32),
            pltpu.VMEM((ahead, ks, ws), F32),
            pltpu.VMEM((dlag + behind, ks, ws), BF16),
            pltpu.VMEM((dlag + behind, 1, ws), F32),
        ],
        compiler_params=_cparams(("arbitrary", "arbitrary", "arbitrary")),
        name="mla_attn",
    )(q5, k5, vt5)


def _tail_kernel(x_ref, yr_ref, o_ref, gt_ref, wor_ref, wom_ref, wout_ref, nf_ref, wup_ref, wdn_ref,
                 nfin_ref, out_ref):
    y_a = _dot(yr_ref[...], wor_ref[...])
    y_b = _dot(o_ref[...], wom_ref[...])
    merged = _sigmoid(gt_ref[:, :D_MODEL]) * y_a + _sigmoid(gt_ref[:, D_MODEL:]) * y_b
    h = x_ref[...] + _dot(merged.astype(BF16), wout_ref[...])
    f_in = _rms(h, nf_ref[...]).astype(BF16)
    cw = 1024
    acc = h
    for c in range(D_FF // cw):
        f = _dot(f_in, wup_ref[:, c * cw:(c + 1) * cw])
        f = jnp.square(jnp.maximum(f, 0.0)).astype(BF16)
        acc = acc + _dot(f, wdn_ref[c * cw:(c + 1) * cw, :])
    out_ref[...] = _rms(acc, nfin_ref[...])


def _tail(x2, yr, o, gt, wor, wom, wout, nf, wup, wdn, nfin, seq):
    t_tok = x2.shape[0]
    tm = min(512, seq)
    full = lambda shape: pl.BlockSpec(shape, lambda i: (0,) * len(shape), pipeline_mode=pl.Buffered(1))
    return pl.pallas_call(
        _tail_kernel,
        grid=(t_tok // tm,),
        in_specs=[
            pl.BlockSpec((tm, D_MODEL), lambda i: (i, 0)),
            pl.BlockSpec((tm, RWKV_DIM), lambda i: (i, 0)),
            pl.BlockSpec((tm, MLA_HEADS * V_DIM), lambda i: (i, 0)),
            pl.BlockSpec((tm, GT_COLS), lambda i: (i, 0)),
            full((RWKV_DIM, D_MODEL)), full((MLA_HEADS * V_DIM, D_MODEL)), full((D_MODEL, D_MODEL)),
            full((1, D_MODEL)), full((D_MODEL, D_FF)), full((D_FF, D_MODEL)), full((1, D_MODEL)),
        ],
        out_specs=pl.BlockSpec((tm, D_MODEL), lambda i: (i, 0)),
        out_shape=jax.ShapeDtypeStruct((t_tok, D_MODEL), F32),
        compiler_params=_cparams(("arbitrary",)),
        name="merge_ffn",
    )(x2, yr, o, gt, wor, wom, wout, nf, wup, wdn, nfin)


def _padc(w, n):
    return jnp.pad(w, ((0, 0), (0, n - w.shape[1])))


def _padr(w, n):
    return jnp.pad(w, ((0, n - w.shape[0]), (0, 0)))


def _rot_half(w):
    half = w.shape[-1] // 2
    return jnp.concatenate([-w[..., half:], w[..., :half]], axis=-1)


def kernel(x, positions, norm_mix, w_in, mu_shift, w0, w_up, a0, a_up, g_up, k_k, k_a, r_k, ln_w, ln_b, w_o_rwkv, q_norm, w_uq, kv_norm, w_ukv, w_o_mla, w_out, norm_ffn, w_ff_up, w_ff_down, norm_final):
    batch, seq, _ = x.shape
    t_tok = batch * seq
    x2 = x.reshape(t_tok, D_MODEL)
    l = 0

    wi = w_in[l]
    o = 0
    w_r3 = wi[:, o:o + 3 * RWKV_DIM]; o += 3 * RWKV_DIM
    w_zw = wi[:, o:o + DECAY_LORA]; o += DECAY_LORA
    w_za = wi[:, o:o + ICLR_LORA]; o += ICLR_LORA
    w_zg = wi[:, o:o + GATE_LORA]; o += GATE_LORA
    w_cq = wi[:, o:o + Q_LORA]; o += Q_LORA
    w_ckv = wi[:, o:o + KV_LORA]; o += KV_LORA
    w_kr = wi[:, o:o + ROPE_DIM]; o += ROPE_DIM
    w_gate = wi[:, o:o + 2 * D_MODEL]
    zeros64 = jnp.zeros((D_MODEL, NOPE_DIM), wi.dtype)
    w_kr_p = _padc(jnp.concatenate([zeros64, w_kr], axis=1), 128)
    w_krr_p = _padc(jnp.concatenate([zeros64, _rot_half(w_kr)], axis=1), 128)
    w_a = jnp.concatenate([
        w_r3, _padc(w_zw, 128), _padc(w_za, 128), _padc(w_zg, 256),
        w_cq, w_ckv, w_kr_p, w_krr_p, w_gate], axis=1).astype(BF16)
    mu = mu_shift[l]
    o = 3 * RWKV_DIM
    mu_a = jnp.concatenate([
        mu[:o], jnp.pad(mu[o:o + 64], (0, 64)), jnp.pad(mu[o + 64:o + 128], (0, 64)),
        jnp.pad(mu[o + 128:], (0, 96))])[None, :]

    rw, ml, gt = _inproj(x2, norm_mix[l][None, :], w_a, mu_a, seq)

    prm = [p[None, :] for p in (w0[l], a0[l], k_k[l], k_a[l], r_k[l].reshape(-1), ln_w[l], ln_b[l])]
    yr = _rwkv(rw, prm, _padr(w_up[l], 128).astype(BF16), _padr(a_up[l], 128).astype(BF16),
               _padr(g_up[l], 256).astype(BF16), batch, seq)

    half = ROPE_DIM // 2
    inv_freq = 1.0 / (ROPE_THETA ** (jnp.arange(half, dtype=F32) * (2.0 / ROPE_DIM)))
    freq = jnp.tile(inv_freq, LANE // half)[None, :]
    scale = (NOPE_DIM + ROPE_DIM) ** -0.5 * math.log2(math.e)
    wq = w_uq[l].reshape(Q_LORA, MLA_HEADS, NOPE_DIM + ROPE_DIM) * scale
    wq_p = jnp.pad(wq, ((0, 0), (0, 0), (0, HEAD_PAD - NOPE_DIM - ROPE_DIM)))
    wqt = jnp.transpose(wq_p, (1, 2, 0)).reshape(MLA_HEADS * HEAD_PAD, Q_LORA)
    wqrt = jnp.transpose(_rot_half(wq[..., NOPE_DIM:]), (1, 2, 0)).reshape(MLA_HEADS * ROPE_DIM, Q_LORA)
    fcol = jnp.concatenate([inv_freq, inv_freq])[:, None]
    wkv = w_ukv[l].reshape(KV_LORA, MLA_HEADS, NOPE_DIM + V_DIM)
    wk_p = jnp.pad(wkv[..., :NOPE_DIM], ((0, 0), (0, 0), (0, HEAD_PAD - NOPE_DIM)))
    wvt = jnp.transpose(wkv[..., NOPE_DIM:], (1, 2, 0))
    tk = min(512, seq)
    q5, k5, vt5 = _mla_prep(
        ml, positions.reshape(t_tok, 1), positions.reshape(t_tok // tk, 1, tk), freq, fcol,
        q_norm[l][None, :], kv_norm[l][None, :], wqt.astype(BF16), wqrt.astype(BF16),
        wk_p.reshape(KV_LORA, -1).astype(BF16), wvt.astype(BF16), batch, seq, tk)
    o_att = _attn(q5, k5, vt5, batch, seq, tk)

    out = _tail(x2, yr, o_att, gt, w_o_rwkv[l].astype(BF16), w_o_mla[l].astype(BF16),
                w_out[l].astype(BF16), norm_ffn[l][None, :], w_ff_up[l].astype(BF16),
                w_ff_down[l].astype(BF16), norm_final[None, :], seq)
    return out.reshape(batch, seq, D_MODEL)
```

```python
import functools
import math

import jax
import jax.numpy as jnp
from jax import lax
from jax.experimental import pallas as pl
from jax.experimental.pallas import tpu as pltpu

F32 = jnp.float32
BF16 = jnp.bfloat16

D_MODEL = 1024
NORM_EPS = 1e-6
RWKV_HEAD = 64
RWKV_HEADS = 8
RWKV_DIM = RWKV_HEADS * RWKV_HEAD
DECAY_LORA = 64
ICLR_LORA = 64
GATE_LORA = 160
GN_EPS = 64e-5
MLA_HEADS = 8
Q_LORA = 256
KV_LORA = 128
NOPE_DIM = 64
ROPE_DIM = 32
V_DIM = 64
ROPE_THETA = 10000.0
D_FF = 4 * D_MODEL

LANE = 128
CHUNK = 64
PAIR = 2 * RWKV_HEAD
HEAD_PAD = 128
V_EXT = V_DIM + 16

ZG_PAD = 256
RW_COLS = 3 * RWKV_DIM + ZG_PAD + 128
ML_COLS = Q_LORA + KV_LORA + 128
GT_COLS = 2 * D_MODEL

VMEM_BYTES_V7X = 64 * 1024 * 1024
VMEM_LIMIT = VMEM_BYTES_V7X * 7 // 8
TOKEN_TILE = 512
RWKV_TILE = 4 * CHUNK
RWKV_PAIRS = RWKV_DIM // PAIR


def _cparams(sem):
    return pltpu.CompilerParams(dimension_semantics=sem, vmem_limit_bytes=VMEM_LIMIT)


def _sigmoid(x):
    return 1.0 / (1.0 + jnp.exp(-x))


def _rms(x, g):
    ms = jnp.mean(x * x, axis=-1, keepdims=True)
    return x * lax.rsqrt(ms + NORM_EPS) * g


def _dot(a, b):
    return jnp.dot(a, b, preferred_element_type=F32)


def _dot_nt(a, b):
    return lax.dot_general(a, b, (((1,), (1,)), ((), ())), preferred_element_type=F32)


def _dot_tn(a, b):
    return lax.dot_general(a, b, (((0,), (0,)), ((), ())), preferred_element_type=F32)


def _inproj_kernel(x_ref, g_ref, w_ref, mu_ref, rw_ref, ml_ref, gt_ref, carry_ref,
                   *, tiles_per_seq, tm):
    i = pl.program_id(0)
    u = _rms(x_ref[...], g_ref[...]).astype(BF16)

    @pl.when(i % tiles_per_seq == 0)
    def _():
        carry_ref[...] = jnp.zeros_like(carry_ref)

    cw = 512
    for c0 in range(0, RW_COLS, cw):
        cs = slice(c0, min(c0 + cw, RW_COLS))
        z = _dot(u, w_ref[:, cs])
        prev = pltpu.roll(z, 1, 0)
        row0 = lax.broadcasted_iota(jnp.int32, z.shape, 0) == 0
        prev = jnp.where(row0, carry_ref[7:8, cs], prev)
        carry_ref[:, cs] = z[tm - 8:tm, :]
        rw_ref[:, cs] = z + (prev - z) * mu_ref[:, cs]
    ml_ref[...] = _dot(u, w_ref[:, RW_COLS:RW_COLS + ML_COLS])
    for c in range(GT_COLS // cw):
        gt_ref[:, c * cw:(c + 1) * cw] = _dot(
            u, w_ref[:, RW_COLS + ML_COLS + c * cw:RW_COLS + ML_COLS + (c + 1) * cw])


def _inproj(x2, g, w_a, mu_a, seq):
    t_tok = x2.shape[0]
    tm = min(TOKEN_TILE, seq)
    ncol = RW_COLS + ML_COLS + GT_COLS
    kern = functools.partial(_inproj_kernel, tiles_per_seq=seq // tm, tm=tm)
    return pl.pallas_call(
        kern,
        grid=(t_tok // tm,),
        in_specs=[
            pl.BlockSpec((tm, D_MODEL), lambda i: (i, 0)),
            pl.BlockSpec((1, D_MODEL), lambda i: (0, 0)),
            pl.BlockSpec((D_MODEL, ncol), lambda i: (0, 0)),
            pl.BlockSpec((1, RW_COLS), lambda i: (0, 0)),
        ],
        out_specs=[
            pl.BlockSpec((tm, RW_COLS), lambda i: (i, 0)),
            pl.BlockSpec((tm, ML_COLS), lambda i: (i, 0)),
            pl.BlockSpec((tm, GT_COLS), lambda i: (i, 0)),
        ],
        out_shape=[
            jax.ShapeDtypeStruct((t_tok, RW_COLS), F32),
            jax.ShapeDtypeStruct((t_tok, ML_COLS), F32),
            jax.ShapeDtypeStruct((t_tok, GT_COLS), F32),
        ],
        scratch_shapes=[pltpu.VMEM((8, RW_COLS), F32)],
        compiler_params=_cparams(("arbitrary",)),
        name="inproj",
    )(x2, g, w_a, mu_a)


def _split3(x):
    hi = x.astype(BF16)
    r1 = x - hi.astype(F32)
    mid = r1.astype(BF16)
    lo = (r1 - mid.astype(F32)).astype(BF16)
    return hi, mid, lo


def _rwkv_masks():
    n = 2 * CHUNK
    row = lax.broadcasted_iota(jnp.int32, (n, n), 0)
    col = lax.broadcasted_iota(jnp.int32, (n, n), 1)
    same = (row // CHUNK) == (col // CHUNK)
    strict = jnp.where(same & (row > col), 1.0, 0.0).astype(F32)
    incl = jnp.where(same & (row >= col), 1.0, 0.0).astype(F32)
    m16 = jnp.where((row // 16) == (col // 16), 1.0, 0.0).astype(F32)
    m32 = jnp.where(((row // 32) == (col // 32)) & ((row // 16) > (col // 16)), 1.0, 0.0).astype(F32)
    m64 = jnp.where((row // 32) > (col // 32), 1.0, 0.0).astype(F32)
    eye = jnp.where(row == col, 1.0, 0.0).astype(F32)
    headsel = (row // CHUNK) == (col // RWKV_HEAD)
    return strict, incl, m16, m32, m64, eye, headsel


def _rows(x, blk):
    return jnp.concatenate([x[s:s + blk] for s in range(blk, x.shape[0], 2 * blk)], axis=0)


def _merge_rows(x, odd, blk):
    parts = []
    for j, s in enumerate(range(0, x.shape[0], 2 * blk)):
        parts += [x[s:s + blk], odd[j * blk:(j + 1) * blk]]
    return jnp.concatenate(parts, axis=0)


def _rwkv_wave(refs, pairs, nchunk, consts, head_sum):
    (zr_ref, zk_ref, zv_ref, w0_ref, a0_ref, kk_ref, ka_ref, rk_ref, lnw_ref, lnb_ref,
     wup_ref, aup_ref, gup_ref, y_ref, h_ref, tanh_zw, za, sig_zg) = refs
    strict, incl, m16, m32, m64, eye, headsel, tril = consts
    n = 2 * CHUNK
    bf = lambda x: x.astype(BF16)
    items = [(q, slice(c * CHUNK, (c + 1) * CHUNK)) for c in range(nchunk) for q in range(len(pairs))]
    rng = range(len(items))
    c = {}

    def stack(x):
        return jnp.where(headsel, jnp.concatenate([x, x], axis=0), 0.0)

    def pre():
        c["keep"] = []
        streams = []
        for p in pairs:
            ls = slice(p * PAIR, (p + 1) * PAIR)
            zr, zk, zv = zr_ref[:, ls], zk_ref[:, ls], zv_ref[:, ls]
            w_pre = w0_ref[:, ls] + _dot(tanh_zw, wup_ref[:, ls])
            lw = -math.exp(-0.5) * _sigmoid(w_pre)
            iclr = _sigmoid(a0_ref[:, ls] + _dot(za, aup_ref[:, ls]))
            gate = _dot(sig_zg, gup_ref[:, ls])
            kk = zk * kk_ref[:, ls]
            kk = kk / jnp.maximum(jnp.sqrt(head_sum(kk * kk)), 1e-12)
            k = zk * (1.0 + (iclr - 1.0) * ka_ref[:, ls])
            streams.append((zr, k, zv, -kk, kk * iclr, lw))
            c["keep"].append((zr, k, zv, gate))
        for j, name in enumerate(("r", "k", "v", "a", "b", "lw")):
            c[name] = [streams[q][j][s] for q, s in items]

    def cumsum():
        c["cs"] = []
        for i in rng:
            hi, mid, lo = _split3(c["lw"][i])
            c["cs"].append(_dot(tril, hi) + _dot(tril, mid) + _dot(tril, lo))

    def scale():
        cs, lw = c["cs"], c["lw"]
        g_in = [jnp.exp(cs[i]) for i in rng]
        g_ex = [jnp.exp(cs[i] - lw[i]) for i in rng]
        g_inv = [jnp.exp(-cs[i]) for i in rng]
        c["g_last"] = [jnp.exp(cs[i][CHUNK - 1:CHUNK, :]) for i in rng]
        c["rt"] = [stack(c["r"][i] * g_in[i]) for i in rng]
        c["at"] = [bf(stack(c["a"][i] * g_ex[i])) for i in rng]
        bt = [stack(c["b"][i] * g_inv[i]) for i in rng]
        kt = [stack(c["k"][i] * g_inv[i]) for i in rng]
        c["bh"] = [bf(bt[i] * c["g_last"][i]) for i in rng]
        c["kh"] = [bf(kt[i] * c["g_last"][i]) for i in rng]
        c["v2"] = [bf(stack(c["v"][i])) for i in rng]
        c["bk"] = [bf(jnp.concatenate([bt[i], kt[i]], axis=0)) for i in rng]

    def gram():
        ga = [_dot_nt(c["at"][i], c["bk"][i]) for i in rng]
        gr = [_dot_nt(bf(c["rt"][i]), c["bk"][i]) for i in rng]
        c["a_ab"] = [g[:, :n] * strict for g in ga]
        c["a_kr"] = [bf(jnp.concatenate([ga[i][:, n:] * strict, gr[i][:, n:] * incl], axis=0)) for i in rng]
        c["a_rb"] = [bf(g[:, :n] * incl) for g in gr]

    def inv0():
        c["d"] = [bf(a * m16) for a in c["a_ab"]]
        c["t"] = [eye + a * m16 for a in c["a_ab"]]
        c["p"] = [_dot(d, d) for d in c["d"]]

    def inv_double():
        xs = [_dot(bf(jnp.concatenate([t, p], axis=0)), bf(p)) for t, p in zip(c["t"], c["p"])]
        c["t"] = [t + x[:n] for t, x in zip(c["t"], xs)]
        c["p"] = [x[n:] for x in xs]

    def inv_last():
        c["t"] = [t + _dot(bf(t), bf(p)) for t, p in zip(c["t"], c["p"])]

    def level_a(msk, blk):
        def f():
            c["tb"] = [bf(t) for t in c["t"]]
            c["x"] = [bf(_dot(_rows(tb, blk), bf(a * msk))) for tb, a in zip(c["tb"], c["a_ab"])]
        return f

    def level_b(blk):
        def f():
            c["t"] = [_merge_rows(t, _rows(t, blk) + _dot(x, tb), blk)
                      for t, x, tb in zip(c["t"], c["x"], c["tb"])]
        return f

    def apply_v():
        c["xv"] = [_dot(c["a_kr"][i], c["v2"][i]) for i in rng]

    def apply_t():
        c["wu"] = [bf(_dot(bf(c["t"][i]), jnp.concatenate([c["at"][i], bf(c["xv"][i][:n])], axis=1)))
                   for i in rng]

    def apply_rb():
        c["qy"] = [_dot(c["a_rb"][i], c["wu"][i]) for i in rng]
        c["mn"] = [_dot_tn(c["bh"][i], c["wu"][i]) for i in rng]

    def assemble():
        kv = [_dot_tn(c["kh"][i], c["v2"][i]) for i in rng]
        qeff = [c["rt"][i] + c["qy"][i][:, :n] for i in rng]
        c["y0"] = [c["qy"][i][:, n:] + c["xv"][i][n:] for i in rng]
        m = [eye * c["g_last"][i] + c["mn"][i][:, :n] for i in rng]
        c["nn"] = [c["mn"][i][:, n:] + kv[i] for i in rng]
        c["mq"] = [bf(jnp.concatenate([m[i], qeff[i]], axis=0)) for i in rng]

    def chain():
        hs = [h_ref[p] for p in pairs]
        ys = [[] for _ in pairs]
        for i in rng:
            q = items[i][0]
            hy = _dot(c["mq"][i], bf(hs[q]))
            hs[q] = hy[:n] + c["nn"][i]
            y2 = hy[n:] + c["y0"][i]
            ys[q].append(y2[:CHUNK, :] + y2[CHUNK:, :])
        for q, p in enumerate(pairs):
            h_ref[p] = hs[q]
        c["ys"] = [jnp.concatenate(y, axis=0) for y in ys]

    def post():
        for q, p in enumerate(pairs):
            ls = slice(p * PAIR, (p + 1) * PAIR)
            zr, k, zv, gate = c["keep"][q]
            y = c["ys"][q]
            mean = head_sum(y) * (1.0 / RWKV_HEAD)
            yc = y - mean
            var = head_sum(yc * yc) * (1.0 / RWKV_HEAD)
            yn = yc * lax.rsqrt(var + GN_EPS) * lnw_ref[:, ls] + lnb_ref[:, ls]
            bonus = head_sum(zr * k * rk_ref[:, ls]) * zv
            y_ref[:, ls] = ((yn + bonus) * gate).astype(y_ref.dtype)

    return [pre, cumsum, scale, gram, inv0, inv_double, inv_double, inv_last,
            level_a(m32, 16), level_b(16), level_a(m64, 32), level_b(32),
            apply_v, apply_t, apply_rb, assemble, chain, post]


def _rwkv_kernel(zr_ref, zk_ref, zv_ref, zwa_ref, zg_ref,
                 w0_ref, a0_ref, kk_ref, ka_ref, rk_ref, lnw_ref, lnb_ref,
                 wup_ref, aup_ref, gup_ref, y_ref, h_ref, *, tb, npair):
    @pl.when(pl.program_id(2) == 0)
    def _():
        h_ref[...] = jnp.zeros_like(h_ref)

    lane = lax.broadcasted_iota(jnp.int32, (tb, PAIR), 1)
    head0 = lane < RWKV_HEAD

    def head_sum(x):
        s0 = jnp.sum(jnp.where(head0, x, 0.0), axis=-1, keepdims=True)
        s1 = jnp.sum(jnp.where(head0, 0.0, x), axis=-1, keepdims=True)
        return jnp.where(head0, s0, s1)

    trow = lax.broadcasted_iota(jnp.int32, (CHUNK, CHUNK), 0)
    tcol = lax.broadcasted_iota(jnp.int32, (CHUNK, CHUNK), 1)
    tril = jnp.where(trow >= tcol, 1.0, 0.0).astype(BF16)
    consts = _rwkv_masks() + (tril,)
    refs = (zr_ref, zk_ref, zv_ref, w0_ref, a0_ref, kk_ref, ka_ref, rk_ref, lnw_ref, lnb_ref,
            wup_ref, aup_ref, gup_ref, y_ref, h_ref,
            jnp.tanh(zwa_ref[...]).astype(BF16), zwa_ref[...].astype(BF16),
            _sigmoid(zg_ref[...]).astype(BF16))
    for stage in _rwkv_wave(refs, list(range(npair)), tb // CHUNK, consts, head_sum):
        stage()


def _rwkv(rw, prm, wup, aup, gup, batch, seq):
    t_tok = rw.shape[0]
    tb = min(RWKV_TILE, seq)
    nt = seq // tb
    npair = RWKV_PAIRS
    wid = npair * PAIR
    ngrp = RWKV_DIM // wid

    def tok(base):
        return pl.BlockSpec((tb, wid), lambda b, p, t: (b * nt + t, base * ngrp + p))

    def lora(base, width):
        return pl.BlockSpec((tb, width), lambda b, p, t: (b * nt + t, base))

    prm_spec = pl.BlockSpec((1, wid), lambda b, p, t: (0, p))
    in_specs = [
        tok(0), tok(1), tok(2),
        lora((3 * RWKV_DIM + ZG_PAD) // 128, 128), lora(3 * RWKV_DIM // ZG_PAD, ZG_PAD),
    ] + [prm_spec] * 7 + [
        pl.BlockSpec((128, wid), lambda b, p, t: (0, p)),
        pl.BlockSpec((128, wid), lambda b, p, t: (0, p)),
        pl.BlockSpec((ZG_PAD, wid), lambda b, p, t: (0, p)),
    ]
    return pl.pallas_call(
        functools.partial(_rwkv_kernel, tb=tb, npair=npair),
        grid=(batch, ngrp, nt),
        in_specs=in_specs,
        out_specs=pl.BlockSpec((tb, wid), lambda b, p, t: (b * nt + t, p)),
        out_shape=jax.ShapeDtypeStruct((t_tok, RWKV_DIM), BF16),
        scratch_shapes=[pltpu.VMEM((npair, PAIR, PAIR), F32)],
        compiler_params=_cparams(("arbitrary", "arbitrary", "arbitrary")),
        name="rwkv7",
    )(rw, rw, rw, rw, rw, *prm, wup, aup, gup)


def _mla_prep_kernel(ml_ref, posrow_ref, fcol_ref, qn_ref, kvn_ref, wqt_ref, wqrt_ref, wk_ref, wvt_ref,
                     q_ref, k_ref, vt_ref, *, tk):
    ml = ml_ref[...]
    ang_t = fcol_ref[...] * posrow_ref[...].astype(F32)
    cos_t = jnp.cos(ang_t)
    sin_t = jnp.sin(ang_t)
    zlo = jnp.zeros((NOPE_DIM, tk), F32)
    zhi = jnp.zeros((HEAD_PAD - NOPE_DIM - ROPE_DIM, tk), F32)
    cosf = jnp.concatenate([zlo, cos_t, zhi], axis=0).T
    sinf = jnp.concatenate([zlo, sin_t, zhi], axis=0).T
    cq = _rms(ml[:, :Q_LORA], qn_ref[...]).astype(BF16)
    ckv = _rms(ml[:, Q_LORA:Q_LORA + KV_LORA], kvn_ref[...]).astype(BF16)
    krb = ml[:, Q_LORA + KV_LORA:Q_LORA + KV_LORA + HEAD_PAD]
    k_rope = krb * cosf + pltpu.roll(krb, HEAD_PAD - ROPE_DIM, 1) * sinf
    rope = slice(NOPE_DIM, NOPE_DIM + ROPE_DIM)
    kf = _dot(ckv, wk_ref[...])
    qf_all = _dot_nt(wqt_ref[...], cq)
    qr_all = _dot_nt(wqrt_ref[...], cq)
    for h in range(MLA_HEADS):
        hs = slice(h * HEAD_PAD, (h + 1) * HEAD_PAD)
        qf_t = qf_all[hs]
        qr_t = qr_all[h * ROPE_DIM:(h + 1) * ROPE_DIM]
        q_t = jnp.concatenate([qf_t[:NOPE_DIM], qf_t[rope] * cos_t + qr_t * sin_t,
                               qf_t[NOPE_DIM + ROPE_DIM:]], axis=0)
        q_ref[h] = q_t.astype(q_ref.dtype)
        k_ref[h] = (kf[:, hs] + k_rope).astype(k_ref.dtype)
        vt_ref[h, :V_DIM] = _dot_nt(wvt_ref[h], ckv).astype(vt_ref.dtype)
        vt_ref[h, V_DIM:] = jnp.ones((V_EXT - V_DIM, tk), vt_ref.dtype)


def _mla_prep(ml, posrow, fcol, qn, kvn, wqt, wqrt, wk, wvt, batch, seq, tk):
    nk = seq // tk
    full = lambda shape: pl.BlockSpec(shape, lambda b, t: (0,) * len(shape))
    return pl.pallas_call(
        functools.partial(_mla_prep_kernel, tk=tk),
        grid=(batch, nk),
        in_specs=[
            pl.BlockSpec((tk, ML_COLS), lambda b, t: (b * nk + t, 0)),
            pl.BlockSpec((None, 1, tk), lambda b, t: (b * nk + t, 0, 0)),
            full((ROPE_DIM, 1)), full((1, Q_LORA)), full((1, KV_LORA)),
            full((MLA_HEADS * HEAD_PAD, Q_LORA)), full((MLA_HEADS * ROPE_DIM, Q_LORA)),
            full((KV_LORA, MLA_HEADS * HEAD_PAD)), full((MLA_HEADS, V_DIM, KV_LORA)),
        ],
        out_specs=[
            pl.BlockSpec((None, MLA_HEADS, None, HEAD_PAD, tk), lambda b, t: (b, 0, t, 0, 0)),
            pl.BlockSpec((None, MLA_HEADS, None, tk, HEAD_PAD), lambda b, t: (b, 0, t, 0, 0)),
            pl.BlockSpec((None, MLA_HEADS, None, V_EXT, tk), lambda b, t: (b, 0, t, 0, 0)),
        ],
        out_shape=[
            jax.ShapeDtypeStruct((batch, MLA_HEADS, nk, HEAD_PAD, tk), BF16),
            jax.ShapeDtypeStruct((batch, MLA_HEADS, nk, tk, HEAD_PAD), BF16),
            jax.ShapeDtypeStruct((batch, MLA_HEADS, nk, V_EXT, tk), BF16),
        ],
        compiler_params=_cparams(("arbitrary", "arbitrary")),
        name="mla_prep",
    )(ml, posrow, fcol, qn, kvn, wqt, wqrt, wk, wvt)


def _attn_kernel(q_ref, k_ref, vt_ref, o_ref, m_ref, acc_ref, s_ref, p_ref, al_ref,
                 *, tk, heads, ks, ws, ahead, dlag, behind):
    i = pl.program_id(2)
    krow = lax.broadcasted_iota(jnp.int32, (ks, ws), 0)
    qcol = lax.broadcasted_iota(jnp.int32, (ks, ws), 1)
    neg = jnp.finfo(F32).min
    nsub, nstrip = tk // ks, tk // ws
    items = [(h, g) for h in range(heads) for g in range(nstrip)]
    rng = range(len(items))
    qs = [q_ref[h, 0, :, g * ws:(g + 1) * ws] for h, g in items]

    full_units = [(sub, i) for sub in range(nsub) for i in rng]
    diag_units = [(sub, i) for sub, i in full_units if not sub * ks > items[i][1] * ws + ws - 1]
    lag = dlag + behind
    tail_units = full_units[len(full_units) - lag:]
    assert diag_units[:ahead] == full_units[:ahead] and len(diag_units) > ahead + lag

    def score(j, unit):
        sub, i = unit
        return _dot(k_ref[items[i][0], j, sub * ks:(sub + 1) * ks, :], qs[i])

    def pv(j, unit, p):
        sub, i = unit
        return _dot(vt_ref[items[i][0], j, :, sub * ks:(sub + 1) * ks], p)

    def step(j, masked):
        units = diag_units if masked else full_units
        nu = len(units)
        keep = 0 if masked else lag
        jprev = jnp.maximum(j - 1, 0)
        s_val, p_val, pv_val, al_val, pv_old = {}, {}, {}, {}, {}
        for t in range(nu + (lag if masked else 0)):
            if t + ahead < nu:
                s_val[t + ahead] = score(j, units[t + ahead])
            if t < nu:
                sub, i = units[t]
                s = s_val.pop(t) if t >= ahead else s_ref[t]
                if masked and sub * ks + ks - 1 > items[i][1] * ws:
                    s = jnp.where(krow + (sub * ks - items[i][1] * ws) <= qcol, s, neg)
                m_old = m_ref[i]
                m_new = jnp.maximum(m_old, jnp.max(s, axis=0, keepdims=True))
                al_val[t] = jnp.exp2(m_old - m_new)
                m_ref[i] = m_new
                p_val[t] = jnp.exp2(s - m_new).astype(BF16)
                if not masked and t + ahead >= nu:
                    s_ref[t + ahead - nu] = score(j + 1, full_units[t + ahead - nu])
            if t < lag:
                pv_old[t] = pv(jprev, tail_units[t], p_ref[t])
            d = t - dlag
            if 0 <= d < nu - keep:
                pv_val[d] = pv(j, units[d], p_val.pop(d))
            x = t - behind
            if 0 <= x < lag:
                i = tail_units[x][1]
                acc_ref[i] = al_ref[x] * acc_ref[i] + pv_old.pop(x)
            w = d - behind
            if 0 <= w < nu - keep:
                i = units[w][1]
                acc_ref[i] = al_val.pop(w) * acc_ref[i] + pv_val.pop(w)
        if not masked:
            for x in range(lag):
                p_ref[x] = p_val[nu - lag + x]
                al_ref[x] = al_val[nu - lag + x]

    m_ref[...] = jnp.full(m_ref.shape, neg, F32)
    acc_ref[...] = jnp.zeros(acc_ref.shape, F32)
    p_ref[...] = jnp.zeros(p_ref.shape, BF16)
    al_ref[...] = jnp.ones(al_ref.shape, F32)
    for t in range(ahead):
        s_ref[t] = score(0, full_units[t])

    def body(j, c):
        step(j, False)
        return c

    lax.fori_loop(0, i, body, 0)
    step(i, True)
    outs = []
    for h in range(heads):
        accs = [acc_ref[h * nstrip + g] for g in range(nstrip)]
        acc = jnp.concatenate([a[:V_DIM] / a[V_DIM:V_DIM + 1] for a in accs], axis=1)
        outs.append(acc.T)
    o_ref[...] = jnp.concatenate(outs, axis=-1).astype(o_ref.dtype)


def _attn(q5, k5, vt5, batch, seq, tk):
    nk = seq // tk
    heads = 8
    ngrp = MLA_HEADS // heads
    ks, ws = min(256, tk), min(256, tk)
    ahead, dlag, behind = 3, 2, 2
    nitem = heads * (tk // ws)
    return pl.pallas_call(
        functools.partial(_attn_kernel, tk=tk, heads=heads, ks=ks, ws=ws, ahead=ahead, dlag=dlag, behind=behind),
        grid=(batch, ngrp, nk),
        in_specs=[
            pl.BlockSpec((None, heads, 1, HEAD_PAD, tk), lambda b, g, i: (b, g, i, 0, 0)),
            pl.BlockSpec((None, heads, nk, tk, HEAD_PAD), lambda b, g, i: (b, g, 0, 0, 0)),
            pl.BlockSpec((None, heads, nk, V_EXT, tk), lambda b, g, i: (b, g, 0, 0, 0)),
        ],
        out_specs=pl.BlockSpec((tk, heads * V_DIM), lambda b, g, i: (b * nk + i, g)),
        out_shape=jax.ShapeDtypeStruct((batch * seq, MLA_HEADS * V_DIM), BF16),
        scratch_shapes=[
            pltpu.VMEM((nitem, 1, ws), F32),
            pltpu.VMEM((nitem, V_EXT, ws), F32),
            pltpu.VMEM((ahead, ks, ws), F32),
            pltpu.VMEM((dlag + behind, ks, ws), BF16),
            pltpu.VMEM((dlag + behind, 1, ws), F32),
        ],
        compiler_params=_cparams(("arbitrary", "arbitrary", "arbitrary")),
        name="mla_attn",
    )(q5, k5, vt5)


def _tail_kernel(x_ref, yr_ref, o_ref, gt_ref, wor_ref, wom_ref, wout_ref, nf_ref, wup_ref, wdn_ref,
                 nfin_ref, out_ref):
    y_a = _dot(yr_ref[...], wor_ref[...])
    y_b = _dot(o_ref[...], wom_ref[...])
    merged = _sigmoid(gt_ref[:, :D_MODEL]) * y_a + _sigmoid(gt_ref[:, D_MODEL:]) * y_b
    h = x_ref[...] + _dot(merged.astype(BF16), wout_ref[...])
    f_in = _rms(h, nf_ref[...]).astype(BF16)
    cw = 1024
    acc = h
    for c in range(D_FF // cw):
        f = _dot(f_in, wup_ref[:, c * cw:(c + 1) * cw])
        f = jnp.square(jnp.maximum(f, 0.0)).astype(BF16)
        acc = acc + _dot(f, wdn_ref[c * cw:(c + 1) * cw, :])
    out_ref[...] = _rms(acc, nfin_ref[...])


def _tail(x2, yr, o, gt, wor, wom, wout, nf, wup, wdn, nfin, seq):
    t_tok = x2.shape[0]
    tm = min(TOKEN_TILE, seq)
    full = lambda shape: pl.BlockSpec(shape, lambda i: (0,) * len(shape), pipeline_mode=pl.Buffered(1))
    return pl.pallas_call(
        _tail_kernel,
        grid=(t_tok // tm,),
        in_specs=[
            pl.BlockSpec((tm, D_MODEL), lambda i: (i, 0)),
            pl.BlockSpec((tm, RWKV_DIM), lambda i: (i, 0)),
            pl.BlockSpec((tm, MLA_HEADS * V_DIM), lambda i: (i, 0)),
            pl.BlockSpec((tm, GT_COLS), lambda i: (i, 0)),
            full((RWKV_DIM, D_MODEL)), full((MLA_HEADS * V_DIM, D_MODEL)), full((D_MODEL, D_MODEL)),
            full((1, D_MODEL)), full((D_MODEL, D_FF)), full((D_FF, D_MODEL)), full((1, D_MODEL)),
        ],
        out_specs=pl.BlockSpec((tm, D_MODEL), lambda i: (i, 0)),
        out_shape=jax.ShapeDtypeStruct((t_tok, D_MODEL), F32),
        compiler_params=_cparams(("arbitrary",)),
        name="merge_ffn",
    )(x2, yr, o, gt, wor, wom, wout, nf, wup, wdn, nfin)


def _padc(w, n):
    return jnp.pad(w, ((0, 0), (0, n - w.shape[1])))


def _padr(w, n):
    return jnp.pad(w, ((0, n - w.shape[0]), (0, 0)))


def _rot_half(w):
    half = w.shape[-1] // 2
    return jnp.concatenate([-w[..., half:], w[..., :half]], axis=-1)


def kernel(x, positions, norm_mix, w_in, mu_shift, w0, w_up, a0, a_up, g_up, k_k, k_a, r_k, ln_w, ln_b, w_o_rwkv, q_norm, w_uq, kv_norm, w_ukv, w_o_mla, w_out, norm_ffn, w_ff_up, w_ff_down, norm_final):
    batch, seq, _ = x.shape
    t_tok = batch * seq
    x2 = x.reshape(t_tok, D_MODEL)
    l = 0

    wi = w_in[l]
    o = 0
    w_r3 = wi[:, o:o + 3 * RWKV_DIM]; o += 3 * RWKV_DIM
    w_zw = wi[:, o:o + DECAY_LORA]; o += DECAY_LORA
    w_za = wi[:, o:o + ICLR_LORA]; o += ICLR_LORA
    w_zg = wi[:, o:o + GATE_LORA]; o += GATE_LORA
    w_cq = wi[:, o:o + Q_LORA]; o += Q_LORA
    w_ckv = wi[:, o:o + KV_LORA]; o += KV_LORA
    w_kr = wi[:, o:o + ROPE_DIM]; o += ROPE_DIM
    w_gate = wi[:, o:o + 2 * D_MODEL]
    zeros64 = jnp.zeros((D_MODEL, NOPE_DIM), wi.dtype)
    w_krb = jnp.concatenate([zeros64, w_kr, _rot_half(w_kr)], axis=1)
    w_a = jnp.concatenate([
        w_r3, _padc(w_zg, ZG_PAD), w_zw, w_za, w_cq, w_ckv, w_krb, w_gate], axis=1).astype(BF16)
    mu = mu_shift[l]
    o = 3 * RWKV_DIM
    lo = o + DECAY_LORA + ICLR_LORA
    mu_a = jnp.concatenate([mu[:o], jnp.pad(mu[lo:], (0, ZG_PAD - GATE_LORA)), mu[o:lo]])[None, :]

    rw, ml, gt = _inproj(x2, norm_mix[l][None, :], w_a, mu_a, seq)

    prm = [p[None, :] for p in (w0[l], a0[l], k_k[l], k_a[l], r_k[l].reshape(-1), ln_w[l], ln_b[l])]
    aup = jnp.concatenate([jnp.zeros_like(a_up[l]), a_up[l]], axis=0)
    yr = _rwkv(rw, prm, _padr(w_up[l], 128).astype(BF16), aup.astype(BF16),
               _padr(g_up[l], ZG_PAD).astype(BF16), batch, seq)

    half = ROPE_DIM // 2
    inv_freq = 1.0 / (ROPE_THETA ** (jnp.arange(half, dtype=F32) * (2.0 / ROPE_DIM)))
    scale = (NOPE_DIM + ROPE_DIM) ** -0.5 * math.log2(math.e)
    wq = w_uq[l].reshape(Q_LORA, MLA_HEADS, NOPE_DIM + ROPE_DIM) * scale
    wq_p = jnp.pad(wq, ((0, 0), (0, 0), (0, HEAD_PAD - NOPE_DIM - ROPE_DIM)))
    wqt = jnp.transpose(wq_p, (1, 2, 0)).reshape(MLA_HEADS * HEAD_PAD, Q_LORA)
    wqrt = jnp.transpose(_rot_half(wq[..., NOPE_DIM:]), (1, 2, 0)).reshape(MLA_HEADS * ROPE_DIM, Q_LORA)
    fcol = jnp.concatenate([inv_freq, inv_freq])[:, None]
    wkv = w_ukv[l].reshape(KV_LORA, MLA_HEADS, NOPE_DIM + V_DIM)
    wk_p = jnp.pad(wkv[..., :NOPE_DIM], ((0, 0), (0, 0), (0, HEAD_PAD - NOPE_DIM)))
    wvt = jnp.transpose(wkv[..., NOPE_DIM:], (1, 2, 0))
    tk = min(TOKEN_TILE, seq)
    q5, k5, vt5 = _mla_prep(
        ml, positions.reshape(t_tok // tk, 1, tk), fcol,
        q_norm[l][None, :], kv_norm[l][None, :], wqt.astype(BF16), wqrt.astype(BF16),
        wk_p.reshape(KV_LORA, -1).astype(BF16), wvt.astype(BF16), batch, seq, tk)
    o_att = _attn(q5, k5, vt5, batch, seq, tk)

    out = _tail(x2, yr, o_att, gt, w_o_rwkv[l].astype(BF16), w_o_mla[l].astype(BF16),
                w_out[l].astype(BF16), norm_ffn[l][None, :], w_ff_up[l].astype(BF16),
                w_ff_down[l].astype(BF16), norm_final[None, :], seq)
    return out.reshape(batch, seq, D_MODEL)
```

```python
import functools
import math

import jax
import jax.numpy as jnp
from jax import lax
from jax.experimental import pallas as pl
from jax.experimental.pallas import tpu as pltpu

F32 = jnp.float32
BF16 = jnp.bfloat16

D_MODEL = 1024
NORM_EPS = 1e-6
RWKV_HEAD = 64
RWKV_HEADS = 8
RWKV_DIM = RWKV_HEADS * RWKV_HEAD
DECAY_LORA = 64
ICLR_LORA = 64
GATE_LORA = 160
GN_EPS = 64e-5
MLA_HEADS = 8
Q_LORA = 256
KV_LORA = 128
NOPE_DIM = 64
ROPE_DIM = 32
V_DIM = 64
ROPE_THETA = 10000.0
D_FF = 4 * D_MODEL

LANE = 128
CHUNK = 64
INV_BASE = 8
PAIR = 2 * RWKV_HEAD
HEAD_PAD = 128
V_EXT = V_DIM + 16

ZG_PAD = 256
RW_COLS = 3 * RWKV_DIM + ZG_PAD + 128
ML_COLS = Q_LORA + KV_LORA + 128
GT_COLS = 2 * D_MODEL

VMEM_BYTES_V7X = 64 * 1024 * 1024
VMEM_LIMIT = VMEM_BYTES_V7X * 7 // 8
TOKEN_TILE = 512
RWKV_TILE = 4 * CHUNK
RWKV_PAIRS = RWKV_DIM // PAIR


def _cparams(sem):
    return pltpu.CompilerParams(dimension_semantics=sem, vmem_limit_bytes=VMEM_LIMIT)


def _sigmoid(x):
    return 1.0 / (1.0 + jnp.exp(-x))


def _rms(x, g):
    ms = jnp.mean(x * x, axis=-1, keepdims=True)
    return x * lax.rsqrt(ms + NORM_EPS) * g


def _dot(a, b):
    return jnp.dot(a, b, preferred_element_type=F32)


def _dot_nt(a, b):
    return lax.dot_general(a, b, (((1,), (1,)), ((), ())), preferred_element_type=F32)


def _dot_tn(a, b):
    return lax.dot_general(a, b, (((0,), (0,)), ((), ())), preferred_element_type=F32)


def _inproj_kernel(x_ref, g_ref, w_ref, mu_ref, rw_ref, ml_ref, gt_ref, carry_ref,
                   *, tiles_per_seq, tm):
    i = pl.program_id(0)
    u = _rms(x_ref[...], g_ref[...]).astype(BF16)

    @pl.when(i % tiles_per_seq == 0)
    def _():
        carry_ref[...] = jnp.zeros_like(carry_ref)

    cw = 512
    for c0 in range(0, RW_COLS, cw):
        cs = slice(c0, min(c0 + cw, RW_COLS))
        z = _dot(u, w_ref[:, cs])
        prev = pltpu.roll(z, 1, 0)
        row0 = lax.broadcasted_iota(jnp.int32, z.shape, 0) == 0
        prev = jnp.where(row0, carry_ref[7:8, cs], prev)
        carry_ref[:, cs] = z[tm - 8:tm, :]
        rw_ref[:, cs] = z + (prev - z) * mu_ref[:, cs]
    ml_ref[...] = _dot(u, w_ref[:, RW_COLS:RW_COLS + ML_COLS])
    for c in range(GT_COLS // cw):
        gt_ref[:, c * cw:(c + 1) * cw] = _dot(
            u, w_ref[:, RW_COLS + ML_COLS + c * cw:RW_COLS + ML_COLS + (c + 1) * cw])


def _inproj(x2, g, w_a, mu_a, seq):
    t_tok = x2.shape[0]
    tm = min(TOKEN_TILE, seq)
    ncol = RW_COLS + ML_COLS + GT_COLS
    kern = functools.partial(_inproj_kernel, tiles_per_seq=seq // tm, tm=tm)
    return pl.pallas_call(
        kern,
        grid=(t_tok // tm,),
        in_specs=[
            pl.BlockSpec((tm, D_MODEL), lambda i: (i, 0)),
            pl.BlockSpec((1, D_MODEL), lambda i: (0, 0)),
            pl.BlockSpec((D_MODEL, ncol), lambda i: (0, 0)),
            pl.BlockSpec((1, RW_COLS), lambda i: (0, 0)),
        ],
        out_specs=[
            pl.BlockSpec((tm, RW_COLS), lambda i: (i, 0)),
            pl.BlockSpec((tm, ML_COLS), lambda i: (i, 0)),
            pl.BlockSpec((tm, GT_COLS), lambda i: (i, 0)),
        ],
        out_shape=[
            jax.ShapeDtypeStruct((t_tok, RW_COLS), F32),
            jax.ShapeDtypeStruct((t_tok, ML_COLS), F32),
            jax.ShapeDtypeStruct((t_tok, GT_COLS), F32),
        ],
        scratch_shapes=[pltpu.VMEM((8, RW_COLS), F32)],
        compiler_params=_cparams(("arbitrary",)),
        name="inproj",
    )(x2, g, w_a, mu_a)


def _split3(x):
    hi = x.astype(BF16)
    r1 = x - hi.astype(F32)
    mid = r1.astype(BF16)
    lo = (r1 - mid.astype(F32)).astype(BF16)
    return hi, mid, lo


def _rwkv_masks():
    n = 2 * CHUNK
    row = lax.broadcasted_iota(jnp.int32, (n, n), 0)
    col = lax.broadcasted_iota(jnp.int32, (n, n), 1)
    same = (row // CHUNK) == (col // CHUNK)
    strict = jnp.where(same & (row > col), 1.0, 0.0).astype(F32)
    incl = jnp.where(same & (row >= col), 1.0, 0.0).astype(F32)
    base = jnp.where((row // INV_BASE) == (col // INV_BASE), 1.0, 0.0).astype(F32)
    levels = []
    blk = INV_BASE
    while blk < CHUNK:
        levels.append((jnp.where(((row // (2 * blk)) == (col // (2 * blk))) & ((row // blk) > (col // blk)),
                                 1.0, 0.0).astype(F32), blk))
        blk *= 2
    eye = jnp.where(row == col, 1.0, 0.0).astype(F32)
    headsel = (row // CHUNK) == (col // RWKV_HEAD)
    return strict, incl, base, levels, eye, headsel


def _rows(x, blk):
    return jnp.concatenate([x[s:s + blk] for s in range(blk, x.shape[0], 2 * blk)], axis=0)


def _merge_rows(x, odd, blk):
    parts = []
    for j, s in enumerate(range(0, x.shape[0], 2 * blk)):
        parts += [x[s:s + blk], odd[j * blk:(j + 1) * blk]]
    return jnp.concatenate(parts, axis=0)


def _rwkv_wave(refs, pairs, nchunk, consts, head_sum):
    (zr_ref, zk_ref, zv_ref, w0_ref, a0_ref, kk_ref, ka_ref, rk_ref, lnw_ref, lnb_ref,
     wup_ref, aup_ref, gup_ref, y_ref, h_ref, tanh_zw, za, sig_zg) = refs
    strict, incl, base, levels, eye, headsel, tril = consts
    incl2 = jnp.concatenate([incl, incl], axis=1)
    n = 2 * CHUNK
    bf = lambda x: x.astype(BF16)
    items = [(q, slice(c * CHUNK, (c + 1) * CHUNK)) for c in range(nchunk) for q in range(len(pairs))]
    rng = range(len(items))
    c = {}

    def stack(x):
        return jnp.where(headsel, jnp.concatenate([x, x], axis=0), 0.0)

    def pre():
        c["keep"] = []
        streams = []
        for p in pairs:
            ls = slice(p * PAIR, (p + 1) * PAIR)
            zr, zk, zv = zr_ref[:, ls], zk_ref[:, ls], zv_ref[:, ls]
            w_pre = w0_ref[:, ls] + _dot(tanh_zw, wup_ref[:, ls])
            lw = -math.exp(-0.5) * _sigmoid(w_pre)
            iclr = _sigmoid(a0_ref[:, ls] + _dot(za, aup_ref[:, ls]))
            gate = _dot(sig_zg, gup_ref[:, ls])
            kk = zk * kk_ref[:, ls]
            kk = kk / jnp.maximum(jnp.sqrt(head_sum(kk * kk)), 1e-12)
            k = zk * (1.0 + (iclr - 1.0) * ka_ref[:, ls])
            streams.append((zr, k, zv, -kk, kk * iclr, lw))
            c["keep"].append((zr, k, zv, gate))
        for j, name in enumerate(("r", "k", "v", "a", "b", "lw")):
            c[name] = [streams[q][j][s] for q, s in items]

    def cumsum():
        c["cs"] = []
        for i in rng:
            hi, mid, lo = _split3(c["lw"][i])
            c["cs"].append(_dot(tril, hi) + _dot(tril, mid) + _dot(tril, lo))

    def scale():
        cs, lw = c["cs"], c["lw"]
        g_in = [jnp.exp(cs[i]) for i in rng]
        g_ex = [jnp.exp(cs[i] - lw[i]) for i in rng]
        g_inv = [jnp.exp(-cs[i]) for i in rng]
        c["g_last"] = [jnp.exp(cs[i][CHUNK - 1:CHUNK, :]) for i in rng]
        c["rt"] = [stack(c["r"][i] * g_in[i]) for i in rng]
        c["at"] = [bf(stack(c["a"][i] * g_ex[i])) for i in rng]
        bt = [stack(c["b"][i] * g_inv[i]) for i in rng]
        kt = [stack(c["k"][i] * g_inv[i]) for i in rng]
        c["bkh_t"] = [bf(jnp.concatenate([(bt[i] * c["g_last"][i]).T, (kt[i] * c["g_last"][i]).T], axis=1))
                      for i in rng]
        c["v2"] = [bf(stack(c["v"][i])) for i in rng]
        c["bk"] = [bf(jnp.concatenate([bt[i], kt[i]], axis=0)) for i in rng]

    def gram():
        ga = [_dot_nt(c["at"][i], c["bk"][i]) for i in rng]
        gr = [_dot_nt(bf(c["rt"][i]), c["bk"][i]) for i in rng]
        c["a_ab"] = [g[:, :n] * strict for g in ga]
        c["a_ak"] = [bf(g[:, n:] * strict) for g in ga]
        c["a_rbk"] = [bf(g * incl2) for g in gr]

    def inv0():
        c["d"] = [bf(a * base) for a in c["a_ab"]]
        c["t"] = [eye + a * base for a in c["a_ab"]]
        c["p"] = [_dot(d, d) for d in c["d"]]

    def inv_double():
        xs = [_dot(bf(jnp.concatenate([t, p], axis=0)), bf(p)) for t, p in zip(c["t"], c["p"])]
        c["t"] = [t + x[:n] for t, x in zip(c["t"], xs)]
        c["p"] = [x[n:] for x in xs]

    def inv_last():
        c["t"] = [t + _dot(bf(t), bf(p)) for t, p in zip(c["t"], c["p"])]

    def level_a(msk, blk):
        def f():
            c["tb"] = [bf(t) for t in c["t"]]
            c["x"] = [bf(_dot(bf(_rows(t, blk)), bf(a * msk))) for t, a in zip(c["t"], c["a_ab"])]
        return f

    def level_b(blk):
        def f():
            c["t"] = [_merge_rows(t, _rows(t, blk) + _dot(x, tb), blk)
                      for t, x, tb in zip(c["t"], c["x"], c["tb"])]
        return f

    def apply_v():
        c["av"] = [bf(_dot(c["a_ak"][i], c["v2"][i])) for i in rng]

    def apply_t():
        c["wu"] = [bf(_dot(bf(c["t"][i]), jnp.concatenate([c["at"][i], c["av"][i]], axis=1)))
                   for i in rng]

    def assemble():
        zero = jnp.zeros((n, n), BF16)
        big = [_dot(jnp.concatenate([c["a_rbk"][i], c["bkh_t"][i]], axis=0),
                    jnp.concatenate([c["wu"][i], jnp.concatenate([zero, c["v2"][i]], axis=1)], axis=0))
               for i in rng]
        qeff = [c["rt"][i] + big[i][:n, :n] for i in rng]
        c["y0"] = [big[i][:n, n:] for i in rng]
        m = [eye * c["g_last"][i] + big[i][n:, :n] for i in rng]
        c["nn"] = [big[i][n:, n:] for i in rng]
        c["mq"] = [bf(jnp.concatenate([m[i], qeff[i]], axis=0)) for i in rng]

    def chain():
        hs = [h_ref[p] for p in pairs]
        ys = [[] for _ in pairs]
        for i in rng:
            q = items[i][0]
            hy = _dot(c["mq"][i], bf(hs[q]))
            hs[q] = hy[:n] + c["nn"][i]
            y2 = hy[n:] + c["y0"][i]
            ys[q].append(y2[:CHUNK, :] + y2[CHUNK:, :])
        for q, p in enumerate(pairs):
            h_ref[p] = hs[q]
        c["ys"] = [jnp.concatenate(y, axis=0) for y in ys]

    def post():
        for q, p in enumerate(pairs):
            ls = slice(p * PAIR, (p + 1) * PAIR)
            zr, k, zv, gate = c["keep"][q]
            y = c["ys"][q]
            mean = head_sum(y) * (1.0 / RWKV_HEAD)
            yc = y - mean
            var = head_sum(yc * yc) * (1.0 / RWKV_HEAD)
            yn = yc * lax.rsqrt(var + GN_EPS) * lnw_ref[:, ls] + lnb_ref[:, ls]
            bonus = head_sum(zr * k * rk_ref[:, ls]) * zv
            y_ref[:, ls] = ((yn + bonus) * gate).astype(y_ref.dtype)

    doublings = [inv_double] * (INV_BASE.bit_length() - 3)
    level_stages = [f for msk, blk in levels for f in (level_a(msk, blk), level_b(blk))]
    return ([pre, cumsum, scale, gram, inv0] + doublings + [inv_last] + level_stages
            + [apply_v, apply_t, assemble, chain, post])


def _rwkv_kernel(zr_ref, zk_ref, zv_ref, zwa_ref, zg_ref,
                 w0_ref, a0_ref, kk_ref, ka_ref, rk_ref, lnw_ref, lnb_ref,
                 wup_ref, aup_ref, gup_ref, y_ref, h_ref, *, tb, npair):
    @pl.when(pl.program_id(2) == 0)
    def _():
        h_ref[...] = jnp.zeros_like(h_ref)

    lane = lax.broadcasted_iota(jnp.int32, (tb, PAIR), 1)
    head0 = lane < RWKV_HEAD

    def head_sum(x):
        s0 = jnp.sum(jnp.where(head0, x, 0.0), axis=-1, keepdims=True)
        s1 = jnp.sum(jnp.where(head0, 0.0, x), axis=-1, keepdims=True)
        return jnp.where(head0, s0, s1)

    trow = lax.broadcasted_iota(jnp.int32, (CHUNK, CHUNK), 0)
    tcol = lax.broadcasted_iota(jnp.int32, (CHUNK, CHUNK), 1)
    tril = jnp.where(trow >= tcol, 1.0, 0.0).astype(BF16)
    consts = _rwkv_masks() + (tril,)
    refs = (zr_ref, zk_ref, zv_ref, w0_ref, a0_ref, kk_ref, ka_ref, rk_ref, lnw_ref, lnb_ref,
            wup_ref, aup_ref, gup_ref, y_ref, h_ref,
            jnp.tanh(zwa_ref[...]).astype(BF16), zwa_ref[...].astype(BF16),
            _sigmoid(zg_ref[...]).astype(BF16))
    for stage in _rwkv_wave(refs, list(range(npair)), tb // CHUNK, consts, head_sum):
        stage()


def _rwkv(rw, prm, wup, aup, gup, batch, seq):
    t_tok = rw.shape[0]
    tb = min(RWKV_TILE, seq)
    nt = seq // tb
    npair = RWKV_PAIRS
    wid = npair * PAIR
    ngrp = RWKV_DIM // wid

    def tok(base):
        return pl.BlockSpec((tb, wid), lambda b, p, t: (b * nt + t, base * ngrp + p))

    def lora(base, width):
        return pl.BlockSpec((tb, width), lambda b, p, t: (b * nt + t, base))

    prm_spec = pl.BlockSpec((1, wid), lambda b, p, t: (0, p))
    in_specs = [
        tok(0), tok(1), tok(2),
        lora((3 * RWKV_DIM + ZG_PAD) // 128, 128), lora(3 * RWKV_DIM // ZG_PAD, ZG_PAD),
    ] + [prm_spec] * 7 + [
        pl.BlockSpec((128, wid), lambda b, p, t: (0, p)),
        pl.BlockSpec((128, wid), lambda b, p, t: (0, p)),
        pl.BlockSpec((ZG_PAD, wid), lambda b, p, t: (0, p)),
    ]
    return pl.pallas_call(
        functools.partial(_rwkv_kernel, tb=tb, npair=npair),
        grid=(batch, ngrp, nt),
        in_specs=in_specs,
        out_specs=pl.BlockSpec((tb, wid), lambda b, p, t: (b * nt + t, p)),
        out_shape=jax.ShapeDtypeStruct((t_tok, RWKV_DIM), BF16),
        scratch_shapes=[pltpu.VMEM((npair, PAIR, PAIR), F32)],
        compiler_params=_cparams(("arbitrary", "arbitrary", "arbitrary")),
        name="rwkv7",
    )(rw, rw, rw, rw, rw, *prm, wup, aup, gup)


def _mla_prep_kernel(ml_ref, posrow_ref, fcol_ref, qn_ref, kvn_ref, wqt_ref, wqrt_ref, wk_ref, wvt_ref,
                     q_ref, k_ref, vt_ref, *, tk):
    ml = ml_ref[...]
    ang_t = fcol_ref[...] * posrow_ref[...].astype(F32)
    cos_t = jnp.cos(ang_t)
    sin_t = jnp.sin(ang_t)
    zlo = jnp.zeros((NOPE_DIM, tk), F32)
    zhi = jnp.zeros((HEAD_PAD - NOPE_DIM - ROPE_DIM, tk), F32)
    cosf = jnp.concatenate([zlo, cos_t, zhi], axis=0).T
    sinf = jnp.concatenate([zlo, sin_t, zhi], axis=0).T
    cq = _rms(ml[:, :Q_LORA], qn_ref[...]).astype(BF16)
    ckv = _rms(ml[:, Q_LORA:Q_LORA + KV_LORA], kvn_ref[...]).astype(BF16)
    krb = ml[:, Q_LORA + KV_LORA:Q_LORA + KV_LORA + HEAD_PAD]
    k_rope = krb * cosf + pltpu.roll(krb, HEAD_PAD - ROPE_DIM, 1) * sinf
    rope = slice(NOPE_DIM, NOPE_DIM + ROPE_DIM)
    kf = _dot(ckv, wk_ref[...])
    qf_all = _dot_nt(wqt_ref[...], cq)
    qr_all = _dot_nt(wqrt_ref[...], cq)
    for h in range(MLA_HEADS):
        hs = slice(h * HEAD_PAD, (h + 1) * HEAD_PAD)
        qf_t = qf_all[hs]
        qr_t = qr_all[h * ROPE_DIM:(h + 1) * ROPE_DIM]
        q_t = jnp.concatenate([qf_t[:NOPE_DIM], qf_t[rope] * cos_t + qr_t * sin_t,
                               qf_t[NOPE_DIM + ROPE_DIM:]], axis=0)
        q_ref[h] = q_t.astype(q_ref.dtype)
        k_ref[h] = (kf[:, hs] + k_rope).astype(k_ref.dtype)
        vt_ref[h, :V_DIM] = _dot_nt(wvt_ref[h], ckv).astype(vt_ref.dtype)
        vt_ref[h, V_DIM:] = jnp.ones((V_EXT - V_DIM, tk), vt_ref.dtype)


def _mla_prep(ml, posrow, fcol, qn, kvn, wqt, wqrt, wk, wvt, batch, seq, tk):
    nk = seq // tk
    full = lambda shape: pl.BlockSpec(shape, lambda b, t: (0,) * len(shape))
    return pl.pallas_call(
        functools.partial(_mla_prep_kernel, tk=tk),
        grid=(batch, nk),
        in_specs=[
            pl.BlockSpec((tk, ML_COLS), lambda b, t: (b * nk + t, 0)),
            pl.BlockSpec((None, 1, tk), lambda b, t: (b * nk + t, 0, 0)),
            full((ROPE_DIM, 1)), full((1, Q_LORA)), full((1, KV_LORA)),
            full((MLA_HEADS * HEAD_PAD, Q_LORA)), full((MLA_HEADS * ROPE_DIM, Q_LORA)),
            full((KV_LORA, MLA_HEADS * HEAD_PAD)), full((MLA_HEADS, V_DIM, KV_LORA)),
        ],
        out_specs=[
            pl.BlockSpec((None, MLA_HEADS, None, HEAD_PAD, tk), lambda b, t: (b, 0, t, 0, 0)),
            pl.BlockSpec((None, MLA_HEADS, None, tk, HEAD_PAD), lambda b, t: (b, 0, t, 0, 0)),
            pl.BlockSpec((None, MLA_HEADS, None, V_EXT, tk), lambda b, t: (b, 0, t, 0, 0)),
        ],
        out_shape=[
            jax.ShapeDtypeStruct((batch, MLA_HEADS, nk, HEAD_PAD, tk), BF16),
            jax.ShapeDtypeStruct((batch, MLA_HEADS, nk, tk, HEAD_PAD), BF16),
            jax.ShapeDtypeStruct((batch, MLA_HEADS, nk, V_EXT, tk), BF16),
        ],
        compiler_params=_cparams(("arbitrary", "arbitrary")),
        name="mla_prep",
    )(ml, posrow, fcol, qn, kvn, wqt, wqrt, wk, wvt)


def _attn_kernel(q_ref, k_ref, vt_ref, o_ref, m_ref, acc_ref, s_ref, p_ref, al_ref,
                 *, tk, heads, ks, ws, ahead, dlag, behind):
    i = pl.program_id(2)
    krow = lax.broadcasted_iota(jnp.int32, (ks, ws), 0)
    qcol = lax.broadcasted_iota(jnp.int32, (ks, ws), 1)
    neg = jnp.finfo(F32).min
    nsub, nstrip = tk // ks, tk // ws
    items = [(h, g) for h in range(heads) for g in range(nstrip)]
    rng = range(len(items))
    qs = [q_ref[h, 0, :, g * ws:(g + 1) * ws] for h, g in items]

    full_units = [(sub, i) for sub in range(nsub) for i in rng]
    diag_units = [(sub, i) for sub, i in full_units if not sub * ks > items[i][1] * ws + ws - 1]
    lag = dlag + behind
    tail_units = full_units[len(full_units) - lag:]
    assert diag_units[:ahead] == full_units[:ahead] and len(diag_units) > ahead + lag

    def score(j, unit):
        sub, i = unit
        return _dot(k_ref[items[i][0], j, sub * ks:(sub + 1) * ks, :], qs[i])

    def pv(j, unit, p):
        sub, i = unit
        return _dot(vt_ref[items[i][0], j, :, sub * ks:(sub + 1) * ks], p)

    def step(j, masked):
        units = diag_units if masked else full_units
        nu = len(units)
        keep = 0 if masked else lag
        jprev = jnp.maximum(j - 1, 0)
        s_val, p_val, pv_val, al_val, pv_old = {}, {}, {}, {}, {}
        for t in range(nu + (lag if masked else 0)):
            if t + ahead < nu:
                s_val[t + ahead] = score(j, units[t + ahead])
            if t < nu:
                sub, i = units[t]
                s = s_val.pop(t) if t >= ahead else s_ref[t]
                if masked and sub * ks + ks - 1 > items[i][1] * ws:
                    s = jnp.where(krow + (sub * ks - items[i][1] * ws) <= qcol, s, neg)
                m_old = m_ref[i]
                m_new = jnp.maximum(m_old, jnp.max(s, axis=0, keepdims=True))
                al_val[t] = jnp.exp2(m_old - m_new)
                m_ref[i] = m_new
                p_val[t] = jnp.exp2(s - m_new).astype(BF16)
                if not masked and t + ahead >= nu:
                    s_ref[t + ahead - nu] = score(j + 1, full_units[t + ahead - nu])
            if t < lag:
                pv_old[t] = pv(jprev, tail_units[t], p_ref[t])
            d = t - dlag
            if 0 <= d < nu - keep:
                pv_val[d] = pv(j, units[d], p_val.pop(d))
            x = t - behind
            if 0 <= x < lag:
                i = tail_units[x][1]
                acc_ref[i] = al_ref[x] * acc_ref[i] + pv_old.pop(x)
            w = d - behind
            if 0 <= w < nu - keep:
                i = units[w][1]
                acc_ref[i] = al_val.pop(w) * acc_ref[i] + pv_val.pop(w)
        if not masked:
            for x in range(lag):
                p_ref[x] = p_val[nu - lag + x]
                al_ref[x] = al_val[nu - lag + x]

    m_ref[...] = jnp.full(m_ref.shape, neg, F32)
    acc_ref[...] = jnp.zeros(acc_ref.shape, F32)
    p_ref[...] = jnp.zeros(p_ref.shape, BF16)
    al_ref[...] = jnp.ones(al_ref.shape, F32)
    for t in range(ahead):
        s_ref[t] = score(0, full_units[t])

    def body(j, c):
        step(j, False)
        return c

    lax.fori_loop(0, i, body, 0)
    step(i, True)
    outs = []
    for h in range(heads):
        accs = [acc_ref[h * nstrip + g] for g in range(nstrip)]
        acc = jnp.concatenate([a[:V_DIM] / a[V_DIM:V_DIM + 1] for a in accs], axis=1)
        outs.append(acc.T)
    o_ref[...] = jnp.concatenate(outs, axis=-1).astype(o_ref.dtype)


def _attn(q5, k5, vt5, batch, seq, tk):
    nk = seq // tk
    heads = 8
    ngrp = MLA_HEADS // heads
    ks, ws = min(256, tk), min(256, tk)
    ahead, dlag, behind = 3, 2, 2
    nitem = heads * (tk // ws)
    return pl.pallas_call(
        functools.partial(_attn_kernel, tk=tk, heads=heads, ks=ks, ws=ws, ahead=ahead, dlag=dlag, behind=behind),
        grid=(batch, ngrp, nk),
        in_specs=[
            pl.BlockSpec((None, heads, 1, HEAD_PAD, tk), lambda b, g, i: (b, g, i, 0, 0)),
            pl.BlockSpec((None, heads, nk, tk, HEAD_PAD), lambda b, g, i: (b, g, 0, 0, 0)),
            pl.BlockSpec((None, heads, nk, V_EXT, tk), lambda b, g, i: (b, g, 0, 0, 0)),
        ],
        out_specs=pl.BlockSpec((tk, heads * V_DIM), lambda b, g, i: (b * nk + i, g)),
        out_shape=jax.ShapeDtypeStruct((batch * seq, MLA_HEADS * V_DIM), BF16),
        scratch_shapes=[
            pltpu.VMEM((nitem, 1, ws), F32),
            pltpu.VMEM((nitem, V_EXT, ws), F32),
            pltpu.VMEM((ahead, ks, ws), F32),
            pltpu.VMEM((dlag + behind, ks, ws), BF16),
            pltpu.VMEM((dlag + behind, 1, ws), F32),
        ],
        compiler_params=_cparams(("arbitrary", "arbitrary", "arbitrary")),
        name="mla_attn",
    )(q5, k5, vt5)


def _tail_kernel(x_ref, yr_ref, o_ref, gt_ref, wor_ref, wom_ref, wout_ref, nf_ref, wup_ref, wdn_ref,
                 nfin_ref, out_ref):
    y_a = _dot(yr_ref[...], wor_ref[...])
    y_b = _dot(o_ref[...], wom_ref[...])
    merged = _sigmoid(gt_ref[:, :D_MODEL]) * y_a + _sigmoid(gt_ref[:, D_MODEL:]) * y_b
    h = x_ref[...] + _dot(merged.astype(BF16), wout_ref[...])
    f_in = _rms(h, nf_ref[...]).astype(BF16)
    cw = 1024
    acc = h
    for c in range(D_FF // cw):
        f = _dot(f_in, wup_ref[:, c * cw:(c + 1) * cw])
        f = jnp.square(jnp.maximum(f, 0.0)).astype(BF16)
        acc = acc + _dot(f, wdn_ref[c * cw:(c + 1) * cw, :])
    out_ref[...] = _rms(acc, nfin_ref[...])


def _tail(x2, yr, o, gt, wor, wom, wout, nf, wup, wdn, nfin, seq):
    t_tok = x2.shape[0]
    tm = min(TOKEN_TILE, seq)
    full = lambda shape: pl.BlockSpec(shape, lambda i: (0,) * len(shape), pipeline_mode=pl.Buffered(1))
    return pl.pallas_call(
        _tail_kernel,
        grid=(t_tok // tm,),
        in_specs=[
            pl.BlockSpec((tm, D_MODEL), lambda i: (i, 0)),
            pl.BlockSpec((tm, RWKV_DIM), lambda i: (i, 0)),
            pl.BlockSpec((tm, MLA_HEADS * V_DIM), lambda i: (i, 0)),
            pl.BlockSpec((tm, GT_COLS), lambda i: (i, 0)),
            full((RWKV_DIM, D_MODEL)), full((MLA_HEADS * V_DIM, D_MODEL)), full((D_MODEL, D_MODEL)),
            full((1, D_MODEL)), full((D_MODEL, D_FF)), full((D_FF, D_MODEL)), full((1, D_MODEL)),
        ],
        out_specs=pl.BlockSpec((tm, D_MODEL), lambda i: (i, 0)),
        out_shape=jax.ShapeDtypeStruct((t_tok, D_MODEL), F32),
        compiler_params=_cparams(("arbitrary",)),
        name="merge_ffn",
    )(x2, yr, o, gt, wor, wom, wout, nf, wup, wdn, nfin)


def _padc(w, n):
    return jnp.pad(w, ((0, 0), (0, n - w.shape[1])))


def _padr(w, n):
    return jnp.pad(w, ((0, n - w.shape[0]), (0, 0)))


def _rot_half(w):
    half = w.shape[-1] // 2
    return jnp.concatenate([-w[..., half:], w[..., :half]], axis=-1)


def kernel(x, positions, norm_mix, w_in, mu_shift, w0, w_up, a0, a_up, g_up, k_k, k_a, r_k, ln_w, ln_b, w_o_rwkv, q_norm, w_uq, kv_norm, w_ukv, w_o_mla, w_out, norm_ffn, w_ff_up, w_ff_down, norm_final):
    batch, seq, _ = x.shape
    t_tok = batch * seq
    x2 = x.reshape(t_tok, D_MODEL)
    l = 0

    wi = w_in[l].astype(BF16)
    o = 0
    w_r3 = wi[:, o:o + 3 * RWKV_DIM]; o += 3 * RWKV_DIM
    w_zw = wi[:, o:o + DECAY_LORA]; o += DECAY_LORA
    w_za = wi[:, o:o + ICLR_LORA]; o += ICLR_LORA
    w_zg = wi[:, o:o + GATE_LORA]; o += GATE_LORA
    w_cq = wi[:, o:o + Q_LORA]; o += Q_LORA
    w_ckv = wi[:, o:o + KV_LORA]; o += KV_LORA
    w_kr = wi[:, o:o + ROPE_DIM]; o += ROPE_DIM
    w_gate = wi[:, o:o + 2 * D_MODEL]
    zeros64 = jnp.zeros((D_MODEL, NOPE_DIM), wi.dtype)
    w_krb = jnp.concatenate([zeros64, w_kr, _rot_half(w_kr)], axis=1)
    w_a = jnp.concatenate([
        w_r3, _padc(w_zg, ZG_PAD), w_zw, w_za, w_cq, w_ckv, w_krb, w_gate], axis=1)
    mu = mu_shift[l]
    o = 3 * RWKV_DIM
    lo = o + DECAY_LORA + ICLR_LORA
    mu_a = jnp.concatenate([mu[:o], jnp.pad(mu[lo:], (0, ZG_PAD - GATE_LORA)), mu[o:lo]])[None, :]

    rw, ml, gt = _inproj(x2, norm_mix[l][None, :], w_a, mu_a, seq)

    prm = [p[None, :] for p in (w0[l], a0[l], k_k[l], k_a[l], r_k[l].reshape(-1), ln_w[l], ln_b[l])]
    aup = jnp.concatenate([jnp.zeros_like(a_up[l]), a_up[l]], axis=0)
    yr = _rwkv(rw, prm, _padr(w_up[l], 128).astype(BF16), aup.astype(BF16),
               _padr(g_up[l], ZG_PAD).astype(BF16), batch, seq)

    half = ROPE_DIM // 2
    inv_freq = 1.0 / (ROPE_THETA ** (jnp.arange(half, dtype=F32) * (2.0 / ROPE_DIM)))
    scale = (NOPE_DIM + ROPE_DIM) ** -0.5 * math.log2(math.e)
    wq = w_uq[l].reshape(Q_LORA, MLA_HEADS, NOPE_DIM + ROPE_DIM) * scale
    wq_p = jnp.pad(wq, ((0, 0), (0, 0), (0, HEAD_PAD - NOPE_DIM - ROPE_DIM)))
    wqt = jnp.transpose(wq_p, (1, 2, 0)).reshape(MLA_HEADS * HEAD_PAD, Q_LORA)
    wqrt = jnp.transpose(_rot_half(wq[..., NOPE_DIM:]), (1, 2, 0)).reshape(MLA_HEADS * ROPE_DIM, Q_LORA)
    fcol = jnp.concatenate([inv_freq, inv_freq])[:, None]
    wkv = w_ukv[l].reshape(KV_LORA, MLA_HEADS, NOPE_DIM + V_DIM)
    wk_p = jnp.pad(wkv[..., :NOPE_DIM], ((0, 0), (0, 0), (0, HEAD_PAD - NOPE_DIM)))
    wvt = jnp.transpose(wkv[..., NOPE_DIM:], (1, 2, 0))
    tk = min(TOKEN_TILE, seq)
    q5, k5, vt5 = _mla_prep(
        ml, positions.reshape(t_tok // tk, 1, tk), fcol,
        q_norm[l][None, :], kv_norm[l][None, :], wqt.astype(BF16), wqrt.astype(BF16),
        wk_p.reshape(KV_LORA, -1).astype(BF16), wvt.astype(BF16), batch, seq, tk)
    o_att = _attn(q5, k5, vt5, batch, seq, tk)

    out = _tail(x2, yr, o_att, gt, w_o_rwkv[l].astype(BF16), w_o_mla[l].astype(BF16),
                w_out[l].astype(BF16), norm_ffn[l][None, :], w_ff_up[l].astype(BF16),
                w_ff_down[l].astype(BF16), norm_final[None, :], seq)
    return out.reshape(batch, seq, D_MODEL)
```

```python
import functools
import math

import jax
import jax.numpy as jnp
from jax import lax
from jax.experimental import pallas as pl
from jax.experimental.pallas import tpu as pltpu

F32 = jnp.float32
BF16 = jnp.bfloat16

D_MODEL = 1024
NORM_EPS = 1e-6
RWKV_HEAD = 64
RWKV_HEADS = 8
RWKV_DIM = RWKV_HEADS * RWKV_HEAD
DECAY_LORA = 64
ICLR_LORA = 64
GATE_LORA = 160
GN_EPS = 64e-5
MLA_HEADS = 8
Q_LORA = 256
KV_LORA = 128
NOPE_DIM = 64
ROPE_DIM = 32
V_DIM = 64
ROPE_THETA = 10000.0
D_FF = 4 * D_MODEL

LANE = 128
CHUNK = 64
INV_BASE = 8
PAIR = 2 * RWKV_HEAD
HEAD_PAD = 128
V_EXT = V_DIM + 16

ZG_PAD = 256
RW_COLS = 3 * RWKV_DIM + ZG_PAD + 128
ML_COLS = Q_LORA + KV_LORA + 128
GT_COLS = 2 * D_MODEL

VMEM_BYTES_V7X = 64 * 1024 * 1024
VMEM_LIMIT = VMEM_BYTES_V7X * 7 // 8
TOKEN_TILE = 512
RWKV_TILE = 8 * CHUNK
RWKV_PAIRS = RWKV_DIM // PAIR


def _cparams(sem):
    return pltpu.CompilerParams(dimension_semantics=sem, vmem_limit_bytes=VMEM_LIMIT)


def _sigmoid(x):
    return 1.0 / (1.0 + jnp.exp(-x))


def _rms(x, g):
    ms = jnp.mean(x * x, axis=-1, keepdims=True)
    return x * lax.rsqrt(ms + NORM_EPS) * g


def _dot(a, b):
    return jnp.dot(a, b, preferred_element_type=F32)


def _dot_nt(a, b):
    return lax.dot_general(a, b, (((1,), (1,)), ((), ())), preferred_element_type=F32)


def _dot_tn(a, b):
    return lax.dot_general(a, b, (((0,), (0,)), ((), ())), preferred_element_type=F32)


def _mla_prep_parts(ml, posrow_ref, fcol_ref, qn_ref, kvn_ref, wqt_ref, wqrt_ref, wk_ref, wvt_ref,
                    q_ref, k_ref, vt_ref, tk):
    c = {}
    rope = slice(NOPE_DIM, NOPE_DIM + ROPE_DIM)

    def trig():
        ang_t = fcol_ref[...] * posrow_ref[...].astype(F32)
        c["cos_t"] = jnp.cos(ang_t)
        c["sin_t"] = jnp.sin(ang_t)
        c["cq"] = _rms(ml[:, :Q_LORA], qn_ref[...]).astype(BF16)
        c["ckv"] = _rms(ml[:, Q_LORA:Q_LORA + KV_LORA], kvn_ref[...]).astype(BF16)

    def project():
        c["qf"] = _dot_nt(wqt_ref[...], c["cq"])
        c["qr"] = _dot_nt(wqrt_ref[...], c["cq"])
        c["kf"] = _dot(c["ckv"], wk_ref[...])

    def k_rope():
        zlo = jnp.zeros((NOPE_DIM, tk), F32)
        zhi = jnp.zeros((HEAD_PAD - NOPE_DIM - ROPE_DIM, tk), F32)
        cosf = jnp.concatenate([zlo, c["cos_t"], zhi], axis=0).T
        sinf = jnp.concatenate([zlo, c["sin_t"], zhi], axis=0).T
        krb = ml[:, Q_LORA + KV_LORA:Q_LORA + KV_LORA + HEAD_PAD]
        c["k_rope"] = krb * cosf + pltpu.roll(krb, HEAD_PAD - ROPE_DIM, 1) * sinf

    def heads(lo, hi):
        def f():
            for h in range(lo, hi):
                hs = slice(h * HEAD_PAD, (h + 1) * HEAD_PAD)
                qf_t = c["qf"][hs]
                qr_t = c["qr"][h * ROPE_DIM:(h + 1) * ROPE_DIM]
                q_t = jnp.concatenate([qf_t[:NOPE_DIM], qf_t[rope] * c["cos_t"] + qr_t * c["sin_t"],
                                       qf_t[NOPE_DIM + ROPE_DIM:]], axis=0)
                q_ref[h] = q_t.astype(q_ref.dtype)
                k_ref[h] = (c["kf"][:, hs] + c["k_rope"]).astype(k_ref.dtype)
                vt_ref[h, :V_DIM] = _dot_nt(wvt_ref[h], c["ckv"]).astype(vt_ref.dtype)
                vt_ref[h, V_DIM:] = jnp.ones((V_EXT - V_DIM, tk), vt_ref.dtype)
        return f

    half = MLA_HEADS // 2
    return [trig, project, k_rope, heads(0, half), heads(half, MLA_HEADS)]


def _inproj_kernel(x_ref, g_ref, w_ref, mu_ref, posrow_ref, fcol_ref, qn_ref, kvn_ref,
                   wqt_ref, wqrt_ref, wk_ref, wvt_ref,
                   rw_ref, gt_ref, q_ref, k_ref, vt_ref, carry_ref, *, tiles_per_seq, tm):
    i = pl.program_id(0)
    u = _rms(x_ref[...], g_ref[...]).astype(BF16)

    @pl.when(i % tiles_per_seq == 0)
    def _():
        carry_ref[...] = jnp.zeros_like(carry_ref)

    def shift_store(cs, z):
        prev = pltpu.roll(z, 1, 0)
        row0 = lax.broadcasted_iota(jnp.int32, z.shape, 0) == 0
        prev = jnp.where(row0, carry_ref[7:8, cs], prev)
        carry_ref[:, cs] = z[tm - 8:tm, :]
        rw_ref[:, cs] = z + (prev - z) * mu_ref[:, cs]

    def gate_store(cs, z):
        gt_ref[:, cs] = z

    cw = 512
    jobs = [(c0, slice(c0, min(c0 + cw, RW_COLS)), shift_store) for c0 in range(0, RW_COLS, cw)]
    jobs += [(RW_COLS + ML_COLS + c0, slice(c0, c0 + cw), gate_store) for c0 in range(0, GT_COLS, cw)]
    ml = _dot(u, w_ref[:, RW_COLS:RW_COLS + ML_COLS])
    fillers = _mla_prep_parts(ml, posrow_ref, fcol_ref, qn_ref, kvn_ref, wqt_ref, wqrt_ref, wk_ref, wvt_ref,
                              q_ref, k_ref, vt_ref, tm)
    mm = lambda job: _dot(u, w_ref[:, job[0]:job[0] + (job[1].stop - job[1].start)])
    z = mm(jobs[0])
    for n, job in enumerate(jobs):
        z_next = mm(jobs[n + 1]) if n + 1 < len(jobs) else None
        if n < len(fillers):
            fillers[n]()
        job[2](job[1], z)
        z = z_next


def _inproj(x2, g, w_a, mu_a, posrow, fcol, qn, kvn, wqt, wqrt, wk, wvt, batch, seq):
    t_tok = x2.shape[0]
    tm = min(TOKEN_TILE, seq)
    nk = seq // tm
    ncol = RW_COLS + ML_COLS + GT_COLS
    kern = functools.partial(_inproj_kernel, tiles_per_seq=nk, tm=tm)
    full = lambda shape: pl.BlockSpec(shape, lambda i: (0,) * len(shape))
    head_major = lambda rows, cols: pl.BlockSpec((None, MLA_HEADS, None, rows, cols),
                                                 lambda i: (i // nk, 0, i % nk, 0, 0))
    return pl.pallas_call(
        kern,
        grid=(t_tok // tm,),
        in_specs=[
            pl.BlockSpec((tm, D_MODEL), lambda i: (i, 0)),
            full((1, D_MODEL)),
            pl.BlockSpec((D_MODEL, ncol), lambda i: (0, 0), pipeline_mode=pl.Buffered(1)),
            full((1, RW_COLS)),
            pl.BlockSpec((None, 1, tm), lambda i: (i, 0, 0)),
            full((ROPE_DIM, 1)), full((1, Q_LORA)), full((1, KV_LORA)),
            full((MLA_HEADS * HEAD_PAD, Q_LORA)), full((MLA_HEADS * ROPE_DIM, Q_LORA)),
            full((KV_LORA, MLA_HEADS * HEAD_PAD)), full((MLA_HEADS, V_DIM, KV_LORA)),
        ],
        out_specs=[
            pl.BlockSpec((tm, RW_COLS), lambda i: (i, 0)),
            pl.BlockSpec((tm, GT_COLS), lambda i: (i, 0)),
            head_major(HEAD_PAD, tm), head_major(tm, HEAD_PAD), head_major(V_EXT, tm),
        ],
        out_shape=[
            jax.ShapeDtypeStruct((t_tok, RW_COLS), F32),
            jax.ShapeDtypeStruct((t_tok, GT_COLS), F32),
            jax.ShapeDtypeStruct((batch, MLA_HEADS, nk, HEAD_PAD, tm), BF16),
            jax.ShapeDtypeStruct((batch, MLA_HEADS, nk, tm, HEAD_PAD), BF16),
            jax.ShapeDtypeStruct((batch, MLA_HEADS, nk, V_EXT, tm), BF16),
        ],
        scratch_shapes=[pltpu.VMEM((8, RW_COLS), F32)],
        compiler_params=_cparams(("arbitrary",)),
        name="inproj",
    )(x2, g, w_a, mu_a, posrow, fcol, qn, kvn, wqt, wqrt, wk, wvt)


def _split3(x):
    hi = x.astype(BF16)
    r1 = x - hi.astype(F32)
    mid = r1.astype(BF16)
    lo = (r1 - mid.astype(F32)).astype(BF16)
    return hi, mid, lo


def _rwkv_masks():
    n = 2 * CHUNK
    row = lax.broadcasted_iota(jnp.int32, (n, n), 0)
    col = lax.broadcasted_iota(jnp.int32, (n, n), 1)
    same = (row // CHUNK) == (col // CHUNK)
    strict = jnp.where(same & (row > col), 1.0, 0.0).astype(F32)
    incl = jnp.where(same & (row >= col), 1.0, 0.0).astype(F32)
    base = jnp.where((row // INV_BASE) == (col // INV_BASE), 1.0, 0.0).astype(F32)
    levels = []
    blk = INV_BASE
    while blk < CHUNK:
        levels.append((jnp.where(((row // (2 * blk)) == (col // (2 * blk))) & ((row // blk) > (col // blk)),
                                 1.0, 0.0).astype(F32), blk))
        blk *= 2
    eye = jnp.where(row == col, 1.0, 0.0).astype(F32)
    headsel = (row // CHUNK) == (col // RWKV_HEAD)
    return strict, incl, base, levels, eye, headsel


def _rows(x, blk):
    return jnp.concatenate([x[s:s + blk] for s in range(blk, x.shape[0], 2 * blk)], axis=0)


def _merge_rows(x, odd, blk):
    parts = []
    for j, s in enumerate(range(0, x.shape[0], 2 * blk)):
        parts += [x[s:s + blk], odd[j * blk:(j + 1) * blk]]
    return jnp.concatenate(parts, axis=0)


def _rwkv_wave(refs, pairs, nchunk, consts, head_sum):
    (zr_ref, zk_ref, zv_ref, w0_ref, a0_ref, kk_ref, ka_ref, rk_ref, lnw_ref, lnb_ref,
     wup_ref, aup_ref, gup_ref, y_ref, h_ref, tanh_zw, za, sig_zg) = refs
    strict, incl, base, levels, eye, headsel, tril = consts
    incl2 = jnp.concatenate([incl, incl], axis=1)
    n = 2 * CHUNK
    bf = lambda x: x.astype(BF16)
    items = [(q, slice(c * CHUNK, (c + 1) * CHUNK)) for c in range(nchunk) for q in range(len(pairs))]
    rng = range(len(items))
    c = {}

    def stack(x):
        return jnp.where(headsel, jnp.concatenate([x, x], axis=0), 0.0)

    def pre():
        c["keep"] = []
        streams = []
        for p in pairs:
            ls = slice(p * PAIR, (p + 1) * PAIR)
            zr, zk, zv = zr_ref[:, ls], zk_ref[:, ls], zv_ref[:, ls]
            w_pre = w0_ref[:, ls] + _dot(tanh_zw, wup_ref[:, ls])
            lw = -math.exp(-0.5) * _sigmoid(w_pre)
            iclr = _sigmoid(a0_ref[:, ls] + _dot(za, aup_ref[:, ls]))
            gate = _dot(sig_zg, gup_ref[:, ls])
            kk = zk * kk_ref[:, ls]
            kk = kk / jnp.maximum(jnp.sqrt(head_sum(kk * kk)), 1e-12)
            k = zk * (1.0 + (iclr - 1.0) * ka_ref[:, ls])
            streams.append((zr, k, zv, -kk, kk * iclr, lw))
            c["keep"].append((zr, k, zv, gate))
        for j, name in enumerate(("r", "k", "v", "a", "b", "lw")):
            c[name] = [streams[q][j][s] for q, s in items]

    def cumsum():
        c["cs"] = []
        for i in rng:
            hi, mid, lo = _split3(c["lw"][i])
            c["cs"].append(_dot(tril, hi) + _dot(tril, mid) + _dot(tril, lo))

    def scale():
        cs, lw = c["cs"], c["lw"]
        g_in = [jnp.exp(cs[i]) for i in rng]
        g_ex = [jnp.exp(cs[i] - lw[i]) for i in rng]
        g_inv = [jnp.exp(-cs[i]) for i in rng]
        c["g_last"] = [jnp.exp(cs[i][CHUNK - 1:CHUNK, :]) for i in rng]
        c["rt"] = [stack(c["r"][i] * g_in[i]) for i in rng]
        c["at"] = [bf(stack(c["a"][i] * g_ex[i])) for i in rng]
        bt = [stack(c["b"][i] * g_inv[i]) for i in rng]
        kt = [stack(c["k"][i] * g_inv[i]) for i in rng]
        c["bkh_t"] = [bf(jnp.concatenate([(bt[i] * c["g_last"][i]).T, (kt[i] * c["g_last"][i]).T], axis=1))
                      for i in rng]
        c["v2"] = [bf(stack(c["v"][i])) for i in rng]
        c["bk"] = [bf(jnp.concatenate([bt[i], kt[i]], axis=0)) for i in rng]

    def gram():
        ga = [_dot_nt(c["at"][i], c["bk"][i]) for i in rng]
        gr = [_dot_nt(bf(c["rt"][i]), c["bk"][i]) for i in rng]
        c["a_ab"] = [g[:, :n] * strict for g in ga]
        c["a_ak"] = [bf(g[:, n:] * strict) for g in ga]
        c["a_rbk"] = [bf(g * incl2) for g in gr]

    def inv0():
        c["d"] = [bf(a * base) for a in c["a_ab"]]
        c["t"] = [eye + a * base for a in c["a_ab"]]
        c["p"] = [_dot(d, d) for d in c["d"]]

    def inv_double():
        xs = [_dot(bf(jnp.concatenate([t, p], axis=0)), bf(p)) for t, p in zip(c["t"], c["p"])]
        c["t"] = [t + x[:n] for t, x in zip(c["t"], xs)]
        c["p"] = [x[n:] for x in xs]

    def inv_last():
        c["t"] = [t + _dot(bf(t), bf(p)) for t, p in zip(c["t"], c["p"])]

    def level_a(msk, blk):
        def f():
            c["tb"] = [bf(t) for t in c["t"]]
            c["x"] = [bf(_dot(bf(_rows(t, blk)), bf(a * msk))) for t, a in zip(c["t"], c["a_ab"])]
        return f

    def level_b(blk):
        def f():
            c["t"] = [_merge_rows(t, _rows(t, blk) + _dot(x, tb), blk)
                      for t, x, tb in zip(c["t"], c["x"], c["tb"])]
        return f

    def apply_v():
        c["av"] = [bf(_dot(c["a_ak"][i], c["v2"][i])) for i in rng]

    def apply_t():
        c["wu"] = [bf(_dot(bf(c["t"][i]), jnp.concatenate([c["at"][i], c["av"][i]], axis=1)))
                   for i in rng]

    def assemble():
        zero = jnp.zeros((n, n), BF16)
        big = [_dot(jnp.concatenate([c["a_rbk"][i], c["bkh_t"][i]], axis=0),
                    jnp.concatenate([c["wu"][i], jnp.concatenate([zero, c["v2"][i]], axis=1)], axis=0))
               for i in rng]
        qeff = [c["rt"][i] + big[i][:n, :n] for i in rng]
        c["y0"] = [big[i][:n, n:] for i in rng]
        m = [eye * c["g_last"][i] + big[i][n:, :n] for i in rng]
        c["nn"] = [big[i][n:, n:] for i in rng]
        c["mq"] = [bf(jnp.concatenate([m[i], qeff[i]], axis=0)) for i in rng]

    def chain():
        hs = [h_ref[p] for p in pairs]
        ys = [[] for _ in pairs]
        for i in rng:
            q = items[i][0]
            hy = _dot(c["mq"][i], bf(hs[q]))
            hs[q] = hy[:n] + c["nn"][i]
            y2 = hy[n:] + c["y0"][i]
            ys[q].append(y2[:CHUNK, :] + y2[CHUNK:, :])
        for q, p in enumerate(pairs):
            h_ref[p] = hs[q]
        c["ys"] = [jnp.concatenate(y, axis=0) for y in ys]

    def post():
        for q, p in enumerate(pairs):
            ls = slice(p * PAIR, (p + 1) * PAIR)
            zr, k, zv, gate = c["keep"][q]
            y = c["ys"][q]
            mean = head_sum(y) * (1.0 / RWKV_HEAD)
            yc = y - mean
            var = head_sum(yc * yc) * (1.0 / RWKV_HEAD)
            yn = yc * lax.rsqrt(var + GN_EPS) * lnw_ref[:, ls] + lnb_ref[:, ls]
            bonus = head_sum(zr * k * rk_ref[:, ls]) * zv
            y_ref[:, ls] = ((yn + bonus) * gate).astype(y_ref.dtype)

    doublings = [inv_double] * (INV_BASE.bit_length() - 3)
    level_stages = [f for msk, blk in levels for f in (level_a(msk, blk), level_b(blk))]
    return ([pre, cumsum, scale, gram, inv0] + doublings + [inv_last] + level_stages
            + [apply_v, apply_t, assemble, chain, post])


def _rwkv_kernel(zr_ref, zk_ref, zv_ref, zwa_ref, zg_ref,
                 w0_ref, a0_ref, kk_ref, ka_ref, rk_ref, lnw_ref, lnb_ref,
                 wup_ref, aup_ref, gup_ref, y_ref, h_ref, *, tb, npair):
    @pl.when(pl.program_id(2) == 0)
    def _():
        h_ref[...] = jnp.zeros_like(h_ref)

    lane = lax.broadcasted_iota(jnp.int32, (tb, PAIR), 1)
    head0 = lane < RWKV_HEAD

    def head_sum(x):
        s0 = jnp.sum(jnp.where(head0, x, 0.0), axis=-1, keepdims=True)
        s1 = jnp.sum(jnp.where(head0, 0.0, x), axis=-1, keepdims=True)
        return jnp.where(head0, s0, s1)

    trow = lax.broadcasted_iota(jnp.int32, (CHUNK, CHUNK), 0)
    tcol = lax.broadcasted_iota(jnp.int32, (CHUNK, CHUNK), 1)
    tril = jnp.where(trow >= tcol, 1.0, 0.0).astype(BF16)
    consts = _rwkv_masks() + (tril,)
    refs = (zr_ref, zk_ref, zv_ref, w0_ref, a0_ref, kk_ref, ka_ref, rk_ref, lnw_ref, lnb_ref,
            wup_ref, aup_ref, gup_ref, y_ref, h_ref,
            jnp.tanh(zwa_ref[...]).astype(BF16), zwa_ref[...].astype(BF16),
            _sigmoid(zg_ref[...]).astype(BF16))
    for stage in _rwkv_wave(refs, list(range(npair)), tb // CHUNK, consts, head_sum):
        stage()


def _rwkv(rw, prm, wup, aup, gup, batch, seq):
    t_tok = rw.shape[0]
    tb = min(RWKV_TILE, seq)
    nt = seq // tb
    npair = RWKV_PAIRS
    wid = npair * PAIR
    ngrp = RWKV_DIM // wid

    def tok(base):
        return pl.BlockSpec((tb, wid), lambda b, p, t: (b * nt + t, base * ngrp + p))

    def lora(base, width):
        return pl.BlockSpec((tb, width), lambda b, p, t: (b * nt + t, base))

    prm_spec = pl.BlockSpec((1, wid), lambda b, p, t: (0, p))
    in_specs = [
        tok(0), tok(1), tok(2),
        lora((3 * RWKV_DIM + ZG_PAD) // 128, 128), lora(3 * RWKV_DIM // ZG_PAD, ZG_PAD),
    ] + [prm_spec] * 7 + [
        pl.BlockSpec((128, wid), lambda b, p, t: (0, p)),
        pl.BlockSpec((128, wid), lambda b, p, t: (0, p)),
        pl.BlockSpec((ZG_PAD, wid), lambda b, p, t: (0, p)),
    ]
    return pl.pallas_call(
        functools.partial(_rwkv_kernel, tb=tb, npair=npair),
        grid=(batch, ngrp, nt),
        in_specs=in_specs,
        out_specs=pl.BlockSpec((tb, wid), lambda b, p, t: (b * nt + t, p)),
        out_shape=jax.ShapeDtypeStruct((t_tok, RWKV_DIM), BF16),
        scratch_shapes=[pltpu.VMEM((npair, PAIR, PAIR), F32)],
        compiler_params=_cparams(("arbitrary", "arbitrary", "arbitrary")),
        name="rwkv7",
    )(rw, rw, rw, rw, rw, *prm, wup, aup, gup)


def _attn_kernel(q_ref, k_ref, vt_ref, o_ref, m_ref, acc_ref, s_ref, p_ref, al_ref,
                 *, tk, heads, ks, ws, ahead, dlag, behind):
    i = pl.program_id(2)
    krow = lax.broadcasted_iota(jnp.int32, (ks, ws), 0)
    qcol = lax.broadcasted_iota(jnp.int32, (ks, ws), 1)
    neg = jnp.finfo(F32).min
    nsub, nstrip = tk // ks, tk // ws
    items = [(h, g) for h in range(heads) for g in range(nstrip)]
    rng = range(len(items))
    qs = [q_ref[h, 0, :, g * ws:(g + 1) * ws] for h, g in items]

    full_units = [(sub, i) for sub in range(nsub) for i in rng]
    diag_units = [(sub, i) for sub, i in full_units if not sub * ks > items[i][1] * ws + ws - 1]
    lag = dlag + behind
    tail_units = full_units[len(full_units) - lag:]
    assert diag_units[:ahead] == full_units[:ahead] and len(diag_units) > ahead + lag

    def score(j, unit):
        sub, i = unit
        return _dot(k_ref[items[i][0], j, sub * ks:(sub + 1) * ks, :], qs[i])

    def pv(j, unit, p):
        sub, i = unit
        return _dot(vt_ref[items[i][0], j, :, sub * ks:(sub + 1) * ks], p)

    def step(j, masked):
        units = diag_units if masked else full_units
        nu = len(units)
        keep = 0 if masked else lag
        jprev = jnp.maximum(j - 1, 0)
        s_val, p_val, pv_val, al_val, pv_old = {}, {}, {}, {}, {}
        for t in range(nu + (lag if masked else 0)):
            if t + ahead < nu:
                s_val[t + ahead] = score(j, units[t + ahead])
            if t < nu:
                sub, i = units[t]
                s = s_val.pop(t) if t >= ahead else s_ref[t]
                if masked and sub * ks + ks - 1 > items[i][1] * ws:
                    s = jnp.where(krow + (sub * ks - items[i][1] * ws) <= qcol, s, neg)
                m_old = m_ref[i]
                m_new = jnp.maximum(m_old, jnp.max(s, axis=0, keepdims=True))
                al_val[t] = jnp.exp2(m_old - m_new)
                m_ref[i] = m_new
                p_val[t] = jnp.exp2(s - m_new).astype(BF16)
                if not masked and t + ahead >= nu:
                    s_ref[t + ahead - nu] = score(j + 1, full_units[t + ahead - nu])
            if t < lag:
                pv_old[t] = pv(jprev, tail_units[t], p_ref[t])
            d = t - dlag
            if 0 <= d < nu - keep:
                pv_val[d] = pv(j, units[d], p_val.pop(d))
            x = t - behind
            if 0 <= x < lag:
                i = tail_units[x][1]
                acc_ref[i] = al_ref[x] * acc_ref[i] + pv_old.pop(x)
            w = d - behind
            if 0 <= w < nu - keep:
                i = units[w][1]
                acc_ref[i] = al_val.pop(w) * acc_ref[i] + pv_val.pop(w)
        if not masked:
            for x in range(lag):
                p_ref[x] = p_val[nu - lag + x]
                al_ref[x] = al_val[nu - lag + x]

    m_ref[...] = jnp.full(m_ref.shape, neg, F32)
    acc_ref[...] = jnp.zeros(acc_ref.shape, F32)
    p_ref[...] = jnp.zeros(p_ref.shape, BF16)
    al_ref[...] = jnp.ones(al_ref.shape, F32)
    for t in range(ahead):
        s_ref[t] = score(0, full_units[t])

    def body(j, c):
        step(j, False)
        return c

    lax.fori_loop(0, i, body, 0)
    step(i, True)
    outs = []
    for h in range(heads):
        accs = [acc_ref[h * nstrip + g] for g in range(nstrip)]
        acc = jnp.concatenate([a[:V_DIM] / a[V_DIM:V_DIM + 1] for a in accs], axis=1)
        outs.append(acc.T)
    o_ref[...] = jnp.concatenate(outs, axis=-1).astype(o_ref.dtype)


def _attn(q5, k5, vt5, batch, seq, tk):
    nk = seq // tk
    heads = 8
    ngrp = MLA_HEADS // heads
    ks, ws = min(256, tk), min(256, tk)
    ahead, dlag, behind = 3, 2, 2
    nitem = heads * (tk // ws)
    return pl.pallas_call(
        functools.partial(_attn_kernel, tk=tk, heads=heads, ks=ks, ws=ws, ahead=ahead, dlag=dlag, behind=behind),
        grid=(batch, ngrp, nk),
        in_specs=[
            pl.BlockSpec((None, heads, 1, HEAD_PAD, tk), lambda b, g, i: (b, g, i, 0, 0)),
            pl.BlockSpec((None, heads, nk, tk, HEAD_PAD), lambda b, g, i: (b, g, 0, 0, 0)),
            pl.BlockSpec((None, heads, nk, V_EXT, tk), lambda b, g, i: (b, g, 0, 0, 0)),
        ],
        out_specs=pl.BlockSpec((tk, heads * V_DIM), lambda b, g, i: (b * nk + i, g)),
        out_shape=jax.ShapeDtypeStruct((batch * seq, MLA_HEADS * V_DIM), BF16),
        scratch_shapes=[
            pltpu.VMEM((nitem, 1, ws), F32),
            pltpu.VMEM((nitem, V_EXT, ws), F32),
            pltpu.VMEM((ahead, ks, ws), F32),
            pltpu.VMEM((dlag + behind, ks, ws), BF16),
            pltpu.VMEM((dlag + behind, 1, ws), F32),
        ],
        compiler_params=_cparams(("arbitrary", "arbitrary", "arbitrary")),
        name="mla_attn",
    )(q5, k5, vt5)


def _tail_kernel(x_ref, yr_ref, o_ref, gt_ref, wor_ref, wom_ref, wout_ref, nf_ref, wup_ref, wdn_ref,
                 nfin_ref, out_ref):
    y_a = _dot(yr_ref[...], wor_ref[...])
    y_b = _dot(o_ref[...], wom_ref[...])
    merged = _sigmoid(gt_ref[:, :D_MODEL]) * y_a + _sigmoid(gt_ref[:, D_MODEL:]) * y_b
    h = x_ref[...] + _dot(merged.astype(BF16), wout_ref[...])
    f_in = _rms(h, nf_ref[...]).astype(BF16)
    cw = 1024
    acc = h
    for c in range(D_FF // cw):
        f = _dot(f_in, wup_ref[:, c * cw:(c + 1) * cw])
        f = jnp.square(jnp.maximum(f, 0.0)).astype(BF16)
        acc = acc + _dot(f, wdn_ref[c * cw:(c + 1) * cw, :])
    out_ref[...] = _rms(acc, nfin_ref[...])


def _tail(x2, yr, o, gt, wor, wom, wout, nf, wup, wdn, nfin, seq):
    t_tok = x2.shape[0]
    tm = min(TOKEN_TILE, seq)
    full = lambda shape: pl.BlockSpec(shape, lambda i: (0,) * len(shape), pipeline_mode=pl.Buffered(1))
    return pl.pallas_call(
        _tail_kernel,
        grid=(t_tok // tm,),
        in_specs=[
            pl.BlockSpec((tm, D_MODEL), lambda i: (i, 0)),
            pl.BlockSpec((tm, RWKV_DIM), lambda i: (i, 0)),
            pl.BlockSpec((tm, MLA_HEADS * V_DIM), lambda i: (i, 0)),
            pl.BlockSpec((tm, GT_COLS), lambda i: (i, 0)),
            full((RWKV_DIM, D_MODEL)), full((MLA_HEADS * V_DIM, D_MODEL)), full((D_MODEL, D_MODEL)),
            full((1, D_MODEL)), full((D_MODEL, D_FF)), full((D_FF, D_MODEL)), full((1, D_MODEL)),
        ],
        out_specs=pl.BlockSpec((tm, D_MODEL), lambda i: (i, 0)),
        out_shape=jax.ShapeDtypeStruct((t_tok, D_MODEL), F32),
        compiler_params=_cparams(("arbitrary",)),
        name="merge_ffn",
    )(x2, yr, o, gt, wor, wom, wout, nf, wup, wdn, nfin)


def _padc(w, n):
    return jnp.pad(w, ((0, 0), (0, n - w.shape[1])))


def _padr(w, n):
    return jnp.pad(w, ((0, n - w.shape[0]), (0, 0)))


def _rot_half(w):
    half = w.shape[-1] // 2
    return jnp.concatenate([-w[..., half:], w[..., :half]], axis=-1)


def kernel(x, positions, norm_mix, w_in, mu_shift, w0, w_up, a0, a_up, g_up, k_k, k_a, r_k, ln_w, ln_b, w_o_rwkv, q_norm, w_uq, kv_norm, w_ukv, w_o_mla, w_out, norm_ffn, w_ff_up, w_ff_down, norm_final):
    batch, seq, _ = x.shape
    t_tok = batch * seq
    x2 = x.reshape(t_tok, D_MODEL)
    l = 0

    wi = w_in[l].astype(BF16)
    o = 0
    w_r3 = wi[:, o:o + 3 * RWKV_DIM]; o += 3 * RWKV_DIM
    w_zw = wi[:, o:o + DECAY_LORA]; o += DECAY_LORA
    w_za = wi[:, o:o + ICLR_LORA]; o += ICLR_LORA
    w_zg = wi[:, o:o + GATE_LORA]; o += GATE_LORA
    w_cq = wi[:, o:o + Q_LORA]; o += Q_LORA
    w_ckv = wi[:, o:o + KV_LORA]; o += KV_LORA
    w_kr = wi[:, o:o + ROPE_DIM]; o += ROPE_DIM
    w_gate = wi[:, o:o + 2 * D_MODEL]
    zeros64 = jnp.zeros((D_MODEL, NOPE_DIM), wi.dtype)
    w_krb = jnp.concatenate([zeros64, w_kr, _rot_half(w_kr)], axis=1)
    w_a = jnp.concatenate([
        w_r3, _padc(w_zg, ZG_PAD), w_zw, w_za, w_cq, w_ckv, w_krb, w_gate], axis=1)
    mu = mu_shift[l]
    o = 3 * RWKV_DIM
    lo = o + DECAY_LORA + ICLR_LORA
    mu_a = jnp.concatenate([mu[:o], jnp.pad(mu[lo:], (0, ZG_PAD - GATE_LORA)), mu[o:lo]])[None, :]

    half = ROPE_DIM // 2
    inv_freq = 1.0 / (ROPE_THETA ** (jnp.arange(half, dtype=F32) * (2.0 / ROPE_DIM)))
    scale = (NOPE_DIM + ROPE_DIM) ** -0.5 * math.log2(math.e)
    wq = w_uq[l].reshape(Q_LORA, MLA_HEADS, NOPE_DIM + ROPE_DIM) * scale
    wq_p = jnp.pad(wq, ((0, 0), (0, 0), (0, HEAD_PAD - NOPE_DIM - ROPE_DIM)))
    wqt = jnp.transpose(wq_p, (1, 2, 0)).reshape(MLA_HEADS * HEAD_PAD, Q_LORA)
    wqrt = jnp.transpose(_rot_half(wq[..., NOPE_DIM:]), (1, 2, 0)).reshape(MLA_HEADS * ROPE_DIM, Q_LORA)
    fcol = jnp.concatenate([inv_freq, inv_freq])[:, None]
    wkv = w_ukv[l].reshape(KV_LORA, MLA_HEADS, NOPE_DIM + V_DIM)
    wk_p = jnp.pad(wkv[..., :NOPE_DIM], ((0, 0), (0, 0), (0, HEAD_PAD - NOPE_DIM)))
    wvt = jnp.transpose(wkv[..., NOPE_DIM:], (1, 2, 0))
    tk = min(TOKEN_TILE, seq)

    rw, gt, q5, k5, vt5 = _inproj(
        x2, norm_mix[l][None, :], w_a, mu_a, positions.reshape(t_tok // tk, 1, tk), fcol,
        q_norm[l][None, :], kv_norm[l][None, :], wqt.astype(BF16), wqrt.astype(BF16),
        wk_p.reshape(KV_LORA, -1).astype(BF16), wvt.astype(BF16), batch, seq)

    prm = [p[None, :] for p in (w0[l], a0[l], k_k[l], k_a[l], r_k[l].reshape(-1), ln_w[l], ln_b[l])]
    aup = jnp.concatenate([jnp.zeros_like(a_up[l]), a_up[l]], axis=0)
    yr = _rwkv(rw, prm, _padr(w_up[l], 128).astype(BF16), aup.astype(BF16),
               _padr(g_up[l], ZG_PAD).astype(BF16), batch, seq)

    o_att = _attn(q5, k5, vt5, batch, seq, tk)

    out = _tail(x2, yr, o_att, gt, w_o_rwkv[l].astype(BF16), w_o_mla[l].astype(BF16),
                w_out[l].astype(BF16), norm_ffn[l][None, :], w_ff_up[l].astype(BF16),
                w_ff_down[l].astype(BF16), norm_final[None, :], seq)
    return out.reshape(batch, seq, D_MODEL)
```

```python
import functools
import math

import jax
import jax.numpy as jnp
from jax import lax
from jax.experimental import pallas as pl
from jax.experimental.pallas import tpu as pltpu

F32 = jnp.float32
BF16 = jnp.bfloat16

D_MODEL = 1024
NORM_EPS = 1e-6
RWKV_HEAD = 64
RWKV_HEADS = 8
RWKV_DIM = RWKV_HEADS * RWKV_HEAD
DECAY_LORA = 64
ICLR_LORA = 64
GATE_LORA = 160
GN_EPS = 64e-5
MLA_HEADS = 8
Q_LORA = 256
KV_LORA = 128
NOPE_DIM = 64
ROPE_DIM = 32
V_DIM = 64
ROPE_THETA = 10000.0
D_FF = 4 * D_MODEL

LANE = 128
CHUNK = 64
INV_BASE = 8
PAIR = 2 * RWKV_HEAD
HEAD_PAD = 128
V_EXT = V_DIM + 16

ZG_PAD = 256
RW_COLS = 3 * RWKV_DIM + ZG_PAD + 128
ML_COLS = Q_LORA + KV_LORA + 128
GT_COLS = 2 * D_MODEL

VMEM_BYTES_V7X = 64 * 1024 * 1024
VMEM_LIMIT = VMEM_BYTES_V7X * 7 // 8
TOKEN_TILE = 512
RWKV_TILE = 4 * CHUNK
RWKV_PAIRS = RWKV_DIM // PAIR


def _cparams(sem):
    return pltpu.CompilerParams(dimension_semantics=sem, vmem_limit_bytes=VMEM_LIMIT)


def _sigmoid(x):
    return 1.0 / (1.0 + jnp.exp(-x))


def _rms(x, g):
    ms = jnp.mean(x * x, axis=-1, keepdims=True)
    return x * lax.rsqrt(ms + NORM_EPS) * g


def _dot(a, b):
    return jnp.dot(a, b, preferred_element_type=F32)


def _dot_nt(a, b):
    return lax.dot_general(a, b, (((1,), (1,)), ((), ())), preferred_element_type=F32)


def _dot_tn(a, b):
    return lax.dot_general(a, b, (((0,), (0,)), ((), ())), preferred_element_type=F32)


def _mla_prep_parts(ml, posrow_ref, fcol_ref, qn_ref, kvn_ref, wqt_ref, wqrt_ref, wk_ref, wvt_ref,
                    q_ref, k_ref, vt_ref, tk):
    c = {}
    rope = slice(NOPE_DIM, NOPE_DIM + ROPE_DIM)

    def trig():
        ang_t = fcol_ref[...] * posrow_ref[...].astype(F32)
        c["cos_t"] = jnp.cos(ang_t)
        c["sin_t"] = jnp.sin(ang_t)
        c["cq"] = _rms(ml[:, :Q_LORA], qn_ref[...]).astype(BF16)
        c["ckv"] = _rms(ml[:, Q_LORA:Q_LORA + KV_LORA], kvn_ref[...]).astype(BF16)

    def project():
        c["qf"] = _dot_nt(wqt_ref[...], c["cq"])
        c["qr"] = _dot_nt(wqrt_ref[...], c["cq"])
        c["kf"] = _dot(c["ckv"], wk_ref[...])

    def k_rope():
        zlo = jnp.zeros((NOPE_DIM, tk), F32)
        zhi = jnp.zeros((HEAD_PAD - NOPE_DIM - ROPE_DIM, tk), F32)
        cosf = jnp.concatenate([zlo, c["cos_t"], zhi], axis=0).T
        sinf = jnp.concatenate([zlo, c["sin_t"], zhi], axis=0).T
        krb = ml[:, Q_LORA + KV_LORA:Q_LORA + KV_LORA + HEAD_PAD]
        c["k_rope"] = krb * cosf + pltpu.roll(krb, HEAD_PAD - ROPE_DIM, 1) * sinf

    def heads(lo, hi):
        def f():
            for h in range(lo, hi):
                hs = slice(h * HEAD_PAD, (h + 1) * HEAD_PAD)
                qf_t = c["qf"][hs]
                qr_t = c["qr"][h * ROPE_DIM:(h + 1) * ROPE_DIM]
                q_t = jnp.concatenate([qf_t[:NOPE_DIM], qf_t[rope] * c["cos_t"] + qr_t * c["sin_t"],
                                       qf_t[NOPE_DIM + ROPE_DIM:]], axis=0)
                q_ref[h] = q_t.astype(q_ref.dtype)
                k_ref[h] = (c["kf"][:, hs] + c["k_rope"]).astype(k_ref.dtype)
                vt_ref[h, :V_DIM] = _dot_nt(wvt_ref[h], c["ckv"]).astype(vt_ref.dtype)
                vt_ref[h, V_DIM:] = jnp.ones((V_EXT - V_DIM, tk), vt_ref.dtype)
        return f

    half = MLA_HEADS // 2
    return [trig, project, k_rope, heads(0, half), heads(half, MLA_HEADS)]


def _inproj_kernel(x_ref, g_ref, w_ref, mu_ref, posrow_ref, fcol_ref, qn_ref, kvn_ref,
                   wqt_ref, wqrt_ref, wk_ref, wvt_ref,
                   rw_ref, gt_ref, q_ref, k_ref, vt_ref, carry_ref, *, tiles_per_seq, tm):
    i = pl.program_id(0)
    u = _rms(x_ref[...], g_ref[...]).astype(BF16)

    @pl.when(i % tiles_per_seq == 0)
    def _():
        carry_ref[...] = jnp.zeros_like(carry_ref)

    def shift_store(cs, z):
        prev = pltpu.roll(z, 1, 0)
        row0 = lax.broadcasted_iota(jnp.int32, z.shape, 0) == 0
        prev = jnp.where(row0, carry_ref[7:8, cs], prev)
        carry_ref[:, cs] = z[tm - 8:tm, :]
        rw_ref[:, cs] = z + (prev - z) * mu_ref[:, cs]

    def gate_store(cs, z):
        gt_ref[:, cs] = z.astype(gt_ref.dtype)

    cw = 512
    jobs = [(c0, slice(c0, min(c0 + cw, RW_COLS)), shift_store) for c0 in range(0, RW_COLS, cw)]
    jobs += [(RW_COLS + ML_COLS + c0, slice(c0, c0 + cw), gate_store) for c0 in range(0, GT_COLS, cw)]
    ml = _dot(u, w_ref[:, RW_COLS:RW_COLS + ML_COLS])
    fillers = _mla_prep_parts(ml, posrow_ref, fcol_ref, qn_ref, kvn_ref, wqt_ref, wqrt_ref, wk_ref, wvt_ref,
                              q_ref, k_ref, vt_ref, tm)
    mm = lambda job: _dot(u, w_ref[:, job[0]:job[0] + (job[1].stop - job[1].start)])
    z = mm(jobs[0])
    for n, job in enumerate(jobs):
        z_next = mm(jobs[n + 1]) if n + 1 < len(jobs) else None
        if n < len(fillers):
            fillers[n]()
        job[2](job[1], z)
        z = z_next


def _inproj(x2, g, w_a, mu_a, posrow, fcol, qn, kvn, wqt, wqrt, wk, wvt, batch, seq):
    t_tok = x2.shape[0]
    tm = min(TOKEN_TILE, seq)
    nk = seq // tm
    ncol = RW_COLS + ML_COLS + GT_COLS
    kern = functools.partial(_inproj_kernel, tiles_per_seq=nk, tm=tm)
    full = lambda shape: pl.BlockSpec(shape, lambda i: (0,) * len(shape))
    head_major = lambda rows, cols: pl.BlockSpec((None, MLA_HEADS, None, rows, cols),
                                                 lambda i: (i // nk, 0, i % nk, 0, 0))
    return pl.pallas_call(
        kern,
        grid=(t_tok // tm,),
        in_specs=[
            pl.BlockSpec((tm, D_MODEL), lambda i: (i, 0)),
            full((1, D_MODEL)),
            pl.BlockSpec((D_MODEL, ncol), lambda i: (0, 0), pipeline_mode=pl.Buffered(1)),
            full((1, RW_COLS)),
            pl.BlockSpec((None, 1, tm), lambda i: (i, 0, 0)),
            full((ROPE_DIM, 1)), full((1, Q_LORA)), full((1, KV_LORA)),
            full((MLA_HEADS * HEAD_PAD, Q_LORA)), full((MLA_HEADS * ROPE_DIM, Q_LORA)),
            full((KV_LORA, MLA_HEADS * HEAD_PAD)), full((MLA_HEADS, V_DIM, KV_LORA)),
        ],
        out_specs=[
            pl.BlockSpec((tm, RW_COLS), lambda i: (i, 0)),
            pl.BlockSpec((tm, GT_COLS), lambda i: (i, 0)),
            head_major(HEAD_PAD, tm), head_major(tm, HEAD_PAD), head_major(V_EXT, tm),
        ],
        out_shape=[
            jax.ShapeDtypeStruct((t_tok, RW_COLS), F32),
            jax.ShapeDtypeStruct((t_tok, GT_COLS), BF16),
            jax.ShapeDtypeStruct((batch, MLA_HEADS, nk, HEAD_PAD, tm), BF16),
            jax.ShapeDtypeStruct((batch, MLA_HEADS, nk, tm, HEAD_PAD), BF16),
            jax.ShapeDtypeStruct((batch, MLA_HEADS, nk, V_EXT, tm), BF16),
        ],
        scratch_shapes=[pltpu.VMEM((8, RW_COLS), F32)],
        compiler_params=_cparams(("arbitrary",)),
        name="inproj",
    )(x2, g, w_a, mu_a, posrow, fcol, qn, kvn, wqt, wqrt, wk, wvt)


def _split3(x):
    hi = x.astype(BF16)
    r1 = x - hi.astype(F32)
    mid = r1.astype(BF16)
    lo = (r1 - mid.astype(F32)).astype(BF16)
    return hi, mid, lo


def _rwkv_masks():
    n = 2 * CHUNK
    row = lax.broadcasted_iota(jnp.int32, (n, n), 0)
    col = lax.broadcasted_iota(jnp.int32, (n, n), 1)
    same = (row // CHUNK) == (col // CHUNK)
    strict = jnp.where(same & (row > col), 1.0, 0.0).astype(F32)
    incl = jnp.where(same & (row >= col), 1.0, 0.0).astype(F32)
    base = jnp.where((row // INV_BASE) == (col // INV_BASE), 1.0, 0.0).astype(F32)
    levels = []
    blk = INV_BASE
    while blk < CHUNK:
        levels.append((jnp.where(((row // (2 * blk)) == (col // (2 * blk))) & ((row // blk) > (col // blk)),
                                 1.0, 0.0).astype(F32), blk))
        blk *= 2
    eye = jnp.where(row == col, 1.0, 0.0).astype(F32)
    headsel = (row // CHUNK) == (col // RWKV_HEAD)
    return strict, incl, base, levels, eye, headsel


def _rows(x, blk):
    return jnp.concatenate([x[s:s + blk] for s in range(blk, x.shape[0], 2 * blk)], axis=0)


def _merge_rows(x, odd, blk):
    parts = []
    for j, s in enumerate(range(0, x.shape[0], 2 * blk)):
        parts += [x[s:s + blk], odd[j * blk:(j + 1) * blk]]
    return jnp.concatenate(parts, axis=0)


def _rwkv_wave(refs, pairs, nchunk, consts, head_sum):
    (zr_ref, zk_ref, zv_ref, w0_ref, a0_ref, kk_ref, ka_ref, rk_ref, lnw_ref, lnb_ref,
     wup_ref, aup_ref, gup_ref, y_ref, h_ref, tanh_zw, za, sig_zg) = refs
    strict, incl, base, levels, eye, headsel, tril = consts
    incl2 = jnp.concatenate([incl, incl], axis=1)
    n = 2 * CHUNK
    bf = lambda x: x.astype(BF16)
    items = [(q, slice(c * CHUNK, (c + 1) * CHUNK)) for c in range(nchunk) for q in range(len(pairs))]
    rng = range(len(items))
    c = {}

    def stack(x):
        return jnp.where(headsel, jnp.concatenate([x, x], axis=0), 0.0)

    def pre():
        c["keep"] = []
        streams = []
        for p in pairs:
            ls = slice(p * PAIR, (p + 1) * PAIR)
            zr, zk, zv = zr_ref[:, ls], zk_ref[:, ls], zv_ref[:, ls]
            w_pre = w0_ref[:, ls] + _dot(tanh_zw, wup_ref[:, ls])
            lw = -math.exp(-0.5) * _sigmoid(w_pre)
            iclr = _sigmoid(a0_ref[:, ls] + _dot(za, aup_ref[:, ls]))
            gate = _dot(sig_zg, gup_ref[:, ls])
            kk = zk * kk_ref[:, ls]
            kk = kk / jnp.maximum(jnp.sqrt(head_sum(kk * kk)), 1e-12)
            k = zk * (1.0 + (iclr - 1.0) * ka_ref[:, ls])
            streams.append((zr, k, zv, -kk, kk * iclr, lw))
            c["keep"].append((zr, k, zv, gate))
        for j, name in enumerate(("r", "k", "v", "a", "b", "lw")):
            c[name] = [streams[q][j][s] for q, s in items]

    def cumsum():
        c["cs"] = []
        for i in rng:
            hi, mid, lo = _split3(c["lw"][i])
            c["cs"].append(_dot(tril, hi) + _dot(tril, mid) + _dot(tril, lo))

    def scale():
        cs, lw = c["cs"], c["lw"]
        g_in = [jnp.exp(cs[i]) for i in rng]
        g_ex = [jnp.exp(cs[i] - lw[i]) for i in rng]
        g_inv = [jnp.exp(-cs[i]) for i in rng]
        c["g_last"] = [jnp.exp(cs[i][CHUNK - 1:CHUNK, :]) for i in rng]
        c["rt"] = [stack(c["r"][i] * g_in[i]) for i in rng]
        c["at"] = [bf(stack(c["a"][i] * g_ex[i])) for i in rng]
        bt = [stack(c["b"][i] * g_inv[i]) for i in rng]
        kt = [stack(c["k"][i] * g_inv[i]) for i in rng]
        c["bkh_t"] = [bf(jnp.concatenate([(bt[i] * c["g_last"][i]).T, (kt[i] * c["g_last"][i]).T], axis=1))
                      for i in rng]
        c["v2"] = [bf(stack(c["v"][i])) for i in rng]
        c["bk"] = [bf(jnp.concatenate([bt[i], kt[i]], axis=0)) for i in rng]

    def gram():
        ga = [_dot_nt(c["at"][i], c["bk"][i]) for i in rng]
        gr = [_dot_nt(bf(c["rt"][i]), c["bk"][i]) for i in rng]
        c["a_ab"] = [g[:, :n] * strict for g in ga]
        c["a_ak"] = [bf(g[:, n:] * strict) for g in ga]
        c["a_rbk"] = [bf(g * incl2) for g in gr]

    def inv0():
        c["d"] = [bf(a * base) for a in c["a_ab"]]
        c["t"] = [eye + a * base for a in c["a_ab"]]
        c["p"] = [_dot(d, d) for d in c["d"]]

    def inv_double():
        xs = [_dot(bf(jnp.concatenate([t, p], axis=0)), bf(p)) for t, p in zip(c["t"], c["p"])]
        c["t"] = [t + x[:n] for t, x in zip(c["t"], xs)]
        c["p"] = [x[n:] for x in xs]

    def inv_last():
        c["t"] = [t + _dot(bf(t), bf(p)) for t, p in zip(c["t"], c["p"])]

    def level_a(msk, blk):
        def f():
            c["tb"] = [bf(t) for t in c["t"]]
            c["x"] = [bf(_dot(bf(_rows(t, blk)), bf(a * msk))) for t, a in zip(c["t"], c["a_ab"])]
        return f

    def level_b(blk):
        def f():
            c["t"] = [_merge_rows(t, _rows(t, blk) + _dot(x, tb), blk)
                      for t, x, tb in zip(c["t"], c["x"], c["tb"])]
        return f

    def apply_v():
        c["av"] = [bf(_dot(c["a_ak"][i], c["v2"][i])) for i in rng]

    def apply_t():
        c["wu"] = [bf(_dot(bf(c["t"][i]), jnp.concatenate([c["at"][i], c["av"][i]], axis=1)))
                   for i in rng]

    def assemble():
        zero = jnp.zeros((n, n), BF16)
        big = [_dot(jnp.concatenate([c["a_rbk"][i], c["bkh_t"][i]], axis=0),
                    jnp.concatenate([c["wu"][i], jnp.concatenate([zero, c["v2"][i]], axis=1)], axis=0))
               for i in rng]
        qeff = [c["rt"][i] + big[i][:n, :n] for i in rng]
        c["y0"] = [big[i][:n, n:] for i in rng]
        m = [eye * c["g_last"][i] + big[i][n:, :n] for i in rng]
        c["nn"] = [big[i][n:, n:] for i in rng]
        c["mq"] = [bf(jnp.concatenate([m[i], qeff[i]], axis=0)) for i in rng]

    def chain():
        hs = [h_ref[p] for p in pairs]
        ys = [[] for _ in pairs]
        for i in rng:
            q = items[i][0]
            hy = _dot(c["mq"][i], bf(hs[q]))
            hs[q] = hy[:n] + c["nn"][i]
            y2 = hy[n:] + c["y0"][i]
            ys[q].append(y2[:CHUNK, :] + y2[CHUNK:, :])
        for q, p in enumerate(pairs):
            h_ref[p] = hs[q]
        c["ys"] = [jnp.concatenate(y, axis=0) for y in ys]

    def post():
        for q, p in enumerate(pairs):
            ls = slice(p * PAIR, (p + 1) * PAIR)
            zr, k, zv, gate = c["keep"][q]
            y = c["ys"][q]
            mean = head_sum(y) * (1.0 / RWKV_HEAD)
            yc = y - mean
            var = head_sum(yc * yc) * (1.0 / RWKV_HEAD)
            yn = yc * lax.rsqrt(var + GN_EPS) * lnw_ref[:, ls] + lnb_ref[:, ls]
            bonus = head_sum(zr * k * rk_ref[:, ls]) * zv
            y_ref[:, ls] = ((yn + bonus) * gate).astype(y_ref.dtype)

    doublings = [inv_double] * (INV_BASE.bit_length() - 3)
    level_stages = [f for msk, blk in levels for f in (level_a(msk, blk), level_b(blk))]
    return ([pre, cumsum, scale, gram, inv0] + doublings + [inv_last] + level_stages
            + [apply_v, apply_t, assemble, chain, post])


def _rwkv_kernel(zr_ref, zk_ref, zv_ref, zwa_ref, zg_ref,
                 w0_ref, a0_ref, kk_ref, ka_ref, rk_ref, lnw_ref, lnb_ref,
                 wup_ref, aup_ref, gup_ref, y_ref, h_ref, *, tb, npair):
    @pl.when(pl.program_id(2) == 0)
    def _():
        h_ref[...] = jnp.zeros_like(h_ref)

    lane = lax.broadcasted_iota(jnp.int32, (tb, PAIR), 1)
    head0 = lane < RWKV_HEAD

    def head_sum(x):
        s0 = jnp.sum(jnp.where(head0, x, 0.0), axis=-1, keepdims=True)
        s1 = jnp.sum(jnp.where(head0, 0.0, x), axis=-1, keepdims=True)
        return jnp.where(head0, s0, s1)

    trow = lax.broadcasted_iota(jnp.int32, (CHUNK, CHUNK), 0)
    tcol = lax.broadcasted_iota(jnp.int32, (CHUNK, CHUNK), 1)
    tril = jnp.where(trow >= tcol, 1.0, 0.0).astype(BF16)
    consts = _rwkv_masks() + (tril,)
    refs = (zr_ref, zk_ref, zv_ref, w0_ref, a0_ref, kk_ref, ka_ref, rk_ref, lnw_ref, lnb_ref,
            wup_ref, aup_ref, gup_ref, y_ref, h_ref,
            jnp.tanh(zwa_ref[...]).astype(BF16), zwa_ref[...].astype(BF16),
            _sigmoid(zg_ref[...]).astype(BF16))
    for stage in _rwkv_wave(refs, list(range(npair)), tb // CHUNK, consts, head_sum):
        stage()


def _rwkv(rw, prm, wup, aup, gup, batch, seq):
    t_tok = rw.shape[0]
    tb = min(RWKV_TILE, seq)
    nt = seq // tb
    npair = RWKV_PAIRS
    wid = npair * PAIR
    ngrp = RWKV_DIM // wid

    def tok(base):
        return pl.BlockSpec((tb, wid), lambda b, p, t: (b * nt + t, base * ngrp + p))

    def lora(base, width):
        return pl.BlockSpec((tb, width), lambda b, p, t: (b * nt + t, base))

    prm_spec = pl.BlockSpec((1, wid), lambda b, p, t: (0, p))
    in_specs = [
        tok(0), tok(1), tok(2),
        lora((3 * RWKV_DIM + ZG_PAD) // 128, 128), lora(3 * RWKV_DIM // ZG_PAD, ZG_PAD),
    ] + [prm_spec] * 7 + [
        pl.BlockSpec((128, wid), lambda b, p, t: (0, p)),
        pl.BlockSpec((128, wid), lambda b, p, t: (0, p)),
        pl.BlockSpec((ZG_PAD, wid), lambda b, p, t: (0, p)),
    ]
    return pl.pallas_call(
        functools.partial(_rwkv_kernel, tb=tb, npair=npair),
        grid=(batch, ngrp, nt),
        in_specs=in_specs,
        out_specs=pl.BlockSpec((tb, wid), lambda b, p, t: (b * nt + t, p)),
        out_shape=jax.ShapeDtypeStruct((t_tok, RWKV_DIM), BF16),
        scratch_shapes=[pltpu.VMEM((npair, PAIR, PAIR), F32)],
        compiler_params=_cparams(("arbitrary", "arbitrary", "arbitrary")),
        name="rwkv7",
    )(rw, rw, rw, rw, rw, *prm, wup, aup, gup)


def _attn_kernel(q_ref, k_ref, vt_ref, o_ref, m_ref, acc_ref, s_ref, p_ref, al_ref,
                 *, tk, heads, ks, ws, ahead, dlag, behind):
    i = pl.program_id(2)
    krow = lax.broadcasted_iota(jnp.int32, (ks, ws), 0)
    qcol = lax.broadcasted_iota(jnp.int32, (ks, ws), 1)
    neg = jnp.finfo(F32).min
    nsub, nstrip = tk // ks, tk // ws
    items = [(h, g) for h in range(heads) for g in range(nstrip)]
    rng = range(len(items))
    qs = [q_ref[h, 0, :, g * ws:(g + 1) * ws] for h, g in items]

    full_units = [(sub, i) for sub in range(nsub) for i in rng]
    diag_units = [(sub, i) for sub, i in full_units if not sub * ks > items[i][1] * ws + ws - 1]
    lag = dlag + behind
    tail_units = full_units[len(full_units) - lag:]
    assert diag_units[:ahead] == full_units[:ahead] and len(diag_units) > ahead + lag

    def score(j, unit):
        sub, i = unit
        return _dot(k_ref[items[i][0], j, sub * ks:(sub + 1) * ks, :], qs[i])

    def pv(j, unit, p):
        sub, i = unit
        return _dot(vt_ref[items[i][0], j, :, sub * ks:(sub + 1) * ks], p)

    def step(j, masked):
        units = diag_units if masked else full_units
        nu = len(units)
        keep = 0 if masked else lag
        jprev = jnp.maximum(j - 1, 0)
        s_val, p_val, pv_val, al_val, pv_old = {}, {}, {}, {}, {}
        for t in range(nu + (lag if masked else 0)):
            if t + ahead < nu:
                s_val[t + ahead] = score(j, units[t + ahead])
            if t < nu:
                sub, i = units[t]
                s = s_val.pop(t) if t >= ahead else s_ref[t]
                if masked and sub * ks + ks - 1 > items[i][1] * ws:
                    s = jnp.where(krow + (sub * ks - items[i][1] * ws) <= qcol, s, neg)
                m_old = m_ref[i]
                m_new = jnp.maximum(m_old, jnp.max(s, axis=0, keepdims=True))
                al_val[t] = jnp.exp2(m_old - m_new)
                m_ref[i] = m_new
                p_val[t] = jnp.exp2(s - m_new).astype(BF16)
                if not masked and t + ahead >= nu:
                    s_ref[t + ahead - nu] = score(j + 1, full_units[t + ahead - nu])
            if t < lag:
                pv_old[t] = pv(jprev, tail_units[t], p_ref[t])
            d = t - dlag
            if 0 <= d < nu - keep:
                pv_val[d] = pv(j, units[d], p_val.pop(d))
            x = t - behind
            if 0 <= x < lag:
                i = tail_units[x][1]
                acc_ref[i] = al_ref[x] * acc_ref[i] + pv_old.pop(x)
            w = d - behind
            if 0 <= w < nu - keep:
                i = units[w][1]
                acc_ref[i] = al_val.pop(w) * acc_ref[i] + pv_val.pop(w)
        if not masked:
            for x in range(lag):
                p_ref[x] = p_val[nu - lag + x]
                al_ref[x] = al_val[nu - lag + x]

    m_ref[...] = jnp.full(m_ref.shape, neg, F32)
    acc_ref[...] = jnp.zeros(acc_ref.shape, F32)
    p_ref[...] = jnp.zeros(p_ref.shape, BF16)
    al_ref[...] = jnp.ones(al_ref.shape, F32)
    for t in range(ahead):
        s_ref[t] = score(0, full_units[t])

    def body(j, c):
        step(j, False)
        return c

    lax.fori_loop(0, i, body, 0)
    step(i, True)
    outs = []
    for h in range(heads):
        accs = [acc_ref[h * nstrip + g] for g in range(nstrip)]
        acc = jnp.concatenate([a[:V_DIM] / a[V_DIM:V_DIM + 1] for a in accs], axis=1)
        outs.append(acc.T)
    o_ref[...] = jnp.concatenate(outs, axis=-1).astype(o_ref.dtype)


def _attn(q5, k5, vt5, batch, seq, tk):
    nk = seq // tk
    heads = 8
    ngrp = MLA_HEADS // heads
    ks, ws = min(256, tk), min(256, tk)
    ahead, dlag, behind = 3, 2, 2
    nitem = heads * (tk // ws)
    return pl.pallas_call(
        functools.partial(_attn_kernel, tk=tk, heads=heads, ks=ks, ws=ws, ahead=ahead, dlag=dlag, behind=behind),
        grid=(batch, ngrp, nk),
        in_specs=[
            pl.BlockSpec((None, heads, 1, HEAD_PAD, tk), lambda b, g, i: (b, g, i, 0, 0)),
            pl.BlockSpec((None, heads, nk, tk, HEAD_PAD), lambda b, g, i: (b, g, 0, 0, 0)),
            pl.BlockSpec((None, heads, nk, V_EXT, tk), lambda b, g, i: (b, g, 0, 0, 0)),
        ],
        out_specs=pl.BlockSpec((tk, heads * V_DIM), lambda b, g, i: (b * nk + i, g)),
        out_shape=jax.ShapeDtypeStruct((batch * seq, MLA_HEADS * V_DIM), BF16),
        scratch_shapes=[
            pltpu.VMEM((nitem, 1, ws), F32),
            pltpu.VMEM((nitem, V_EXT, ws), F32),
            pltpu.VMEM((ahead, ks, ws), F32),
            pltpu.VMEM((dlag + behind, ks, ws), BF16),
            pltpu.VMEM((dlag + behind, 1, ws), F32),
        ],
        compiler_params=_cparams(("arbitrary", "arbitrary", "arbitrary")),
        name="mla_attn",
    )(q5, k5, vt5)


def _tail_kernel(x_ref, yr_ref, o_ref, gt_ref, wor_ref, wom_ref, wout_ref, nf_ref, wup_ref, wdn_ref,
                 nfin_ref, out_ref):
    y_a = _dot(yr_ref[...], wor_ref[...])
    y_b = _dot(o_ref[...], wom_ref[...])
    gate_a = gt_ref[:, :D_MODEL].astype(F32)
    gate_b = gt_ref[:, D_MODEL:].astype(F32)
    merged = _sigmoid(gate_a) * y_a + _sigmoid(gate_b) * y_b
    h = x_ref[...] + _dot(merged.astype(BF16), wout_ref[...])
    f_in = _rms(h, nf_ref[...]).astype(BF16)
    cw = 1024
    acc = h
    for c in range(D_FF // cw):
        f = _dot(f_in, wup_ref[:, c * cw:(c + 1) * cw])
        f = jnp.square(jnp.maximum(f, 0.0)).astype(BF16)
        acc = acc + _dot(f, wdn_ref[c * cw:(c + 1) * cw, :])
    out_ref[...] = _rms(acc, nfin_ref[...])


def _tail(x2, yr, o, gt, wor, wom, wout, nf, wup, wdn, nfin, seq):
    t_tok = x2.shape[0]
    tm = min(TOKEN_TILE, seq)
    full = lambda shape: pl.BlockSpec(shape, lambda i: (0,) * len(shape), pipeline_mode=pl.Buffered(1))
    return pl.pallas_call(
        _tail_kernel,
        grid=(t_tok // tm,),
        in_specs=[
            pl.BlockSpec((tm, D_MODEL), lambda i: (i, 0)),
            pl.BlockSpec((tm, RWKV_DIM), lambda i: (i, 0)),
            pl.BlockSpec((tm, MLA_HEADS * V_DIM), lambda i: (i, 0)),
            pl.BlockSpec((tm, GT_COLS), lambda i: (i, 0)),
            full((RWKV_DIM, D_MODEL)), full((MLA_HEADS * V_DIM, D_MODEL)), full((D_MODEL, D_MODEL)),
            full((1, D_MODEL)), full((D_MODEL, D_FF)), full((D_FF, D_MODEL)), full((1, D_MODEL)),
        ],
        out_specs=pl.BlockSpec((tm, D_MODEL), lambda i: (i, 0)),
        out_shape=jax.ShapeDtypeStruct((t_tok, D_MODEL), F32),
        compiler_params=_cparams(("arbitrary",)),
        name="merge_ffn",
    )(x2, yr, o, gt, wor, wom, wout, nf, wup, wdn, nfin)


def _padc(w, n):
    return jnp.pad(w, ((0, 0), (0, n - w.shape[1])))


def _padr(w, n):
    return jnp.pad(w, ((0, n - w.shape[0]), (0, 0)))


def _rot_half(w):
    half = w.shape[-1] // 2
    return jnp.concatenate([-w[..., half:], w[..., :half]], axis=-1)


def kernel(x, positions, norm_mix, w_in, mu_shift, w0, w_up, a0, a_up, g_up, k_k, k_a, r_k, ln_w, ln_b, w_o_rwkv, q_norm, w_uq, kv_norm, w_ukv, w_o_mla, w_out, norm_ffn, w_ff_up, w_ff_down, norm_final):
    batch, seq, _ = x.shape
    t_tok = batch * seq
    x2 = x.reshape(t_tok, D_MODEL)
    l = 0

    wi = w_in[l].astype(BF16)
    o = 0
    w_r3 = wi[:, o:o + 3 * RWKV_DIM]; o += 3 * RWKV_DIM
    w_zw = wi[:, o:o + DECAY_LORA]; o += DECAY_LORA
    w_za = wi[:, o:o + ICLR_LORA]; o += ICLR_LORA
    w_zg = wi[:, o:o + GATE_LORA]; o += GATE_LORA
    w_cq = wi[:, o:o + Q_LORA]; o += Q_LORA
    w_ckv = wi[:, o:o + KV_LORA]; o += KV_LORA
    w_kr = wi[:, o:o + ROPE_DIM]; o += ROPE_DIM
    w_gate = wi[:, o:o + 2 * D_MODEL]
    zeros64 = jnp.zeros((D_MODEL, NOPE_DIM), wi.dtype)
    w_krb = jnp.concatenate([zeros64, w_kr, _rot_half(w_kr)], axis=1)
    w_a = jnp.concatenate([
        w_r3, _padc(w_zg, ZG_PAD), w_zw, w_za, w_cq, w_ckv, w_krb, w_gate], axis=1)
    mu = mu_shift[l]
    o = 3 * RWKV_DIM
    lo = o + DECAY_LORA + ICLR_LORA
    mu_a = jnp.concatenate([mu[:o], jnp.pad(mu[lo:], (0, ZG_PAD - GATE_LORA)), mu[o:lo]])[None, :]

    half = ROPE_DIM // 2
    inv_freq = 1.0 / (ROPE_THETA ** (jnp.arange(half, dtype=F32) * (2.0 / ROPE_DIM)))
    scale = (NOPE_DIM + ROPE_DIM) ** -0.5 * math.log2(math.e)
    wq = w_uq[l].reshape(Q_LORA, MLA_HEADS, NOPE_DIM + ROPE_DIM) * scale
    wq_p = jnp.pad(wq, ((0, 0), (0, 0), (0, HEAD_PAD - NOPE_DIM - ROPE_DIM)))
    wqt = jnp.transpose(wq_p, (1, 2, 0)).reshape(MLA_HEADS * HEAD_PAD, Q_LORA)
    wqrt = jnp.transpose(_rot_half(wq[..., NOPE_DIM:]), (1, 2, 0)).reshape(MLA_HEADS * ROPE_DIM, Q_LORA)
    fcol = jnp.concatenate([inv_freq, inv_freq])[:, None]
    wkv = w_ukv[l].reshape(KV_LORA, MLA_HEADS, NOPE_DIM + V_DIM)
    wk_p = jnp.pad(wkv[..., :NOPE_DIM], ((0, 0), (0, 0), (0, HEAD_PAD - NOPE_DIM)))
    wvt = jnp.transpose(wkv[..., NOPE_DIM:], (1, 2, 0))
    tk = min(TOKEN_TILE, seq)

    rw, gt, q5, k5, vt5 = _inproj(
        x2, norm_mix[l][None, :], w_a, mu_a, positions.reshape(t_tok // tk, 1, tk), fcol,
        q_norm[l][None, :], kv_norm[l][None, :], wqt.astype(BF16), wqrt.astype(BF16),
        wk_p.reshape(KV_LORA, -1).astype(BF16), wvt.astype(BF16), batch, seq)

    prm = [p[None, :] for p in (w0[l], a0[l], k_k[l], k_a[l], r_k[l].reshape(-1), ln_w[l], ln_b[l])]
    aup = jnp.concatenate([jnp.zeros_like(a_up[l]), a_up[l]], axis=0)
    yr = _rwkv(rw, prm, _padr(w_up[l], 128).astype(BF16), aup.astype(BF16),
               _padr(g_up[l], ZG_PAD).astype(BF16), batch, seq)

    o_att = _attn(q5, k5, vt5, batch, seq, tk)

    out = _tail(x2, yr, o_att, gt, w_o_rwkv[l].astype(BF16), w_o_mla[l].astype(BF16),
                w_out[l].astype(BF16), norm_ffn[l][None, :], w_ff_up[l].astype(BF16),
                w_ff_down[l].astype(BF16), norm_final[None, :], seq)
    return out.reshape(batch, seq, D_MODEL)
```

```python
import functools
import math

import jax
import jax.numpy as jnp
from jax import lax
from jax.experimental import pallas as pl
from jax.experimental.pallas import tpu as pltpu

F32 = jnp.float32
BF16 = jnp.bfloat16

D_MODEL = 1024
NORM_EPS = 1e-6
RWKV_HEAD = 64
RWKV_HEADS = 8
RWKV_DIM = RWKV_HEADS * RWKV_HEAD
DECAY_LORA = 64
ICLR_LORA = 64
GATE_LORA = 160
GN_EPS = 64e-5
MLA_HEADS = 8
Q_LORA = 256
KV_LORA = 128
NOPE_DIM = 64
ROPE_DIM = 32
V_DIM = 64
ROPE_THETA = 10000.0
D_FF = 4 * D_MODEL

LANE = 128
CHUNK = 64
INV_BASE = 8
PAIR = 2 * RWKV_HEAD
HEAD_PAD = 128
V_EXT = V_DIM + 16

ZG_PAD = 256
RW_COLS = 3 * RWKV_DIM + ZG_PAD + 128
ML_COLS = Q_LORA + KV_LORA + 128
GT_COLS = 2 * D_MODEL

VMEM_BYTES_V7X = 64 * 1024 * 1024
VMEM_LIMIT = VMEM_BYTES_V7X * 7 // 8
TOKEN_TILE = 512
RWKV_TILE = 4 * CHUNK
RWKV_PAIRS = RWKV_DIM // PAIR


def _cparams(sem):
    return pltpu.CompilerParams(dimension_semantics=sem, vmem_limit_bytes=VMEM_LIMIT)


def _sigmoid(x):
    return 1.0 / (1.0 + jnp.exp(-x))


def _rms(x, g):
    ms = jnp.mean(x * x, axis=-1, keepdims=True)
    return x * lax.rsqrt(ms + NORM_EPS) * g


def _dot(a, b):
    return jnp.dot(a, b, preferred_element_type=F32)


def _dot_nt(a, b):
    return lax.dot_general(a, b, (((1,), (1,)), ((), ())), preferred_element_type=F32)


def _dot_tn(a, b):
    return lax.dot_general(a, b, (((0,), (0,)), ((), ())), preferred_element_type=F32)


def _mla_prep_parts(ml, posrow_ref, fcol_ref, qn_ref, kvn_ref, wqt_ref, wqrt_ref, wk_ref, wvt_ref,
                    q_ref, k_ref, vt_ref, tk):
    c = {}
    rope = slice(NOPE_DIM, NOPE_DIM + ROPE_DIM)

    def trig():
        ang_t = fcol_ref[...] * posrow_ref[...].astype(F32)
        c["cos_t"] = jnp.cos(ang_t)
        c["sin_t"] = jnp.sin(ang_t)
        c["cq"] = _rms(ml[:, :Q_LORA], qn_ref[...]).astype(BF16)
        c["ckv"] = _rms(ml[:, Q_LORA:Q_LORA + KV_LORA], kvn_ref[...]).astype(BF16)

    def project():
        c["qf"] = _dot_nt(wqt_ref[...], c["cq"])
        c["qr"] = _dot_nt(wqrt_ref[...], c["cq"])
        c["kf"] = _dot(c["ckv"], wk_ref[...])

    def k_rope():
        zlo = jnp.zeros((NOPE_DIM, tk), F32)
        zhi = jnp.zeros((HEAD_PAD - NOPE_DIM - ROPE_DIM, tk), F32)
        cosf = jnp.concatenate([zlo, c["cos_t"], zhi], axis=0).T
        sinf = jnp.concatenate([zlo, c["sin_t"], zhi], axis=0).T
        krb = ml[:, Q_LORA + KV_LORA:Q_LORA + KV_LORA + HEAD_PAD]
        c["k_rope"] = krb * cosf + pltpu.roll(krb, HEAD_PAD - ROPE_DIM, 1) * sinf

    def heads(lo, hi):
        def f():
            for h in range(lo, hi):
                hs = slice(h * HEAD_PAD, (h + 1) * HEAD_PAD)
                qf_t = c["qf"][hs]
                qr_t = c["qr"][h * ROPE_DIM:(h + 1) * ROPE_DIM]
                q_t = jnp.concatenate([qf_t[:NOPE_DIM], qf_t[rope] * c["cos_t"] + qr_t * c["sin_t"],
                                       qf_t[NOPE_DIM + ROPE_DIM:]], axis=0)
                q_ref[h] = q_t.astype(q_ref.dtype)
                k_ref[h] = (c["kf"][:, hs] + c["k_rope"]).astype(k_ref.dtype)
                vt_ref[h, :V_DIM] = _dot_nt(wvt_ref[h], c["ckv"]).astype(vt_ref.dtype)
                vt_ref[h, V_DIM:] = jnp.ones((V_EXT - V_DIM, tk), vt_ref.dtype)
        return f

    half = MLA_HEADS // 2
    return [trig, project, k_rope, heads(0, half), heads(half, MLA_HEADS)]


def _inproj_kernel(x_ref, g_ref, w_ref, mu_ref, posrow_ref, fcol_ref, qn_ref, kvn_ref,
                   wqt_ref, wqrt_ref, wk_ref, wvt_ref,
                   rw_ref, gt_ref, q_ref, k_ref, vt_ref, carry_ref, *, tiles_per_seq, tm):
    i = pl.program_id(0)
    u = _rms(x_ref[...], g_ref[...]).astype(BF16)

    @pl.when(i % tiles_per_seq == 0)
    def _():
        carry_ref[...] = jnp.zeros_like(carry_ref)

    def shift_store(cs, z):
        prev = pltpu.roll(z, 1, 0)
        row0 = lax.broadcasted_iota(jnp.int32, z.shape, 0) == 0
        prev = jnp.where(row0, carry_ref[7:8, cs], prev)
        carry_ref[:, cs] = z[tm - 8:tm, :]
        rw_ref[:, cs] = z + (prev - z) * mu_ref[:, cs]

    def gate_store(cs, z):
        gt_ref[:, cs] = z

    cw = 512
    jobs = [(c0, slice(c0, min(c0 + cw, RW_COLS)), shift_store) for c0 in range(0, RW_COLS, cw)]
    jobs += [(RW_COLS + ML_COLS + c0, slice(c0, c0 + cw), gate_store) for c0 in range(0, GT_COLS, cw)]
    ml = _dot(u, w_ref[:, RW_COLS:RW_COLS + ML_COLS])
    fillers = _mla_prep_parts(ml, posrow_ref, fcol_ref, qn_ref, kvn_ref, wqt_ref, wqrt_ref, wk_ref, wvt_ref,
                              q_ref, k_ref, vt_ref, tm)
    mm = lambda job: _dot(u, w_ref[:, job[0]:job[0] + (job[1].stop - job[1].start)])
    z = mm(jobs[0])
    for n, job in enumerate(jobs):
        z_next = mm(jobs[n + 1]) if n + 1 < len(jobs) else None
        if n < len(fillers):
            fillers[n]()
        job[2](job[1], z)
        z = z_next


def _inproj(x2, g, w_a, mu_a, posrow, fcol, qn, kvn, wqt, wqrt, wk, wvt, batch, seq):
    t_tok = x2.shape[0]
    tm = min(TOKEN_TILE, seq)
    nk = seq // tm
    ncol = RW_COLS + ML_COLS + GT_COLS
    kern = functools.partial(_inproj_kernel, tiles_per_seq=nk, tm=tm)
    full = lambda shape: pl.BlockSpec(shape, lambda i: (0,) * len(shape))
    head_major = lambda rows, cols: pl.BlockSpec((None, MLA_HEADS, None, rows, cols),
                                                 lambda i: (i // nk, 0, i % nk, 0, 0))
    return pl.pallas_call(
        kern,
        grid=(t_tok // tm,),
        in_specs=[
            pl.BlockSpec((tm, D_MODEL), lambda i: (i, 0)),
            full((1, D_MODEL)),
            pl.BlockSpec((D_MODEL, ncol), lambda i: (0, 0), pipeline_mode=pl.Buffered(1)),
            full((1, RW_COLS)),
            pl.BlockSpec((None, 1, tm), lambda i: (i, 0, 0)),
            full((ROPE_DIM, 1)), full((1, Q_LORA)), full((1, KV_LORA)),
            full((MLA_HEADS * HEAD_PAD, Q_LORA)), full((MLA_HEADS * ROPE_DIM, Q_LORA)),
            full((KV_LORA, MLA_HEADS * HEAD_PAD)), full((MLA_HEADS, V_DIM, KV_LORA)),
        ],
        out_specs=[
            pl.BlockSpec((tm, RW_COLS), lambda i: (i, 0)),
            pl.BlockSpec((tm, GT_COLS), lambda i: (i, 0)),
            head_major(HEAD_PAD, tm), head_major(tm, HEAD_PAD), head_major(V_EXT, tm),
        ],
        out_shape=[
            jax.ShapeDtypeStruct((t_tok, RW_COLS), F32),
            jax.ShapeDtypeStruct((t_tok, GT_COLS), F32),
            jax.ShapeDtypeStruct((batch, MLA_HEADS, nk, HEAD_PAD, tm), BF16),
            jax.ShapeDtypeStruct((batch, MLA_HEADS, nk, tm, HEAD_PAD), BF16),
            jax.ShapeDtypeStruct((batch, MLA_HEADS, nk, V_EXT, tm), BF16),
        ],
        scratch_shapes=[pltpu.VMEM((8, RW_COLS), F32)],
        compiler_params=_cparams(("arbitrary",)),
        name="inproj",
    )(x2, g, w_a, mu_a, posrow, fcol, qn, kvn, wqt, wqrt, wk, wvt)


def _split3(x):
    hi = x.astype(BF16)
    r1 = x - hi.astype(F32)
    mid = r1.astype(BF16)
    lo = (r1 - mid.astype(F32)).astype(BF16)
    return hi, mid, lo


def _rwkv_masks():
    n = 2 * CHUNK
    row = lax.broadcasted_iota(jnp.int32, (n, n), 0)
    col = lax.broadcasted_iota(jnp.int32, (n, n), 1)
    same = (row // CHUNK) == (col // CHUNK)
    strict = jnp.where(same & (row > col), 1.0, 0.0).astype(F32)
    incl = jnp.where(same & (row >= col), 1.0, 0.0).astype(F32)
    base = jnp.where((row // INV_BASE) == (col // INV_BASE), 1.0, 0.0).astype(F32)
    levels = []
    blk = INV_BASE
    while blk < CHUNK:
        levels.append((jnp.where(((row // (2 * blk)) == (col // (2 * blk))) & ((row // blk) > (col // blk)),
                                 1.0, 0.0).astype(F32), blk))
        blk *= 2
    eye = jnp.where(row == col, 1.0, 0.0).astype(F32)
    headsel = (row // CHUNK) == (col // RWKV_HEAD)
    return strict, incl, base, levels, eye, headsel


def _rows(x, blk):
    return jnp.concatenate([x[s:s + blk] for s in range(blk, x.shape[0], 2 * blk)], axis=0)


def _merge_rows(x, odd, blk):
    parts = []
    for j, s in enumerate(range(0, x.shape[0], 2 * blk)):
        parts += [x[s:s + blk], odd[j * blk:(j + 1) * blk]]
    return jnp.concatenate(parts, axis=0)


def _rwkv_wave(refs, pairs, nchunk, consts, head_sum):
    (zr_ref, zk_ref, zv_ref, w0_ref, a0_ref, kk_ref, ka_ref, rk_ref, lnw_ref, lnb_ref,
     wup_ref, aup_ref, gup_ref, y_ref, h_ref, tanh_zw, za, sig_zg) = refs
    strict, incl, base, levels, eye, headsel, tril = consts
    incl2 = jnp.concatenate([incl, incl], axis=1)
    n = 2 * CHUNK
    bf = lambda x: x.astype(BF16)
    items = [(q, slice(c * CHUNK, (c + 1) * CHUNK)) for c in range(nchunk) for q in range(len(pairs))]
    rng = range(len(items))
    c = {}

    def stack(x):
        return jnp.where(headsel, jnp.concatenate([x, x], axis=0), 0.0)

    def pre():
        c["keep"] = []
        streams = []
        for p in pairs:
            ls = slice(p * PAIR, (p + 1) * PAIR)
            zr, zk, zv = zr_ref[:, ls], zk_ref[:, ls], zv_ref[:, ls]
            w_pre = w0_ref[:, ls] + _dot(tanh_zw, wup_ref[:, ls])
            lw = -math.exp(-0.5) * _sigmoid(w_pre)
            iclr = _sigmoid(a0_ref[:, ls] + _dot(za, aup_ref[:, ls]))
            gate = _dot(sig_zg, gup_ref[:, ls])
            kk = zk * kk_ref[:, ls]
            kk = kk / jnp.maximum(jnp.sqrt(head_sum(kk * kk)), 1e-12)
            k = zk * (1.0 + (iclr - 1.0) * ka_ref[:, ls])
            streams.append((zr, k, zv, -kk, kk * iclr, lw))
            c["keep"].append((zr, k, zv, gate))
        for j, name in enumerate(("r", "k", "v", "a", "b", "lw")):
            c[name] = [streams[q][j][s] for q, s in items]

    def cumsum():
        c["cs"] = []
        for i in rng:
            hi, mid, lo = _split3(c["lw"][i])
            c["cs"].append(_dot(tril, hi) + _dot(tril, mid) + _dot(tril, lo))

    def scale():
        cs, lw = c["cs"], c["lw"]
        g_in = [jnp.exp(cs[i]) for i in rng]
        g_ex = [jnp.exp(cs[i] - lw[i]) for i in rng]
        g_inv = [jnp.exp(-cs[i]) for i in rng]
        c["g_last"] = [jnp.exp(cs[i][CHUNK - 1:CHUNK, :]) for i in rng]
        c["rt"] = [stack(c["r"][i] * g_in[i]) for i in rng]
        c["at"] = [bf(stack(c["a"][i] * g_ex[i])) for i in rng]
        bt = [stack(c["b"][i] * g_inv[i]) for i in rng]
        kt = [stack(c["k"][i] * g_inv[i]) for i in rng]
        c["bkh_t"] = [bf(jnp.concatenate([(bt[i] * c["g_last"][i]).T, (kt[i] * c["g_last"][i]).T], axis=1))
                      for i in rng]
        c["v2"] = [bf(stack(c["v"][i])) for i in rng]
        c["bk"] = [bf(jnp.concatenate([bt[i], kt[i]], axis=0)) for i in rng]

    def gram():
        ga = [_dot_nt(c["at"][i], c["bk"][i]) for i in rng]
        gr = [_dot_nt(bf(c["rt"][i]), c["bk"][i]) for i in rng]
        c["a_ab"] = [g[:, :n] * strict for g in ga]
        c["a_ak"] = [bf(g[:, n:] * strict) for g in ga]
        c["a_rbk"] = [bf(g * incl2) for g in gr]

    def inv0():
        c["d"] = [bf(a * base) for a in c["a_ab"]]
        c["t"] = [eye + a * base for a in c["a_ab"]]
        c["p"] = [_dot(d, d) for d in c["d"]]

    def inv_double():
        xs = [_dot(bf(jnp.concatenate([t, p], axis=0)), bf(p)) for t, p in zip(c["t"], c["p"])]
        c["t"] = [t + x[:n] for t, x in zip(c["t"], xs)]
        c["p"] = [x[n:] for x in xs]

    def inv_last():
        c["t"] = [t + _dot(bf(t), bf(p)) for t, p in zip(c["t"], c["p"])]

    def level_a(msk, blk):
        def f():
            c["tb"] = [bf(t) for t in c["t"]]
            c["x"] = [bf(_dot(bf(_rows(t, blk)), bf(a * msk))) for t, a in zip(c["t"], c["a_ab"])]
        return f

    def level_b(blk):
        def f():
            c["t"] = [_merge_rows(t, _rows(t, blk) + _dot(x, tb), blk)
                      for t, x, tb in zip(c["t"], c["x"], c["tb"])]
        return f

    def apply_v():
        c["av"] = [bf(_dot(c["a_ak"][i], c["v2"][i])) for i in rng]

    def apply_t():
        c["wu"] = [bf(_dot(bf(c["t"][i]), jnp.concatenate([c["at"][i], c["av"][i]], axis=1)))
                   for i in rng]

    def assemble():
        zero = jnp.zeros((n, n), BF16)
        big = [_dot(jnp.concatenate([c["a_rbk"][i], c["bkh_t"][i]], axis=0),
                    jnp.concatenate([c["wu"][i], jnp.concatenate([zero, c["v2"][i]], axis=1)], axis=0))
               for i in rng]
        qeff = [c["rt"][i] + big[i][:n, :n] for i in rng]
        c["y0"] = [big[i][:n, n:] for i in rng]
        m = [eye * c["g_last"][i] + big[i][n:, :n] for i in rng]
        c["nn"] = [big[i][n:, n:] for i in rng]
        c["mq"] = [bf(jnp.concatenate([m[i], qeff[i]], axis=0)) for i in rng]

    def chain():
        hs = [h_ref[p] for p in pairs]
        ys = [[] for _ in pairs]
        for i in rng:
            q = items[i][0]
            hy = _dot(c["mq"][i], bf(hs[q]))
            hs[q] = hy[:n] + c["nn"][i]
            y2 = hy[n:] + c["y0"][i]
            ys[q].append(y2[:CHUNK, :] + y2[CHUNK:, :])
        for q, p in enumerate(pairs):
            h_ref[p] = hs[q]
        c["ys"] = [jnp.concatenate(y, axis=0) for y in ys]

    def post():
        for q, p in enumerate(pairs):
            ls = slice(p * PAIR, (p + 1) * PAIR)
            zr, k, zv, gate = c["keep"][q]
            y = c["ys"][q]
            mean = head_sum(y) * (1.0 / RWKV_HEAD)
            yc = y - mean
            var = head_sum(yc * yc) * (1.0 / RWKV_HEAD)
            yn = yc * lax.rsqrt(var + GN_EPS) * lnw_ref[:, ls] + lnb_ref[:, ls]
            bonus = head_sum(zr * k * rk_ref[:, ls]) * zv
            y_ref[:, ls] = ((yn + bonus) * gate).astype(y_ref.dtype)

    doublings = [inv_double] * (INV_BASE.bit_length() - 3)
    level_stages = [f for msk, blk in levels for f in (level_a(msk, blk), level_b(blk))]
    return ([pre, cumsum, scale, gram, inv0] + doublings + [inv_last] + level_stages
            + [apply_v, apply_t, assemble, chain, post])


def _rwkv_kernel(zr_ref, zk_ref, zv_ref, zwa_ref, zg_ref,
                 w0_ref, a0_ref, kk_ref, ka_ref, rk_ref, lnw_ref, lnb_ref,
                 wup_ref, aup_ref, gup_ref, y_ref, h_ref, *, tb, npair):
    @pl.when(pl.program_id(2) == 0)
    def _():
        h_ref[...] = jnp.zeros_like(h_ref)

    lane = lax.broadcasted_iota(jnp.int32, (tb, PAIR), 1)
    head0 = lane < RWKV_HEAD

    def head_sum(x):
        s0 = jnp.sum(jnp.where(head0, x, 0.0), axis=-1, keepdims=True)
        s1 = jnp.sum(jnp.where(head0, 0.0, x), axis=-1, keepdims=True)
        return jnp.where(head0, s0, s1)

    trow = lax.broadcasted_iota(jnp.int32, (CHUNK, CHUNK), 0)
    tcol = lax.broadcasted_iota(jnp.int32, (CHUNK, CHUNK), 1)
    tril = jnp.where(trow >= tcol, 1.0, 0.0).astype(BF16)
    consts = _rwkv_masks() + (tril,)
    refs = (zr_ref, zk_ref, zv_ref, w0_ref, a0_ref, kk_ref, ka_ref, rk_ref, lnw_ref, lnb_ref,
            wup_ref, aup_ref, gup_ref, y_ref, h_ref,
            jnp.tanh(zwa_ref[...]).astype(BF16), zwa_ref[...].astype(BF16),
            _sigmoid(zg_ref[...]).astype(BF16))
    for stage in _rwkv_wave(refs, list(range(npair)), tb // CHUNK, consts, head_sum):
        stage()


def _rwkv(rw, prm, wup, aup, gup, batch, seq):
    t_tok = rw.shape[0]
    tb = min(RWKV_TILE, seq)
    nt = seq // tb
    npair = RWKV_PAIRS
    wid = npair * PAIR
    ngrp = RWKV_DIM // wid

    def tok(base):
        return pl.BlockSpec((tb, wid), lambda b, p, t: (b * nt + t, base * ngrp + p))

    def lora(base, width):
        return pl.BlockSpec((tb, width), lambda b, p, t: (b * nt + t, base))

    prm_spec = pl.BlockSpec((1, wid), lambda b, p, t: (0, p))
    in_specs = [
        tok(0), tok(1), tok(2),
        lora((3 * RWKV_DIM + ZG_PAD) // 128, 128), lora(3 * RWKV_DIM // ZG_PAD, ZG_PAD),
    ] + [prm_spec] * 7 + [
        pl.BlockSpec((128, wid), lambda b, p, t: (0, p)),
        pl.BlockSpec((128, wid), lambda b, p, t: (0, p)),
        pl.BlockSpec((ZG_PAD, wid), lambda b, p, t: (0, p)),
    ]
    return pl.pallas_call(
        functools.partial(_rwkv_kernel, tb=tb, npair=npair),
        grid=(batch, ngrp, nt),
        in_specs=in_specs,
        out_specs=pl.BlockSpec((tb, wid), lambda b, p, t: (b * nt + t, p)),
        out_shape=jax.ShapeDtypeStruct((t_tok, RWKV_DIM), BF16),
        scratch_shapes=[pltpu.VMEM((npair, PAIR, PAIR), F32)],
        compiler_params=_cparams(("arbitrary", "arbitrary", "arbitrary")),
        name="rwkv7",
    )(rw, rw, rw, rw, rw, *prm, wup, aup, gup)


def _attn_kernel(q_ref, k_ref, vt_ref, o_ref, m_ref, acc_ref, s_ref, p_ref, al_ref,
                 *, tk, heads, ks, ws, ahead, dlag, behind):
    i = pl.program_id(2)
    krow = lax.broadcasted_iota(jnp.int32, (ks, ws), 0)
    qcol = lax.broadcasted_iota(jnp.int32, (ks, ws), 1)
    neg = jnp.finfo(F32).min
    nsub, nstrip = tk // ks, tk // ws
    items = [(h, g) for h in range(heads) for g in range(nstrip)]
    rng = range(len(items))
    qs = [q_ref[h, 0, :, g * ws:(g + 1) * ws] for h, g in items]

    full_units = [(sub, i) for sub in range(nsub) for i in rng]
    diag_units = [(sub, i) for sub, i in full_units if not sub * ks > items[i][1] * ws + ws - 1]
    lag = dlag + behind
    tail_units = full_units[len(full_units) - lag:]
    assert diag_units[:ahead] == full_units[:ahead] and len(diag_units) > ahead + lag

    def score(j, unit):
        sub, i = unit
        return _dot(k_ref[items[i][0], j, sub * ks:(sub + 1) * ks, :], qs[i])

    def pv(j, unit, p):
        sub, i = unit
        return _dot(vt_ref[items[i][0], j, :, sub * ks:(sub + 1) * ks], p)

    def step(j, masked):
        units = diag_units if masked else full_units
        nu = len(units)
        keep = 0 if masked else lag
        jprev = jnp.maximum(j - 1, 0)
        s_val, p_val, pv_val, al_val, pv_old = {}, {}, {}, {}, {}
        for t in range(nu + (lag if masked else 0)):
            if t + ahead < nu:
                s_val[t + ahead] = score(j, units[t + ahead])
            if t < nu:
                sub, i = units[t]
                s = s_val.pop(t) if t >= ahead else s_ref[t]
                if masked and sub * ks + ks - 1 > items[i][1] * ws:
                    s = jnp.where(krow + (sub * ks - items[i][1] * ws) <= qcol, s, neg)
                m_old = m_ref[i]
                m_new = jnp.maximum(m_old, jnp.max(s, axis=0, keepdims=True))
                al_val[t] = jnp.exp2(m_old - m_new)
                m_ref[i] = m_new
                p_val[t] = jnp.exp2(s - m_new).astype(BF16)
                if not masked and t + ahead >= nu:
                    s_ref[t + ahead - nu] = score(j + 1, full_units[t + ahead - nu])
            if t < lag:
                pv_old[t] = pv(jprev, tail_units[t], p_ref[t])
            d = t - dlag
            if 0 <= d < nu - keep:
                pv_val[d] = pv(j, units[d], p_val.pop(d))
            x = t - behind
            if 0 <= x < lag:
                i = tail_units[x][1]
                acc_ref[i] = al_ref[x] * acc_ref[i] + pv_old.pop(x)
            w = d - behind
            if 0 <= w < nu - keep:
                i = units[w][1]
                acc_ref[i] = al_val.pop(w) * acc_ref[i] + pv_val.pop(w)
        if not masked:
            for x in range(lag):
                p_ref[x] = p_val[nu - lag + x]
                al_ref[x] = al_val[nu - lag + x]

    m_ref[...] = jnp.full(m_ref.shape, neg, F32)
    acc_ref[...] = jnp.zeros(acc_ref.shape, F32)
    p_ref[...] = jnp.zeros(p_ref.shape, BF16)
    al_ref[...] = jnp.ones(al_ref.shape, F32)
    for t in range(ahead):
        s_ref[t] = score(0, full_units[t])

    def body(j, c):
        step(j, False)
        return c

    lax.fori_loop(0, i, body, 0)
    step(i, True)
    outs = []
    for h in range(heads):
        accs = [acc_ref[h * nstrip + g] for g in range(nstrip)]
        acc = jnp.concatenate([a[:V_DIM] / a[V_DIM:V_DIM + 1] for a in accs], axis=1)
        outs.append(acc.T)
    o_ref[...] = jnp.concatenate(outs, axis=-1).astype(o_ref.dtype)


def _attn(q5, k5, vt5, batch, seq, tk):
    nk = seq // tk
    heads = 8
    ngrp = MLA_HEADS // heads
    ks, ws = min(256, tk), min(256, tk)
    ahead, dlag, behind = 3, 2, 2
    nitem = heads * (tk // ws)
    return pl.pallas_call(
        functools.partial(_attn_kernel, tk=tk, heads=heads, ks=ks, ws=ws, ahead=ahead, dlag=dlag, behind=behind),
        grid=(batch, ngrp, nk),
        in_specs=[
            pl.BlockSpec((None, heads, 1, HEAD_PAD, tk), lambda b, g, i: (b, g, i, 0, 0)),
            pl.BlockSpec((None, heads, nk, tk, HEAD_PAD), lambda b, g, i: (b, g, 0, 0, 0)),
            pl.BlockSpec((None, heads, nk, V_EXT, tk), lambda b, g, i: (b, g, 0, 0, 0)),
        ],
        out_specs=pl.BlockSpec((tk, heads * V_DIM), lambda b, g, i: (b * nk + i, g)),
        out_shape=jax.ShapeDtypeStruct((batch * seq, MLA_HEADS * V_DIM), BF16),
        scratch_shapes=[
            pltpu.VMEM((nitem, 1, ws), F32),
            pltpu.VMEM((nitem, V_EXT, ws), F32),
            pltpu.VMEM((ahead, ks, ws), F32),
            pltpu.VMEM((dlag + behind, ks, ws), BF16),
            pltpu.VMEM((dlag + behind, 1, ws), F32),
        ],
        compiler_params=_cparams(("arbitrary", "arbitrary", "arbitrary")),
        name="mla_attn",
    )(q5, k5, vt5)


def _tail_kernel(x_ref, yr_ref, o_ref, gt_ref, wor_ref, wom_ref, wout_ref, nf_ref, wup_ref, wdn_ref,
                 nfin_ref, out_ref):
    y_a = _dot(yr_ref[...], wor_ref[...])
    y_b = _dot(o_ref[...], wom_ref[...])
    merged = _sigmoid(gt_ref[:, :D_MODEL]) * y_a + _sigmoid(gt_ref[:, D_MODEL:]) * y_b
    h = x_ref[...] + _dot(merged.astype(BF16), wout_ref[...])
    f_in = _rms(h, nf_ref[...]).astype(BF16)
    cw = 1024
    acc = h
    for c in range(D_FF // cw):
        f = _dot(f_in, wup_ref[:, c * cw:(c + 1) * cw])
        f = jnp.square(jnp.maximum(f, 0.0)).astype(BF16)
        acc = acc + _dot(f, wdn_ref[c * cw:(c + 1) * cw, :])
    out_ref[...] = _rms(acc, nfin_ref[...])


def _tail(x2, yr, o, gt, wor, wom, wout, nf, wup, wdn, nfin, seq):
    t_tok = x2.shape[0]
    tm = min(TOKEN_TILE, seq)
    full = lambda shape: pl.BlockSpec(shape, lambda i: (0,) * len(shape), pipeline_mode=pl.Buffered(1))
    return pl.pallas_call(
        _tail_kernel,
        grid=(t_tok // tm,),
        in_specs=[
            pl.BlockSpec((tm, D_MODEL), lambda i: (i, 0)),
            pl.BlockSpec((tm, RWKV_DIM), lambda i: (i, 0)),
            pl.BlockSpec((tm, MLA_HEADS * V_DIM), lambda i: (i, 0)),
            pl.BlockSpec((tm, GT_COLS), lambda i: (i, 0)),
            full((RWKV_DIM, D_MODEL)), full((MLA_HEADS * V_DIM, D_MODEL)), full((D_MODEL, D_MODEL)),
            full((1, D_MODEL)), full((D_MODEL, D_FF)), full((D_FF, D_MODEL)), full((1, D_MODEL)),
        ],
        out_specs=pl.BlockSpec((tm, D_MODEL), lambda i: (i, 0)),
        out_shape=jax.ShapeDtypeStruct((t_tok, D_MODEL), F32),
        compiler_params=_cparams(("arbitrary",)),
        name="merge_ffn",
    )(x2, yr, o, gt, wor, wom, wout, nf, wup, wdn, nfin)


def _padc(w, n):
    return jnp.pad(w, ((0, 0), (0, n - w.shape[1])))


def _padr(w, n):
    return jnp.pad(w, ((0, n - w.shape[0]), (0, 0)))


def _rot_half(w):
    half = w.shape[-1] // 2
    return jnp.concatenate([-w[..., half:], w[..., :half]], axis=-1)


def kernel(x, positions, norm_mix, w_in, mu_shift, w0, w_up, a0, a_up, g_up, k_k, k_a, r_k, ln_w, ln_b, w_o_rwkv, q_norm, w_uq, kv_norm, w_ukv, w_o_mla, w_out, norm_ffn, w_ff_up, w_ff_down, norm_final):
    batch, seq, _ = x.shape
    t_tok = batch * seq
    x2 = x.reshape(t_tok, D_MODEL)
    l = 0

    wi = w_in[l].astype(BF16)
    o = 0
    w_r3 = wi[:, o:o + 3 * RWKV_DIM]; o += 3 * RWKV_DIM
    w_zw = wi[:, o:o + DECAY_LORA]; o += DECAY_LORA
    w_za = wi[:, o:o + ICLR_LORA]; o += ICLR_LORA
    w_zg = wi[:, o:o + GATE_LORA]; o += GATE_LORA
    w_cq = wi[:, o:o + Q_LORA]; o += Q_LORA
    w_ckv = wi[:, o:o + KV_LORA]; o += KV_LORA
    w_kr = wi[:, o:o + ROPE_DIM]; o += ROPE_DIM
    w_gate = wi[:, o:o + 2 * D_MODEL]
    zeros64 = jnp.zeros((D_MODEL, NOPE_DIM), wi.dtype)
    w_krb = jnp.concatenate([zeros64, w_kr, _rot_half(w_kr)], axis=1)
    w_a = jnp.concatenate([
        w_r3, _padc(w_zg, ZG_PAD), w_zw, w_za, w_cq, w_ckv, w_krb, w_gate], axis=1)
    mu = mu_shift[l]
    o = 3 * RWKV_DIM
    lo = o + DECAY_LORA + ICLR_LORA
    mu_a = jnp.concatenate([mu[:o], jnp.pad(mu[lo:], (0, ZG_PAD - GATE_LORA)), mu[o:lo]])[None, :]

    half = ROPE_DIM // 2
    inv_freq = 1.0 / (ROPE_THETA ** (jnp.arange(half, dtype=F32) * (2.0 / ROPE_DIM)))
    scale = (NOPE_DIM + ROPE_DIM) ** -0.5 * math.log2(math.e)
    wq = w_uq[l].reshape(Q_LORA, MLA_HEADS, NOPE_DIM + ROPE_DIM) * scale
    wq_p = jnp.pad(wq, ((0, 0), (0, 0), (0, HEAD_PAD - NOPE_DIM - ROPE_DIM)))
    wqt = jnp.transpose(wq_p, (1, 2, 0)).reshape(MLA_HEADS * HEAD_PAD, Q_LORA)
    wqrt = jnp.transpose(_rot_half(wq[..., NOPE_DIM:]), (1, 2, 0)).reshape(MLA_HEADS * ROPE_DIM, Q_LORA)
    fcol = jnp.concatenate([inv_freq, inv_freq])[:, None]
    wkv = w_ukv[l].reshape(KV_LORA, MLA_HEADS, NOPE_DIM + V_DIM)
    wk_p = jnp.pad(wkv[..., :NOPE_DIM], ((0, 0), (0, 0), (0, HEAD_PAD - NOPE_DIM)))
    wvt = jnp.transpose(wkv[..., NOPE_DIM:], (1, 2, 0))
    tk = min(TOKEN_TILE, seq)

    rw, gt, q5, k5, vt5 = _inproj(
        x2, norm_mix[l][None, :], w_a, mu_a, positions.reshape(t_tok // tk, 1, tk), fcol,
        q_norm[l][None, :], kv_norm[l][None, :], wqt.astype(BF16), wqrt.astype(BF16),
        wk_p.reshape(KV_LORA, -1).astype(BF16), wvt.astype(BF16), batch, seq)

    prm = [p[None, :] for p in (w0[l], a0[l], k_k[l], k_a[l], r_k[l].reshape(-1), ln_w[l], ln_b[l])]
    aup = jnp.concatenate([jnp.zeros_like(a_up[l]), a_up[l]], axis=0)
    yr = _rwkv(rw, prm, _padr(w_up[l], 128).astype(BF16), aup.astype(BF16),
               _padr(g_up[l], ZG_PAD).astype(BF16), batch, seq)

    o_att = _attn(q5, k5, vt5, batch, seq, tk)

    out = _tail(x2, yr, o_att, gt, w_o_rwkv[l].astype(BF16), w_o_mla[l].astype(BF16),
                w_out[l].astype(BF16), norm_ffn[l][None, :], w_ff_up[l].astype(BF16),
                w_ff_down[l].astype(BF16), norm_final[None, :], seq)
    return out.reshape(batch, seq, D_MODEL)
```

```python
import functools
import math

import jax
import jax.numpy as jnp
from jax import lax
from jax.experimental import pallas as pl
from jax.experimental.pallas import tpu as pltpu

F32 = jnp.float32
BF16 = jnp.bfloat16

D_MODEL = 1024
NORM_EPS = 1e-6
RWKV_HEAD = 64
RWKV_HEADS = 8
RWKV_DIM = RWKV_HEADS * RWKV_HEAD
DECAY_LORA = 64
ICLR_LORA = 64
GATE_LORA = 160
GN_EPS = 64e-5
MLA_HEADS = 8
Q_LORA = 256
KV_LORA = 128
NOPE_DIM = 64
ROPE_DIM = 32
V_DIM = 64
ROPE_THETA = 10000.0
D_FF = 4 * D_MODEL

LANE = 128
CHUNK = 64
INV_BASE = 8
PAIR = 2 * RWKV_HEAD
HEAD_PAD = 128
V_EXT = V_DIM + 16

ZG_PAD = 256
RW_COLS = 3 * RWKV_DIM + ZG_PAD + 128
ML_COLS = Q_LORA + KV_LORA + 128
GT_COLS = 2 * D_MODEL

VMEM_BYTES_V7X = 64 * 1024 * 1024
VMEM_LIMIT = VMEM_BYTES_V7X * 7 // 8
TOKEN_TILE = 512
RWKV_TILE = 4 * CHUNK
RWKV_PAIRS = RWKV_DIM // PAIR


def _cparams(sem):
    return pltpu.CompilerParams(dimension_semantics=sem, vmem_limit_bytes=VMEM_LIMIT)


def _sigmoid(x):
    return 1.0 / (1.0 + jnp.exp(-x))


def _rms(x, g):
    ms = jnp.mean(x * x, axis=-1, keepdims=True)
    return x * lax.rsqrt(ms + NORM_EPS) * g


def _dot(a, b):
    return jnp.dot(a, b, preferred_element_type=F32)


def _dot_nt(a, b):
    return lax.dot_general(a, b, (((1,), (1,)), ((), ())), preferred_element_type=F32)


def _dot_tn(a, b):
    return lax.dot_general(a, b, (((0,), (0,)), ((), ())), preferred_element_type=F32)


def _mla_prep_parts(ml, posrow_ref, fcol_ref, qn_ref, kvn_ref, wqt_ref, wqrt_ref, wk_ref, wvt_ref,
                    q_ref, k_ref, vt_ref, tk):
    c = {}
    rope = slice(NOPE_DIM, NOPE_DIM + ROPE_DIM)

    def trig():
        ang_t = fcol_ref[...] * posrow_ref[...].astype(F32)
        c["cos_t"] = jnp.cos(ang_t)
        c["sin_t"] = jnp.sin(ang_t)
        c["cq"] = _rms(ml[:, :Q_LORA], qn_ref[...]).astype(BF16)
        c["ckv"] = _rms(ml[:, Q_LORA:Q_LORA + KV_LORA], kvn_ref[...]).astype(BF16)

    def project():
        c["qf"] = _dot_nt(wqt_ref[...], c["cq"])
        c["qr"] = _dot_nt(wqrt_ref[...], c["cq"])
        c["kf"] = _dot(c["ckv"], wk_ref[...])

    def k_rope():
        zlo = jnp.zeros((NOPE_DIM, tk), F32)
        zhi = jnp.zeros((HEAD_PAD - NOPE_DIM - ROPE_DIM, tk), F32)
        cosf = jnp.concatenate([zlo, c["cos_t"], zhi], axis=0).T
        sinf = jnp.concatenate([zlo, c["sin_t"], zhi], axis=0).T
        krb = ml[:, Q_LORA + KV_LORA:Q_LORA + KV_LORA + HEAD_PAD]
        c["k_rope"] = krb * cosf + pltpu.roll(krb, HEAD_PAD - ROPE_DIM, 1) * sinf

    def heads(lo, hi):
        def f():
            for h in range(lo, hi):
                hs = slice(h * HEAD_PAD, (h + 1) * HEAD_PAD)
                qf_t = c["qf"][hs]
                qr_t = c["qr"][h * ROPE_DIM:(h + 1) * ROPE_DIM]
                q_t = jnp.concatenate([qf_t[:NOPE_DIM], qf_t[rope] * c["cos_t"] + qr_t * c["sin_t"],
                                       qf_t[NOPE_DIM + ROPE_DIM:]], axis=0)
                q_ref[h] = q_t.astype(q_ref.dtype)
                k_ref[h] = (c["kf"][:, hs] + c["k_rope"]).astype(k_ref.dtype)
                vt_ref[h, :V_DIM] = _dot_nt(wvt_ref[h], c["ckv"]).astype(vt_ref.dtype)
                vt_ref[h, V_DIM:] = jnp.ones((V_EXT - V_DIM, tk), vt_ref.dtype)
        return f

    half = MLA_HEADS // 2
    return [trig, project, k_rope, heads(0, half), heads(half, MLA_HEADS)]


def _inproj_kernel(x_ref, g_ref, w_ref, mu_ref, posrow_ref, fcol_ref, qn_ref, kvn_ref,
                   wqt_ref, wqrt_ref, wk_ref, wvt_ref,
                   rw_ref, gt_ref, q_ref, k_ref, vt_ref, carry_ref, *, tiles_per_seq, tm):
    i = pl.program_id(0)
    u = _rms(x_ref[...], g_ref[...]).astype(BF16)

    @pl.when(i % tiles_per_seq == 0)
    def _():
        carry_ref[...] = jnp.zeros_like(carry_ref)

    def shift_store(cs, z):
        prev = pltpu.roll(z, 1, 0)
        row0 = lax.broadcasted_iota(jnp.int32, z.shape, 0) == 0
        prev = jnp.where(row0, carry_ref[7:8, cs], prev)
        carry_ref[:, cs] = z[tm - 8:tm, :]
        rw_ref[:, cs] = z + (prev - z) * mu_ref[:, cs]

    def gate_store(cs, z):
        gt_ref[:, cs] = z

    cw = 512
    jobs = [(c0, slice(c0, min(c0 + cw, RW_COLS)), shift_store) for c0 in range(0, RW_COLS, cw)]
    jobs += [(RW_COLS + ML_COLS + c0, slice(c0, c0 + cw), gate_store) for c0 in range(0, GT_COLS, cw)]
    ml = _dot(u, w_ref[:, RW_COLS:RW_COLS + ML_COLS])
    fillers = _mla_prep_parts(ml, posrow_ref, fcol_ref, qn_ref, kvn_ref, wqt_ref, wqrt_ref, wk_ref, wvt_ref,
                              q_ref, k_ref, vt_ref, tm)
    mm = lambda job: _dot(u, w_ref[:, job[0]:job[0] + (job[1].stop - job[1].start)])
    z = mm(jobs[0])
    for n, job in enumerate(jobs):
        z_next = mm(jobs[n + 1]) if n + 1 < len(jobs) else None
        if n < len(fillers):
            fillers[n]()
        job[2](job[1], z)
        z = z_next


def _inproj(x2, g, w_a, mu_a, posrow, fcol, qn, kvn, wqt, wqrt, wk, wvt, batch, seq):
    t_tok = x2.shape[0]
    tm = min(TOKEN_TILE, seq)
    nk = seq // tm
    ncol = RW_COLS + ML_COLS + GT_COLS
    kern = functools.partial(_inproj_kernel, tiles_per_seq=nk, tm=tm)
    full = lambda shape: pl.BlockSpec(shape, lambda i: (0,) * len(shape))
    head_major = lambda rows, cols: pl.BlockSpec((None, MLA_HEADS, None, rows, cols),
                                                 lambda i: (i // nk, 0, i % nk, 0, 0))
    return pl.pallas_call(
        kern,
        grid=(t_tok // tm,),
        in_specs=[
            pl.BlockSpec((tm, D_MODEL), lambda i: (i, 0)),
            full((1, D_MODEL)),
            pl.BlockSpec((D_MODEL, ncol), lambda i: (0, 0), pipeline_mode=pl.Buffered(1)),
            full((1, RW_COLS)),
            pl.BlockSpec((None, 1, tm), lambda i: (i, 0, 0)),
            full((ROPE_DIM, 1)), full((1, Q_LORA)), full((1, KV_LORA)),
            full((MLA_HEADS * HEAD_PAD, Q_LORA)), full((MLA_HEADS * ROPE_DIM, Q_LORA)),
            full((KV_LORA, MLA_HEADS * HEAD_PAD)), full((MLA_HEADS, V_DIM, KV_LORA)),
        ],
        out_specs=[
            pl.BlockSpec((tm, RW_COLS), lambda i: (i, 0)),
            pl.BlockSpec((tm, GT_COLS), lambda i: (i, 0)),
            head_major(HEAD_PAD, tm), head_major(tm, HEAD_PAD), head_major(V_EXT, tm),
        ],
        out_shape=[
            jax.ShapeDtypeStruct((t_tok, RW_COLS), F32),
            jax.ShapeDtypeStruct((t_tok, GT_COLS), F32),
            jax.ShapeDtypeStruct((batch, MLA_HEADS, nk, HEAD_PAD, tm), BF16),
            jax.ShapeDtypeStruct((batch, MLA_HEADS, nk, tm, HEAD_PAD), BF16),
            jax.ShapeDtypeStruct((batch, MLA_HEADS, nk, V_EXT, tm), BF16),
        ],
        scratch_shapes=[pltpu.VMEM((8, RW_COLS), F32)],
        compiler_params=_cparams(("arbitrary",)),
        name="inproj",
    )(x2, g, w_a, mu_a, posrow, fcol, qn, kvn, wqt, wqrt, wk, wvt)


def _split3(x):
    hi = x.astype(BF16)
    r1 = x - hi.astype(F32)
    mid = r1.astype(BF16)
    lo = (r1 - mid.astype(F32)).astype(BF16)
    return hi, mid, lo


def _rwkv_masks():
    n = 2 * CHUNK
    row = lax.broadcasted_iota(jnp.int32, (n, n), 0)
    col = lax.broadcasted_iota(jnp.int32, (n, n), 1)
    same = (row // CHUNK) == (col // CHUNK)
    strict = jnp.where(same & (row > col), 1.0, 0.0).astype(F32)
    incl = jnp.where(same & (row >= col), 1.0, 0.0).astype(F32)
    base = jnp.where((row // INV_BASE) == (col // INV_BASE), 1.0, 0.0).astype(F32)
    levels = []
    blk = INV_BASE
    while blk < CHUNK:
        levels.append((jnp.where(((row // (2 * blk)) == (col // (2 * blk))) & ((row // blk) > (col // blk)),
                                 1.0, 0.0).astype(F32), blk))
        blk *= 2
    eye = jnp.where(row == col, 1.0, 0.0).astype(F32)
    headsel = (row // CHUNK) == (col // RWKV_HEAD)
    return strict, incl, base, levels, eye, headsel


def _rows(x, blk):
    return jnp.concatenate([x[s:s + blk] for s in range(blk, x.shape[0], 2 * blk)], axis=0)


def _merge_rows(x, odd, blk):
    parts = []
    for j, s in enumerate(range(0, x.shape[0], 2 * blk)):
        parts += [x[s:s + blk], odd[j * blk:(j + 1) * blk]]
    return jnp.concatenate(parts, axis=0)


def _rwkv_wave(refs, pairs, nchunk, consts, head_sum):
    (zr_ref, zk_ref, zv_ref, w0_ref, a0_ref, kk_ref, ka_ref, rk_ref, lnw_ref, lnb_ref,
     wup_ref, aup_ref, gup_ref, y_ref, h_ref, tanh_zw, za, sig_zg) = refs
    strict, incl, base, levels, eye, headsel, tril = consts
    incl2 = jnp.concatenate([incl, incl], axis=1)
    n = 2 * CHUNK
    bf = lambda x: x.astype(BF16)
    items = [(q, slice(c * CHUNK, (c + 1) * CHUNK)) for c in range(nchunk) for q in range(len(pairs))]
    rng = range(len(items))
    c = {}

    def stack(x):
        return jnp.where(headsel, jnp.concatenate([x, x], axis=0), 0.0)

    def pre():
        c["keep"] = []
        streams = []
        for p in pairs:
            ls = slice(p * PAIR, (p + 1) * PAIR)
            zr, zk, zv = zr_ref[:, ls], zk_ref[:, ls], zv_ref[:, ls]
            w_pre = w0_ref[:, ls] + _dot(tanh_zw, wup_ref[:, ls])
            lw = -math.exp(-0.5) * _sigmoid(w_pre)
            iclr = _sigmoid(a0_ref[:, ls] + _dot(za, aup_ref[:, ls]))
            gate = _dot(sig_zg, gup_ref[:, ls])
            kk = zk * kk_ref[:, ls]
            kk = kk / jnp.maximum(jnp.sqrt(head_sum(kk * kk)), 1e-12)
            k = zk * (1.0 + (iclr - 1.0) * ka_ref[:, ls])
            streams.append((zr, k, zv, -kk, kk * iclr, lw))
            c["keep"].append((zr, k, zv, gate))
        for j, name in enumerate(("r", "k", "v", "a", "b", "lw")):
            c[name] = [streams[q][j][s] for q, s in items]

    def cumsum():
        c["cs"] = []
        for i in rng:
            hi, mid, lo = _split3(c["lw"][i])
            c["cs"].append(_dot(tril, hi) + _dot(tril, mid) + _dot(tril, lo))

    def scale():
        cs, lw = c["cs"], c["lw"]
        g_in = [jnp.exp(cs[i]) for i in rng]
        g_ex = [jnp.exp(cs[i] - lw[i]) for i in rng]
        g_inv = [jnp.exp(-cs[i]) for i in rng]
        c["g_last"] = [jnp.exp(cs[i][CHUNK - 1:CHUNK, :]) for i in rng]
        c["rt"] = [stack(c["r"][i] * g_in[i]) for i in rng]
        c["at"] = [bf(stack(c["a"][i] * g_ex[i])) for i in rng]
        bt = [stack(c["b"][i] * g_inv[i]) for i in rng]
        kt = [stack(c["k"][i] * g_inv[i]) for i in rng]
        c["bkh_t"] = [bf(jnp.concatenate([(bt[i] * c["g_last"][i]).T, (kt[i] * c["g_last"][i]).T], axis=1))
                      for i in rng]
        c["v2"] = [bf(stack(c["v"][i])) for i in rng]
        c["bk"] = [bf(jnp.concatenate([bt[i], kt[i]], axis=0)) for i in rng]

    def gram():
        ga = [_dot_nt(c["at"][i], c["bk"][i]) for i in rng]
        gr = [_dot_nt(bf(c["rt"][i]), c["bk"][i]) for i in rng]
        c["a_ab"] = [g[:, :n] * strict for g in ga]
        c["a_ak"] = [bf(g[:, n:] * strict) for g in ga]
        c["a_rbk"] = [bf(g * incl2) for g in gr]

    def inv0():
        c["d"] = [bf(a * base) for a in c["a_ab"]]
        c["t"] = [eye + a * base for a in c["a_ab"]]
        c["p"] = [_dot(d, d) for d in c["d"]]

    def inv_double():
        xs = [_dot(bf(jnp.concatenate([t, p], axis=0)), bf(p)) for t, p in zip(c["t"], c["p"])]
        c["t"] = [t + x[:n] for t, x in zip(c["t"], xs)]
        c["p"] = [x[n:] for x in xs]

    def inv_last():
        c["t"] = [t + _dot(bf(t), bf(p)) for t, p in zip(c["t"], c["p"])]

    def level_a(msk, blk):
        def f():
            c["tb"] = [bf(t) for t in c["t"]]
            c["x"] = [bf(_dot(bf(_rows(t, blk)), bf(a * msk))) for t, a in zip(c["t"], c["a_ab"])]
        return f

    def level_b(blk):
        def f():
            c["t"] = [_merge_rows(t, _rows(t, blk) + _dot(x, tb), blk)
                      for t, x, tb in zip(c["t"], c["x"], c["tb"])]
        return f

    def apply_v():
        c["av"] = [bf(_dot(c["a_ak"][i], c["v2"][i])) for i in rng]

    def apply_t():
        c["wu"] = [bf(_dot(bf(c["t"][i]), jnp.concatenate([c["at"][i], c["av"][i]], axis=1)))
                   for i in rng]

    def assemble():
        zero = jnp.zeros((n, n), BF16)
        big = [_dot(jnp.concatenate([c["a_rbk"][i], c["bkh_t"][i]], axis=0),
                    jnp.concatenate([c["wu"][i], jnp.concatenate([zero, c["v2"][i]], axis=1)], axis=0))
               for i in rng]
        qeff = [c["rt"][i] + big[i][:n, :n] for i in rng]
        c["y0"] = [big[i][:n, n:] for i in rng]
        m = [eye * c["g_last"][i] + big[i][n:, :n] for i in rng]
        c["nn"] = [big[i][n:, n:] for i in rng]
        c["mq"] = [bf(jnp.concatenate([m[i], qeff[i]], axis=0)) for i in rng]

    def chain():
        hs = [h_ref[p] for p in pairs]
        ys = [[] for _ in pairs]
        for i in rng:
            q = items[i][0]
            hy = _dot(c["mq"][i], bf(hs[q]))
            hs[q] = hy[:n] + c["nn"][i]
            y2 = hy[n:] + c["y0"][i]
            ys[q].append(y2[:CHUNK, :] + y2[CHUNK:, :])
        for q, p in enumerate(pairs):
            h_ref[p] = hs[q]
        c["ys"] = [jnp.concatenate(y, axis=0) for y in ys]

    def post():
        for q, p in enumerate(pairs):
            ls = slice(p * PAIR, (p + 1) * PAIR)
            zr, k, zv, gate = c["keep"][q]
            y = c["ys"][q]
            mean = head_sum(y) * (1.0 / RWKV_HEAD)
            yc = y - mean
            var = head_sum(yc * yc) * (1.0 / RWKV_HEAD)
            yn = yc * lax.rsqrt(var + GN_EPS) * lnw_ref[:, ls] + lnb_ref[:, ls]
            bonus = head_sum(zr * k * rk_ref[:, ls]) * zv
            y_ref[:, ls] = ((yn + bonus) * gate).astype(y_ref.dtype)

    doublings = [inv_double] * (INV_BASE.bit_length() - 3)
    level_stages = [f for msk, blk in levels for f in (level_a(msk, blk), level_b(blk))]
    return ([pre, cumsum, scale, gram, inv0] + doublings + [inv_last] + level_stages
            + [apply_v, apply_t, assemble, chain, post])


def _rwkv_kernel(zr_ref, zk_ref, zv_ref, zwa_ref, zg_ref,
                 w0_ref, a0_ref, kk_ref, ka_ref, rk_ref, lnw_ref, lnb_ref,
                 wup_ref, aup_ref, gup_ref, y_ref, h_ref, *, tb, npair):
    @pl.when(pl.program_id(2) == 0)
    def _():
        h_ref[...] = jnp.zeros_like(h_ref)

    lane = lax.broadcasted_iota(jnp.int32, (tb, PAIR), 1)
    head0 = lane < RWKV_HEAD

    def head_sum(x):
        s0 = jnp.sum(jnp.where(head0, x, 0.0), axis=-1, keepdims=True)
        s1 = jnp.sum(jnp.where(head0, 0.0, x), axis=-1, keepdims=True)
        return jnp.where(head0, s0, s1)

    trow = lax.broadcasted_iota(jnp.int32, (CHUNK, CHUNK), 0)
    tcol = lax.broadcasted_iota(jnp.int32, (CHUNK, CHUNK), 1)
    tril = jnp.where(trow >= tcol, 1.0, 0.0).astype(BF16)
    consts = _rwkv_masks() + (tril,)
    refs = (zr_ref, zk_ref, zv_ref, w0_ref, a0_ref, kk_ref, ka_ref, rk_ref, lnw_ref, lnb_ref,
            wup_ref, aup_ref, gup_ref, y_ref, h_ref,
            jnp.tanh(zwa_ref[...]).astype(BF16), zwa_ref[...].astype(BF16),
            _sigmoid(zg_ref[...]).astype(BF16))
    for stage in _rwkv_wave(refs, list(range(npair)), tb // CHUNK, consts, head_sum):
        stage()


def _rwkv(rw, prm, wup, aup, gup, batch, seq):
    t_tok = rw.shape[0]
    tb = min(RWKV_TILE, seq)
    nt = seq // tb
    npair = RWKV_PAIRS
    wid = npair * PAIR
    ngrp = RWKV_DIM // wid

    def tok(base):
        return pl.BlockSpec((tb, wid), lambda b, p, t: (b * nt + t, base * ngrp + p))

    def lora(base, width):
        return pl.BlockSpec((tb, width), lambda b, p, t: (b * nt + t, base))

    prm_spec = pl.BlockSpec((1, wid), lambda b, p, t: (0, p))
    in_specs = [
        tok(0), tok(1), tok(2),
        lora((3 * RWKV_DIM + ZG_PAD) // 128, 128), lora(3 * RWKV_DIM // ZG_PAD, ZG_PAD),
    ] + [prm_spec] * 7 + [
        pl.BlockSpec((128, wid), lambda b, p, t: (0, p)),
        pl.BlockSpec((128, wid), lambda b, p, t: (0, p)),
        pl.BlockSpec((ZG_PAD, wid), lambda b, p, t: (0, p)),
    ]
    return pl.pallas_call(
        functools.partial(_rwkv_kernel, tb=tb, npair=npair),
        grid=(batch, ngrp, nt),
        in_specs=in_specs,
        out_specs=pl.BlockSpec((tb, wid), lambda b, p, t: (b * nt + t, p)),
        out_shape=jax.ShapeDtypeStruct((t_tok, RWKV_DIM), BF16),
        scratch_shapes=[pltpu.VMEM((npair, PAIR, PAIR), F32)],
        compiler_params=_cparams(("arbitrary", "arbitrary", "arbitrary")),
        name="rwkv7",
    )(rw, rw, rw, rw, rw, *prm, wup, aup, gup)


def _attn_kernel(q_ref, k_ref, vt_ref, o_ref, m_ref, acc_ref, s_ref, p_ref, al_ref,
                 *, tk, heads, ks, ws, ahead, dlag, behind):
    i = pl.program_id(2)
    krow = lax.broadcasted_iota(jnp.int32, (ks, ws), 0)
    qcol = lax.broadcasted_iota(jnp.int32, (ks, ws), 1)
    neg = jnp.finfo(F32).min
    nsub, nstrip = tk // ks, tk // ws
    items = [(h, g) for h in range(heads) for g in range(nstrip)]
    rng = range(len(items))
    qs = [q_ref[h, 0, :, g * ws:(g + 1) * ws] for h, g in items]

    full_units = [(sub, i) for sub in range(nsub) for i in rng]
    diag_units = [(sub, i) for sub, i in full_units if not sub * ks > items[i][1] * ws + ws - 1]
    lag = dlag + behind
    tail_units = full_units[len(full_units) - lag:]
    assert diag_units[:ahead] == full_units[:ahead] and len(diag_units) > ahead + lag

    def score(j, unit):
        sub, i = unit
        return _dot(k_ref[items[i][0], j, sub * ks:(sub + 1) * ks, :], qs[i])

    def pv(j, unit, p):
        sub, i = unit
        return _dot(vt_ref[items[i][0], j, :, sub * ks:(sub + 1) * ks], p)

    def step(j, masked):
        units = diag_units if masked else full_units
        nu = len(units)
        keep = 0 if masked else lag
        jprev = jnp.maximum(j - 1, 0)
        s_val, p_val, pv_val, al_val, pv_old = {}, {}, {}, {}, {}
        for t in range(nu + (lag if masked else 0)):
            if t + ahead < nu:
                s_val[t + ahead] = score(j, units[t + ahead])
            if t < nu:
                sub, i = units[t]
                s = s_val.pop(t) if t >= ahead else s_ref[t]
                if masked and sub * ks + ks - 1 > items[i][1] * ws:
                    s = jnp.where(krow + (sub * ks - items[i][1] * ws) <= qcol, s, neg)
                m_old = m_ref[i]
                m_new = jnp.maximum(m_old, jnp.max(s, axis=0, keepdims=True))
                al_val[t] = jnp.exp2(m_old - m_new)
                m_ref[i] = m_new
                p_val[t] = jnp.exp2(s - m_new).astype(BF16)
                if not masked and t + ahead >= nu:
                    s_ref[t + ahead - nu] = score(j + 1, full_units[t + ahead - nu])
            if t < lag:
                pv_old[t] = pv(jprev, tail_units[t], p_ref[t])
            d = t - dlag
            if 0 <= d < nu - keep:
                pv_val[d] = pv(j, units[d], p_val.pop(d))
            x = t - behind
            if 0 <= x < lag:
                i = tail_units[x][1]
                acc_ref[i] = al_ref[x] * acc_ref[i] + pv_old.pop(x)
            w = d - behind
            if 0 <= w < nu - keep:
                i = units[w][1]
                acc_ref[i] = al_val.pop(w) * acc_ref[i] + pv_val.pop(w)
        if not masked:
            for x in range(lag):
                p_ref[x] = p_val[nu - lag + x]
                al_ref[x] = al_val[nu - lag + x]

    m_ref[...] = jnp.full(m_ref.shape, neg, F32)
    acc_ref[...] = jnp.zeros(acc_ref.shape, F32)
    p_ref[...] = jnp.zeros(p_ref.shape, BF16)
    al_ref[...] = jnp.ones(al_ref.shape, F32)
    for t in range(ahead):
        s_ref[t] = score(0, full_units[t])

    def body(j, c):
        step(j, False)
        return c

    lax.fori_loop(0, i, body, 0)
    step(i, True)
    def normalized(h):
        accs = [acc_ref[h * nstrip + g] for g in range(nstrip)]
        return jnp.concatenate([a[:V_DIM] / a[V_DIM:V_DIM + 1] for a in accs], axis=1)

    outs = [jnp.concatenate([normalized(h), normalized(h + 1)], axis=0).T for h in range(0, heads, 2)]
    o_ref[...] = jnp.concatenate(outs, axis=-1).astype(o_ref.dtype)


def _attn(q5, k5, vt5, batch, seq, tk):
    nk = seq // tk
    heads = 8
    ngrp = MLA_HEADS // heads
    ks, ws = min(256, tk), min(256, tk)
    ahead, dlag, behind = 3, 2, 2
    nitem = heads * (tk // ws)
    return pl.pallas_call(
        functools.partial(_attn_kernel, tk=tk, heads=heads, ks=ks, ws=ws, ahead=ahead, dlag=dlag, behind=behind),
        grid=(batch, ngrp, nk),
        in_specs=[
            pl.BlockSpec((None, heads, 1, HEAD_PAD, tk), lambda b, g, i: (b, g, i, 0, 0)),
            pl.BlockSpec((None, heads, nk, tk, HEAD_PAD), lambda b, g, i: (b, g, 0, 0, 0)),
            pl.BlockSpec((None, heads, nk, V_EXT, tk), lambda b, g, i: (b, g, 0, 0, 0)),
        ],
        out_specs=pl.BlockSpec((tk, heads * V_DIM), lambda b, g, i: (b * nk + i, g)),
        out_shape=jax.ShapeDtypeStruct((batch * seq, MLA_HEADS * V_DIM), BF16),
        scratch_shapes=[
            pltpu.VMEM((nitem, 1, ws), F32),
            pltpu.VMEM((nitem, V_EXT, ws), F32),
            pltpu.VMEM((ahead, ks, ws), F32),
            pltpu.VMEM((dlag + behind, ks, ws), BF16),
            pltpu.VMEM((dlag + behind, 1, ws), F32),
        ],
        compiler_params=_cparams(("arbitrary", "arbitrary", "arbitrary")),
        name="mla_attn",
    )(q5, k5, vt5)


def _tail_kernel(x_ref, yr_ref, o_ref, gt_ref, wor_ref, wom_ref, wout_ref, nf_ref, wup_ref, wdn_ref,
                 nfin_ref, out_ref):
    y_a = _dot(yr_ref[...], wor_ref[...])
    y_b = _dot(o_ref[...], wom_ref[...])
    merged = _sigmoid(gt_ref[:, :D_MODEL]) * y_a + _sigmoid(gt_ref[:, D_MODEL:]) * y_b
    h = x_ref[...] + _dot(merged.astype(BF16), wout_ref[...])
    f_in = _rms(h, nf_ref[...]).astype(BF16)
    cw = 1024
    acc = h
    for c in range(D_FF // cw):
        f = _dot(f_in, wup_ref[:, c * cw:(c + 1) * cw])
        f = jnp.square(jnp.maximum(f, 0.0)).astype(BF16)
        acc = acc + _dot(f, wdn_ref[c * cw:(c + 1) * cw, :])
    out_ref[...] = _rms(acc, nfin_ref[...])


def _tail(x2, yr, o, gt, wor, wom, wout, nf, wup, wdn, nfin, seq):
    t_tok = x2.shape[0]
    tm = min(TOKEN_TILE, seq)
    full = lambda shape: pl.BlockSpec(shape, lambda i: (0,) * len(shape), pipeline_mode=pl.Buffered(1))
    return pl.pallas_call(
        _tail_kernel,
        grid=(t_tok // tm,),
        in_specs=[
            pl.BlockSpec((tm, D_MODEL), lambda i: (i, 0)),
            pl.BlockSpec((tm, RWKV_DIM), lambda i: (i, 0)),
            pl.BlockSpec((tm, MLA_HEADS * V_DIM), lambda i: (i, 0)),
            pl.BlockSpec((tm, GT_COLS), lambda i: (i, 0)),
            full((RWKV_DIM, D_MODEL)), full((MLA_HEADS * V_DIM, D_MODEL)), full((D_MODEL, D_MODEL)),
            full((1, D_MODEL)), full((D_MODEL, D_FF)), full((D_FF, D_MODEL)), full((1, D_MODEL)),
        ],
        out_specs=pl.BlockSpec((tm, D_MODEL), lambda i: (i, 0)),
        out_shape=jax.ShapeDtypeStruct((t_tok, D_MODEL), F32),
        compiler_params=_cparams(("arbitrary",)),
        name="merge_ffn",
    )(x2, yr, o, gt, wor, wom, wout, nf, wup, wdn, nfin)


def _padc(w, n):
    return jnp.pad(w, ((0, 0), (0, n - w.shape[1])))


def _padr(w, n):
    return jnp.pad(w, ((0, n - w.shape[0]), (0, 0)))


def _rot_half(w):
    half = w.shape[-1] // 2
    return jnp.concatenate([-w[..., half:], w[..., :half]], axis=-1)


def kernel(x, positions, norm_mix, w_in, mu_shift, w0, w_up, a0, a_up, g_up, k_k, k_a, r_k, ln_w, ln_b, w_o_rwkv, q_norm, w_uq, kv_norm, w_ukv, w_o_mla, w_out, norm_ffn, w_ff_up, w_ff_down, norm_final):
    batch, seq, _ = x.shape
    t_tok = batch * seq
    x2 = x.reshape(t_tok, D_MODEL)
    l = 0

    wi = w_in[l].astype(BF16)
    o = 0
    w_r3 = wi[:, o:o + 3 * RWKV_DIM]; o += 3 * RWKV_DIM
    w_zw = wi[:, o:o + DECAY_LORA]; o += DECAY_LORA
    w_za = wi[:, o:o + ICLR_LORA]; o += ICLR_LORA
    w_zg = wi[:, o:o + GATE_LORA]; o += GATE_LORA
    w_cq = wi[:, o:o + Q_LORA]; o += Q_LORA
    w_ckv = wi[:, o:o + KV_LORA]; o += KV_LORA
    w_kr = wi[:, o:o + ROPE_DIM]; o += ROPE_DIM
    w_gate = wi[:, o:o + 2 * D_MODEL]
    zeros64 = jnp.zeros((D_MODEL, NOPE_DIM), wi.dtype)
    w_krb = jnp.concatenate([zeros64, w_kr, _rot_half(w_kr)], axis=1)
    w_a = jnp.concatenate([
        w_r3, _padc(w_zg, ZG_PAD), w_zw, w_za, w_cq, w_ckv, w_krb, w_gate], axis=1)
    mu = mu_shift[l]
    o = 3 * RWKV_DIM
    lo = o + DECAY_LORA + ICLR_LORA
    mu_a = jnp.concatenate([mu[:o], jnp.pad(mu[lo:], (0, ZG_PAD - GATE_LORA)), mu[o:lo]])[None, :]

    half = ROPE_DIM // 2
    inv_freq = 1.0 / (ROPE_THETA ** (jnp.arange(half, dtype=F32) * (2.0 / ROPE_DIM)))
    scale = (NOPE_DIM + ROPE_DIM) ** -0.5 * math.log2(math.e)
    wq = w_uq[l].reshape(Q_LORA, MLA_HEADS, NOPE_DIM + ROPE_DIM) * scale
    wq_p = jnp.pad(wq, ((0, 0), (0, 0), (0, HEAD_PAD - NOPE_DIM - ROPE_DIM)))
    wqt = jnp.transpose(wq_p, (1, 2, 0)).reshape(MLA_HEADS * HEAD_PAD, Q_LORA)
    wqrt = jnp.transpose(_rot_half(wq[..., NOPE_DIM:]), (1, 2, 0)).reshape(MLA_HEADS * ROPE_DIM, Q_LORA)
    fcol = jnp.concatenate([inv_freq, inv_freq])[:, None]
    wkv = w_ukv[l].reshape(KV_LORA, MLA_HEADS, NOPE_DIM + V_DIM)
    wk_p = jnp.pad(wkv[..., :NOPE_DIM], ((0, 0), (0, 0), (0, HEAD_PAD - NOPE_DIM)))
    wvt = jnp.transpose(wkv[..., NOPE_DIM:], (1, 2, 0))
    tk = min(TOKEN_TILE, seq)

    rw, gt, q5, k5, vt5 = _inproj(
        x2, norm_mix[l][None, :], w_a, mu_a, positions.reshape(t_tok // tk, 1, tk), fcol,
        q_norm[l][None, :], kv_norm[l][None, :], wqt.astype(BF16), wqrt.astype(BF16),
        wk_p.reshape(KV_LORA, -1).astype(BF16), wvt.astype(BF16), batch, seq)

    prm = [p[None, :] for p in (w0[l], a0[l], k_k[l], k_a[l], r_k[l].reshape(-1), ln_w[l], ln_b[l])]
    aup = jnp.concatenate([jnp.zeros_like(a_up[l]), a_up[l]], axis=0)
    yr = _rwkv(rw, prm, _padr(w_up[l], 128).astype(BF16), aup.astype(BF16),
               _padr(g_up[l], ZG_PAD).astype(BF16), batch, seq)

    o_att = _attn(q5, k5, vt5, batch, seq, tk)

    out = _tail(x2, yr, o_att, gt, w_o_rwkv[l].astype(BF16), w_o_mla[l].astype(BF16),
                w_out[l].astype(BF16), norm_ffn[l][None, :], w_ff_up[l].astype(BF16),
                w_ff_down[l].astype(BF16), norm_final[None, :], seq)
    return out.reshape(batch, seq, D_MODEL)
```

```python
import functools
import math

import jax
import jax.numpy as jnp
from jax import lax
from jax.experimental import pallas as pl
from jax.experimental.pallas import tpu as pltpu

F32 = jnp.float32
BF16 = jnp.bfloat16

D_MODEL = 1024
NORM_EPS = 1e-6
RWKV_HEAD = 64
RWKV_HEADS = 8
RWKV_DIM = RWKV_HEADS * RWKV_HEAD
DECAY_LORA = 64
ICLR_LORA = 64
GATE_LORA = 160
GN_EPS = 64e-5
MLA_HEADS = 8
Q_LORA = 256
KV_LORA = 128
NOPE_DIM = 64
ROPE_DIM = 32
V_DIM = 64
ROPE_THETA = 10000.0
D_FF = 4 * D_MODEL

LANE = 128
CHUNK = 64
INV_BASE = 8
PAIR = 2 * RWKV_HEAD
HEAD_PAD = 128
V_EXT = V_DIM + 16

ZG_PAD = 256
RW_COLS = 3 * RWKV_DIM + ZG_PAD + 128
ML_COLS = Q_LORA + KV_LORA + 128
GT_COLS = 2 * D_MODEL

VMEM_BYTES_V7X = 64 * 1024 * 1024
VMEM_LIMIT = VMEM_BYTES_V7X * 7 // 8
TOKEN_TILE = 512
RWKV_TILE = 4 * CHUNK
RWKV_PAIRS = RWKV_DIM // PAIR


def _cparams(sem):
    return pltpu.CompilerParams(dimension_semantics=sem, vmem_limit_bytes=VMEM_LIMIT)


def _sigmoid(x):
    return 1.0 / (1.0 + jnp.exp(-x))


def _rms(x, g):
    ms = jnp.mean(x * x, axis=-1, keepdims=True)
    return x * lax.rsqrt(ms + NORM_EPS) * g


def _dot(a, b):
    return jnp.dot(a, b, preferred_element_type=F32)


def _dot_nt(a, b):
    return lax.dot_general(a, b, (((1,), (1,)), ((), ())), preferred_element_type=F32)


def _dot_tn(a, b):
    return lax.dot_general(a, b, (((0,), (0,)), ((), ())), preferred_element_type=F32)


def _mla_prep_parts(ml, posrow_ref, fcol_ref, qn_ref, kvn_ref, wqt_ref, wqrt_ref, wk_ref, wvt_ref,
                    q_ref, k_ref, vt_ref, tk):
    c = {}
    rope = slice(NOPE_DIM, NOPE_DIM + ROPE_DIM)

    def trig():
        ang_t = fcol_ref[...] * posrow_ref[...].astype(F32)
        c["cos_t"] = jnp.cos(ang_t)
        c["sin_t"] = jnp.sin(ang_t)
        c["cq"] = _rms(ml[:, :Q_LORA], qn_ref[...]).astype(BF16)
        c["ckv"] = _rms(ml[:, Q_LORA:Q_LORA + KV_LORA], kvn_ref[...]).astype(BF16)

    def project():
        c["qf"] = _dot_nt(wqt_ref[...], c["cq"])
        c["qr"] = _dot_nt(wqrt_ref[...], c["cq"])
        c["kf"] = _dot(c["ckv"], wk_ref[...])

    def k_rope():
        zlo = jnp.zeros((NOPE_DIM, tk), F32)
        zhi = jnp.zeros((HEAD_PAD - NOPE_DIM - ROPE_DIM, tk), F32)
        cosf = jnp.concatenate([zlo, c["cos_t"], zhi], axis=0).T
        sinf = jnp.concatenate([zlo, c["sin_t"], zhi], axis=0).T
        krb = ml[:, Q_LORA + KV_LORA:Q_LORA + KV_LORA + HEAD_PAD]
        c["k_rope"] = krb * cosf + pltpu.roll(krb, HEAD_PAD - ROPE_DIM, 1) * sinf

    def heads(lo, hi):
        def f():
            for h in range(lo, hi):
                hs = slice(h * HEAD_PAD, (h + 1) * HEAD_PAD)
                qf_t = c["qf"][hs]
                qr_t = c["qr"][h * ROPE_DIM:(h + 1) * ROPE_DIM]
                q_t = jnp.concatenate([qf_t[:NOPE_DIM], qf_t[rope] * c["cos_t"] + qr_t * c["sin_t"],
                                       qf_t[NOPE_DIM + ROPE_DIM:]], axis=0)
                q_ref[h] = q_t.astype(q_ref.dtype)
                k_ref[h] = (c["kf"][:, hs] + c["k_rope"]).astype(k_ref.dtype)
                vt_ref[h, :V_DIM] = _dot_nt(wvt_ref[h], c["ckv"]).astype(vt_ref.dtype)
                vt_ref[h, V_DIM:] = jnp.ones((V_EXT - V_DIM, tk), vt_ref.dtype)
        return f

    half = MLA_HEADS // 2
    return [trig, project, k_rope, heads(0, half), heads(half, MLA_HEADS)]


def _inproj_kernel(x_ref, g_ref, w_ref, mu_ref, posrow_ref, fcol_ref, qn_ref, kvn_ref,
                   wqt_ref, wqrt_ref, wk_ref, wvt_ref,
                   rw_ref, gt_ref, q_ref, k_ref, vt_ref, carry_ref, *, tiles_per_seq, tm):
    i = pl.program_id(0)
    u = _rms(x_ref[...], g_ref[...]).astype(BF16)

    @pl.when(i % tiles_per_seq == 0)
    def _():
        carry_ref[...] = jnp.zeros_like(carry_ref)

    def shift_store(cs, z):
        prev = pltpu.roll(z, 1, 0)
        row0 = lax.broadcasted_iota(jnp.int32, z.shape, 0) == 0
        prev = jnp.where(row0, carry_ref[7:8, cs], prev)
        carry_ref[:, cs] = z[tm - 8:tm, :]
        rw_ref[:, cs] = z + (prev - z) * mu_ref[:, cs]

    def gate_store(cs, z):
        gt_ref[:, cs] = z

    cw = 512
    jobs = [(c0, slice(c0, min(c0 + cw, RW_COLS)), shift_store) for c0 in range(0, RW_COLS, cw)]
    jobs += [(RW_COLS + ML_COLS + c0, slice(c0, c0 + cw), gate_store) for c0 in range(0, GT_COLS, cw)]
    ml = _dot(u, w_ref[:, RW_COLS:RW_COLS + ML_COLS])
    fillers = _mla_prep_parts(ml, posrow_ref, fcol_ref, qn_ref, kvn_ref, wqt_ref, wqrt_ref, wk_ref, wvt_ref,
                              q_ref, k_ref, vt_ref, tm)
    mm = lambda job: _dot(u, w_ref[:, job[0]:job[0] + (job[1].stop - job[1].start)])
    z = mm(jobs[0])
    for n, job in enumerate(jobs):
        z_next = mm(jobs[n + 1]) if n + 1 < len(jobs) else None
        if n < len(fillers):
            fillers[n]()
        job[2](job[1], z)
        z = z_next


def _inproj(x2, g, w_a, mu_a, posrow, fcol, qn, kvn, wqt, wqrt, wk, wvt, batch, seq):
    t_tok = x2.shape[0]
    tm = min(TOKEN_TILE, seq)
    nk = seq // tm
    ncol = RW_COLS + ML_COLS + GT_COLS
    kern = functools.partial(_inproj_kernel, tiles_per_seq=nk, tm=tm)
    full = lambda shape: pl.BlockSpec(shape, lambda i: (0,) * len(shape))
    head_major = lambda rows, cols: pl.BlockSpec((None, MLA_HEADS, None, rows, cols),
                                                 lambda i: (i // nk, 0, i % nk, 0, 0))
    return pl.pallas_call(
        kern,
        grid=(t_tok // tm,),
        in_specs=[
            pl.BlockSpec((tm, D_MODEL), lambda i: (i, 0)),
            full((1, D_MODEL)),
            pl.BlockSpec((D_MODEL, ncol), lambda i: (0, 0), pipeline_mode=pl.Buffered(1)),
            full((1, RW_COLS)),
            pl.BlockSpec((None, 1, tm), lambda i: (i, 0, 0)),
            full((ROPE_DIM, 1)), full((1, Q_LORA)), full((1, KV_LORA)),
            full((MLA_HEADS * HEAD_PAD, Q_LORA)), full((MLA_HEADS * ROPE_DIM, Q_LORA)),
            full((KV_LORA, MLA_HEADS * HEAD_PAD)), full((MLA_HEADS, V_DIM, KV_LORA)),
        ],
        out_specs=[
            pl.BlockSpec((tm, RW_COLS), lambda i: (i, 0)),
            pl.BlockSpec((tm, GT_COLS), lambda i: (i, 0)),
            head_major(HEAD_PAD, tm), head_major(tm, HEAD_PAD), head_major(V_EXT, tm),
        ],
        out_shape=[
            jax.ShapeDtypeStruct((t_tok, RW_COLS), F32),
            jax.ShapeDtypeStruct((t_tok, GT_COLS), F32),
            jax.ShapeDtypeStruct((batch, MLA_HEADS, nk, HEAD_PAD, tm), BF16),
            jax.ShapeDtypeStruct((batch, MLA_HEADS, nk, tm, HEAD_PAD), BF16),
            jax.ShapeDtypeStruct((batch, MLA_HEADS, nk, V_EXT, tm), BF16),
        ],
        scratch_shapes=[pltpu.VMEM((8, RW_COLS), F32)],
        compiler_params=_cparams(("arbitrary",)),
        name="inproj",
    )(x2, g, w_a, mu_a, posrow, fcol, qn, kvn, wqt, wqrt, wk, wvt)


def _split3(x):
    hi = x.astype(BF16)
    r1 = x - hi.astype(F32)
    mid = r1.astype(BF16)
    lo = (r1 - mid.astype(F32)).astype(BF16)
    return hi, mid, lo


def _rwkv_masks():
    n = 2 * CHUNK
    row = lax.broadcasted_iota(jnp.int32, (n, n), 0)
    col = lax.broadcasted_iota(jnp.int32, (n, n), 1)
    same = (row // CHUNK) == (col // CHUNK)
    strict = jnp.where(same & (row > col), 1.0, 0.0).astype(F32)
    incl = jnp.where(same & (row >= col), 1.0, 0.0).astype(F32)
    base = jnp.where((row // INV_BASE) == (col // INV_BASE), 1.0, 0.0).astype(F32)
    levels = []
    blk = INV_BASE
    while blk < CHUNK:
        levels.append((jnp.where(((row // (2 * blk)) == (col // (2 * blk))) & ((row // blk) > (col // blk)),
                                 1.0, 0.0).astype(F32), blk))
        blk *= 2
    eye = jnp.where(row == col, 1.0, 0.0).astype(F32)
    headsel = (row // CHUNK) == (col // RWKV_HEAD)
    return strict, incl, base, levels, eye, headsel


def _rows(x, blk):
    return jnp.concatenate([x[s:s + blk] for s in range(blk, x.shape[0], 2 * blk)], axis=0)


def _merge_rows(x, odd, blk):
    parts = []
    for j, s in enumerate(range(0, x.shape[0], 2 * blk)):
        parts += [x[s:s + blk], odd[j * blk:(j + 1) * blk]]
    return jnp.concatenate(parts, axis=0)


def _rwkv_wave(refs, pairs, nchunk, consts, head_sum):
    (zr_ref, zk_ref, zv_ref, w0_ref, a0_ref, kk_ref, ka_ref, rk_ref, lnw_ref, lnb_ref,
     wup_ref, aup_ref, gup_ref, y_ref, h_ref, tanh_zw, za, sig_zg) = refs
    strict, incl, base, levels, eye, headsel, tril = consts
    incl2 = jnp.concatenate([incl, incl], axis=1)
    n = 2 * CHUNK
    bf = lambda x: x.astype(BF16)
    items = [(q, slice(c * CHUNK, (c + 1) * CHUNK)) for c in range(nchunk) for q in range(len(pairs))]
    rng = range(len(items))
    c = {}

    def stack(x):
        return jnp.where(headsel, jnp.concatenate([x, x], axis=0), 0.0)

    def pre():
        c["keep"] = []
        streams = []
        for p in pairs:
            ls = slice(p * PAIR, (p + 1) * PAIR)
            zr, zk, zv = zr_ref[:, ls], zk_ref[:, ls], zv_ref[:, ls]
            w_pre = w0_ref[:, ls] + _dot(tanh_zw, wup_ref[:, ls])
            lw = -math.exp(-0.5) * _sigmoid(w_pre)
            iclr = _sigmoid(a0_ref[:, ls] + _dot(za, aup_ref[:, ls]))
            gate = _dot(sig_zg, gup_ref[:, ls])
            kk = zk * kk_ref[:, ls]
            kk = kk / jnp.maximum(jnp.sqrt(head_sum(kk * kk)), 1e-12)
            k = zk * (1.0 + (iclr - 1.0) * ka_ref[:, ls])
            streams.append((zr, k, zv, -kk, kk * iclr, lw))
            c["keep"].append((zr, k, zv, gate))
        for j, name in enumerate(("r", "k", "v", "a", "b", "lw")):
            c[name] = [streams[q][j][s] for q, s in items]

    def cumsum():
        c["cs"] = []
        for i in rng:
            hi, mid, lo = _split3(c["lw"][i])
            c["cs"].append(_dot(tril, hi) + _dot(tril, mid) + _dot(tril, lo))

    def scale():
        cs, lw = c["cs"], c["lw"]
        g_in = [jnp.exp(cs[i]) for i in rng]
        g_ex = [jnp.exp(cs[i] - lw[i]) for i in rng]
        g_inv = [jnp.exp(-cs[i]) for i in rng]
        c["g_last"] = [jnp.exp(cs[i][CHUNK - 1:CHUNK, :]) for i in rng]
        c["rt"] = [stack(c["r"][i] * g_in[i]) for i in rng]
        c["at"] = [bf(stack(c["a"][i] * g_ex[i])) for i in rng]
        bt = [stack(c["b"][i] * g_inv[i]) for i in rng]
        kt = [stack(c["k"][i] * g_inv[i]) for i in rng]
        c["bkh_t"] = [bf(jnp.concatenate([(bt[i] * c["g_last"][i]).T, (kt[i] * c["g_last"][i]).T], axis=1))
                      for i in rng]
        c["v2"] = [bf(stack(c["v"][i])) for i in rng]
        c["bk"] = [bf(jnp.concatenate([bt[i], kt[i]], axis=0)) for i in rng]

    def gram():
        ga = [_dot_nt(c["at"][i], c["bk"][i]) for i in rng]
        gr = [_dot_nt(bf(c["rt"][i]), c["bk"][i]) for i in rng]
        c["a_ab"] = [g[:, :n] * strict for g in ga]
        c["a_ak"] = [bf(g[:, n:] * strict) for g in ga]
        c["a_rbk"] = [bf(g * incl2) for g in gr]

    def inv0():
        c["d"] = [bf(a * base) for a in c["a_ab"]]
        c["t"] = [eye + a * base for a in c["a_ab"]]
        c["p"] = [_dot(d, d) for d in c["d"]]

    def inv_double():
        xs = [_dot(bf(jnp.concatenate([t, p], axis=0)), bf(p)) for t, p in zip(c["t"], c["p"])]
        c["t"] = [t + x[:n] for t, x in zip(c["t"], xs)]
        c["p"] = [x[n:] for x in xs]

    def inv_last():
        c["t"] = [t + _dot(bf(t), bf(p)) for t, p in zip(c["t"], c["p"])]

    def level_a(msk, blk):
        def f():
            c["tb"] = [bf(t) for t in c["t"]]
            c["x"] = [bf(_dot(bf(_rows(t, blk)), bf(a * msk))) for t, a in zip(c["t"], c["a_ab"])]
        return f

    def level_b(blk):
        def f():
            c["t"] = [_merge_rows(t, _rows(t, blk) + _dot(x, tb), blk)
                      for t, x, tb in zip(c["t"], c["x"], c["tb"])]
        return f

    def apply_v():
        c["av"] = [bf(_dot(c["a_ak"][i], c["v2"][i])) for i in rng]

    def apply_t():
        c["wu"] = [bf(_dot(bf(c["t"][i]), jnp.concatenate([c["at"][i], c["av"][i]], axis=1)))
                   for i in rng]

    def assemble():
        zero = jnp.zeros((n, n), BF16)
        big = [_dot(jnp.concatenate([c["a_rbk"][i], c["bkh_t"][i]], axis=0),
                    jnp.concatenate([c["wu"][i], jnp.concatenate([zero, c["v2"][i]], axis=1)], axis=0))
               for i in rng]
        qeff = [c["rt"][i] + big[i][:n, :n] for i in rng]
        c["y0"] = [big[i][:n, n:] for i in rng]
        m = [eye * c["g_last"][i] + big[i][n:, :n] for i in rng]
        c["nn"] = [big[i][n:, n:] for i in rng]
        c["mq"] = [bf(jnp.concatenate([m[i], qeff[i]], axis=0)) for i in rng]

    def chain():
        hs = [h_ref[p] for p in pairs]
        ys = [[] for _ in pairs]
        for i in rng:
            q = items[i][0]
            hy = _dot(c["mq"][i], bf(hs[q]))
            hs[q] = hy[:n] + c["nn"][i]
            y2 = hy[n:] + c["y0"][i]
            ys[q].append(y2[:CHUNK, :] + y2[CHUNK:, :])
        for q, p in enumerate(pairs):
            h_ref[p] = hs[q]
        c["ys"] = [jnp.concatenate(y, axis=0) for y in ys]

    def post():
        for q, p in enumerate(pairs):
            ls = slice(p * PAIR, (p + 1) * PAIR)
            zr, k, zv, gate = c["keep"][q]
            y = c["ys"][q]
            mean = head_sum(y) * (1.0 / RWKV_HEAD)
            yc = y - mean
            var = head_sum(yc * yc) * (1.0 / RWKV_HEAD)
            yn = yc * lax.rsqrt(var + GN_EPS) * lnw_ref[:, ls] + lnb_ref[:, ls]
            bonus = head_sum(zr * k * rk_ref[:, ls]) * zv
            y_ref[:, ls] = ((yn + bonus) * gate).astype(y_ref.dtype)

    doublings = [inv_double] * (INV_BASE.bit_length() - 3)
    level_stages = [f for msk, blk in levels for f in (level_a(msk, blk), level_b(blk))]
    return ([pre, cumsum, scale, gram, inv0] + doublings + [inv_last] + level_stages
            + [apply_v, apply_t, assemble, chain, post])


def _rwkv_kernel(zr_ref, zk_ref, zv_ref, zwa_ref, zg_ref,
                 w0_ref, a0_ref, kk_ref, ka_ref, rk_ref, lnw_ref, lnb_ref,
                 wup_ref, aup_ref, gup_ref, y_ref, h_ref, *, tb, npair):
    @pl.when(pl.program_id(2) == 0)
    def _():
        h_ref[...] = jnp.zeros_like(h_ref)

    lane = lax.broadcasted_iota(jnp.int32, (tb, PAIR), 1)
    head0 = lane < RWKV_HEAD

    def head_sum(x):
        s0 = jnp.sum(jnp.where(head0, x, 0.0), axis=-1, keepdims=True)
        s1 = jnp.sum(jnp.where(head0, 0.0, x), axis=-1, keepdims=True)
        return jnp.where(head0, s0, s1)

    trow = lax.broadcasted_iota(jnp.int32, (CHUNK, CHUNK), 0)
    tcol = lax.broadcasted_iota(jnp.int32, (CHUNK, CHUNK), 1)
    tril = jnp.where(trow >= tcol, 1.0, 0.0).astype(BF16)
    consts = _rwkv_masks() + (tril,)
    refs = (zr_ref, zk_ref, zv_ref, w0_ref, a0_ref, kk_ref, ka_ref, rk_ref, lnw_ref, lnb_ref,
            wup_ref, aup_ref, gup_ref, y_ref, h_ref,
            jnp.tanh(zwa_ref[...]).astype(BF16), zwa_ref[...].astype(BF16),
            _sigmoid(zg_ref[...]).astype(BF16))
    for stage in _rwkv_wave(refs, list(range(npair)), tb // CHUNK, consts, head_sum):
        stage()


def _rwkv(rw, prm, wup, aup, gup, batch, seq):
    t_tok = rw.shape[0]
    tb = min(RWKV_TILE, seq)
    nt = seq // tb
    npair = RWKV_PAIRS
    wid = npair * PAIR
    ngrp = RWKV_DIM // wid

    def tok(base):
        return pl.BlockSpec((tb, wid), lambda b, p, t: (b * nt + t, base * ngrp + p))

    def lora(base, width):
        return pl.BlockSpec((tb, width), lambda b, p, t: (b * nt + t, base))

    prm_spec = pl.BlockSpec((1, wid), lambda b, p, t: (0, p))
    in_specs = [
        tok(0), tok(1), tok(2),
        lora((3 * RWKV_DIM + ZG_PAD) // 128, 128), lora(3 * RWKV_DIM // ZG_PAD, ZG_PAD),
    ] + [prm_spec] * 7 + [
        pl.BlockSpec((128, wid), lambda b, p, t: (0, p)),
        pl.BlockSpec((128, wid), lambda b, p, t: (0, p)),
        pl.BlockSpec((ZG_PAD, wid), lambda b, p, t: (0, p)),
    ]
    return pl.pallas_call(
        functools.partial(_rwkv_kernel, tb=tb, npair=npair),
        grid=(batch, ngrp, nt),
        in_specs=in_specs,
        out_specs=pl.BlockSpec((tb, wid), lambda b, p, t: (b * nt + t, p)),
        out_shape=jax.ShapeDtypeStruct((t_tok, RWKV_DIM), BF16),
        scratch_shapes=[pltpu.VMEM((npair, PAIR, PAIR), F32)],
        compiler_params=_cparams(("arbitrary", "arbitrary", "arbitrary")),
        name="rwkv7",
    )(rw, rw, rw, rw, rw, *prm, wup, aup, gup)


def _attn_kernel(q_ref, k_ref, vt_ref, o_ref, m_ref, acc_ref, s_ref, p_ref, al_ref,
                 *, tk, heads, ks, ws, ahead, dlag, behind):
    i = pl.program_id(2)
    krow = lax.broadcasted_iota(jnp.int32, (ks, ws), 0)
    qcol = lax.broadcasted_iota(jnp.int32, (ks, ws), 1)
    neg = jnp.finfo(F32).min
    nsub, nstrip = tk // ks, tk // ws
    items = [(h, g) for g in reversed(range(nstrip)) for h in range(heads)]
    rng = range(len(items))
    qs = [q_ref[h, 0, :, g * ws:(g + 1) * ws] for h, g in items]

    full_units = [(sub, i) for i in rng for sub in range(nsub)]
    diag_units = [(sub, i) for sub, i in full_units if not sub * ks > items[i][1] * ws + ws - 1]
    lag = dlag + behind
    tail_units = full_units[len(full_units) - lag:]

    def jobs_of(units):
        out, f = {}, 0
        for u in range(len(units)):
            if units[u][1] != units[f][1]:
                f = u
            out[u] = f
        size = {f: sum(1 for u in out if out[u] == f) for f in set(out.values())}
        return {u: (out[u], size[out[u]]) for u in out}

    full_jobs, diag_jobs = jobs_of(full_units), jobs_of(diag_units)
    ncar = s_ref.shape[0]
    assert ncar == max(u for u in full_jobs if full_jobs[u][0] < ahead) + 1
    assert diag_units[:ncar] == full_units[:ncar] and len(diag_units) > ncar + lag
    assert all(diag_jobs[u] == full_jobs[u] for u in range(ncar))

    def score(j, units, first, count):
        sub, i = units[first]
        return _dot(k_ref[items[i][0], j, sub * ks:(sub + count) * ks, :], qs[i])

    def pv(j, unit, p):
        sub, i = unit
        return _dot(vt_ref[items[i][0], j, :, sub * ks:(sub + 1) * ks], p)

    def step(j, masked):
        units = diag_units if masked else full_units
        nu = len(units)
        keep = 0 if masked else lag
        jprev = jnp.maximum(j - 1, 0)
        jobs = diag_jobs if masked else full_jobs
        s_val, p_val, pv_val, al_val, pv_old = {}, {}, {}, {}, {}
        for t in range(nu + (lag if masked else 0)):
            v = t + ahead
            if ncar <= v < nu and jobs[v][0] == v:
                s_job = score(j, units, v, jobs[v][1])
                for x in range(jobs[v][1]):
                    s_val[v + x] = s_job[x * ks:(x + 1) * ks]
            if t < nu:
                sub, i = units[t]
                s = s_val.pop(t) if t >= ncar else s_ref[t]
                if masked and sub * ks + ks - 1 > items[i][1] * ws:
                    s = jnp.where(krow + (sub * ks - items[i][1] * ws) <= qcol, s, neg)
                m_old = m_ref[i]
                m_new = jnp.maximum(m_old, jnp.max(s, axis=0, keepdims=True))
                al_val[t] = jnp.exp2(m_old - m_new)
                m_ref[i] = m_new
                p_val[t] = jnp.exp2(s - m_new).astype(BF16)
                if not masked and v >= nu and v - nu < ncar and full_jobs[v - nu][0] == v - nu:
                    s_job = score(j + 1, full_units, v - nu, full_jobs[v - nu][1])
                    for x in range(full_jobs[v - nu][1]):
                        s_ref[v - nu + x] = s_job[x * ks:(x + 1) * ks]
            if t < lag:
                pv_old[t] = pv(jprev, tail_units[t], p_ref[t])
            d = t - dlag
            if 0 <= d < nu - keep:
                pv_val[d] = pv(j, units[d], p_val.pop(d))
            x = t - behind
            if 0 <= x < lag:
                i = tail_units[x][1]
                acc_ref[i] = al_ref[x] * acc_ref[i] + pv_old.pop(x)
            w = d - behind
            if 0 <= w < nu - keep:
                i = units[w][1]
                acc_ref[i] = al_val.pop(w) * acc_ref[i] + pv_val.pop(w)
        if not masked:
            for x in range(lag):
                p_ref[x] = p_val[nu - lag + x]
                al_ref[x] = al_val[nu - lag + x]

    m_ref[...] = jnp.full(m_ref.shape, neg, F32)
    acc_ref[...] = jnp.zeros(acc_ref.shape, F32)
    p_ref[...] = jnp.zeros(p_ref.shape, BF16)
    al_ref[...] = jnp.ones(al_ref.shape, F32)
    for t in range(ncar):
        if full_jobs[t][0] == t:
            s_job = score(0, full_units, t, full_jobs[t][1])
            for x in range(full_jobs[t][1]):
                s_ref[t + x] = s_job[x * ks:(x + 1) * ks]

    def body(j, c):
        step(j, False)
        return c

    lax.fori_loop(0, i, body, 0)
    step(i, True)
    def normalized(h):
        accs = [acc_ref[items.index((h, g))] for g in range(nstrip)]
        return jnp.concatenate([a[:V_DIM] / a[V_DIM:V_DIM + 1] for a in accs], axis=1)

    outs = [jnp.concatenate([normalized(h), normalized(h + 1)], axis=0).T for h in range(0, heads, 2)]
    o_ref[...] = jnp.concatenate(outs, axis=-1).astype(o_ref.dtype)


def _attn(q5, k5, vt5, batch, seq, tk):
    nk = seq // tk
    heads = 8
    ngrp = MLA_HEADS // heads
    ks, ws = min(256, tk), min(256, tk)
    ahead, dlag, behind = 3, 2, 2
    nitem = heads * (tk // ws)
    nsub = tk // ks
    ncar = -(-ahead // nsub) * nsub
    return pl.pallas_call(
        functools.partial(_attn_kernel, tk=tk, heads=heads, ks=ks, ws=ws, ahead=ahead, dlag=dlag, behind=behind),
        grid=(batch, ngrp, nk),
        in_specs=[
            pl.BlockSpec((None, heads, 1, HEAD_PAD, tk), lambda b, g, i: (b, g, i, 0, 0)),
            pl.BlockSpec((None, heads, nk, tk, HEAD_PAD), lambda b, g, i: (b, g, 0, 0, 0)),
            pl.BlockSpec((None, heads, nk, V_EXT, tk), lambda b, g, i: (b, g, 0, 0, 0)),
        ],
        out_specs=pl.BlockSpec((tk, heads * V_DIM), lambda b, g, i: (b * nk + i, g)),
        out_shape=jax.ShapeDtypeStruct((batch * seq, MLA_HEADS * V_DIM), BF16),
        scratch_shapes=[
            pltpu.VMEM((nitem, 1, ws), F32),
            pltpu.VMEM((nitem, V_EXT, ws), F32),
            pltpu.VMEM((ncar, ks, ws), F32),
            pltpu.VMEM((dlag + behind, ks, ws), BF16),
            pltpu.VMEM((dlag + behind, 1, ws), F32),
        ],
        compiler_params=_cparams(("arbitrary", "arbitrary", "arbitrary")),
        name="mla_attn",
    )(q5, k5, vt5)


def _tail_kernel(x_ref, yr_ref, o_ref, gt_ref, wor_ref, wom_ref, wout_ref, nf_ref, wup_ref, wdn_ref,
                 nfin_ref, out_ref):
    y_a = _dot(yr_ref[...], wor_ref[...])
    y_b = _dot(o_ref[...], wom_ref[...])
    merged = _sigmoid(gt_ref[:, :D_MODEL]) * y_a + _sigmoid(gt_ref[:, D_MODEL:]) * y_b
    h = x_ref[...] + _dot(merged.astype(BF16), wout_ref[...])
    f_in = _rms(h, nf_ref[...]).astype(BF16)
    cw = 1024
    acc = h
    for c in range(D_FF // cw):
        f = _dot(f_in, wup_ref[:, c * cw:(c + 1) * cw])
        f = jnp.square(jnp.maximum(f, 0.0)).astype(BF16)
        acc = acc + _dot(f, wdn_ref[c * cw:(c + 1) * cw, :])
    out_ref[...] = _rms(acc, nfin_ref[...])


def _tail(x2, yr, o, gt, wor, wom, wout, nf, wup, wdn, nfin, seq):
    t_tok = x2.shape[0]
    tm = min(TOKEN_TILE, seq)
    full = lambda shape: pl.BlockSpec(shape, lambda i: (0,) * len(shape), pipeline_mode=pl.Buffered(1))
    return pl.pallas_call(
        _tail_kernel,
        grid=(t_tok // tm,),
        in_specs=[
            pl.BlockSpec((tm, D_MODEL), lambda i: (i, 0)),
            pl.BlockSpec((tm, RWKV_DIM), lambda i: (i, 0)),
            pl.BlockSpec((tm, MLA_HEADS * V_DIM), lambda i: (i, 0)),
            pl.BlockSpec((tm, GT_COLS), lambda i: (i, 0)),
            full((RWKV_DIM, D_MODEL)), full((MLA_HEADS * V_DIM, D_MODEL)), full((D_MODEL, D_MODEL)),
            full((1, D_MODEL)), full((D_MODEL, D_FF)), full((D_FF, D_MODEL)), full((1, D_MODEL)),
        ],
        out_specs=pl.BlockSpec((tm, D_MODEL), lambda i: (i, 0)),
        out_shape=jax.ShapeDtypeStruct((t_tok, D_MODEL), F32),
        compiler_params=_cparams(("arbitrary",)),
        name="merge_ffn",
    )(x2, yr, o, gt, wor, wom, wout, nf, wup, wdn, nfin)


def _padc(w, n):
    return jnp.pad(w, ((0, 0), (0, n - w.shape[1])))


def _padr(w, n):
    return jnp.pad(w, ((0, n - w.shape[0]), (0, 0)))


def _rot_half(w):
    half = w.shape[-1] // 2
    return jnp.concatenate([-w[..., half:], w[..., :half]], axis=-1)


def kernel(x, positions, norm_mix, w_in, mu_shift, w0, w_up, a0, a_up, g_up, k_k, k_a, r_k, ln_w, ln_b, w_o_rwkv, q_norm, w_uq, kv_norm, w_ukv, w_o_mla, w_out, norm_ffn, w_ff_up, w_ff_down, norm_final):
    batch, seq, _ = x.shape
    t_tok = batch * seq
    x2 = x.reshape(t_tok, D_MODEL)
    l = 0

    wi = w_in[l].astype(BF16)
    o = 0
    w_r3 = wi[:, o:o + 3 * RWKV_DIM]; o += 3 * RWKV_DIM
    w_zw = wi[:, o:o + DECAY_LORA]; o += DECAY_LORA
    w_za = wi[:, o:o + ICLR_LORA]; o += ICLR_LORA
    w_zg = wi[:, o:o + GATE_LORA]; o += GATE_LORA
    w_cq = wi[:, o:o + Q_LORA]; o += Q_LORA
    w_ckv = wi[:, o:o + KV_LORA]; o += KV_LORA
    w_kr = wi[:, o:o + ROPE_DIM]; o += ROPE_DIM
    w_gate = wi[:, o:o + 2 * D_MODEL]
    zeros64 = jnp.zeros((D_MODEL, NOPE_DIM), wi.dtype)
    w_krb = jnp.concatenate([zeros64, w_kr, _rot_half(w_kr)], axis=1)
    w_a = jnp.concatenate([
        w_r3, _padc(w_zg, ZG_PAD), w_zw, w_za, w_cq, w_ckv, w_krb, w_gate], axis=1)
    mu = mu_shift[l]
    o = 3 * RWKV_DIM
    lo = o + DECAY_LORA + ICLR_LORA
    mu_a = jnp.concatenate([mu[:o], jnp.pad(mu[lo:], (0, ZG_PAD - GATE_LORA)), mu[o:lo]])[None, :]

    half = ROPE_DIM // 2
    inv_freq = 1.0 / (ROPE_THETA ** (jnp.arange(half, dtype=F32) * (2.0 / ROPE_DIM)))
    scale = (NOPE_DIM + ROPE_DIM) ** -0.5 * math.log2(math.e)
    wq = w_uq[l].reshape(Q_LORA, MLA_HEADS, NOPE_DIM + ROPE_DIM) * scale
    wq_p = jnp.pad(wq, ((0, 0), (0, 0), (0, HEAD_PAD - NOPE_DIM - ROPE_DIM)))
    wqt = jnp.transpose(wq_p, (1, 2, 0)).reshape(MLA_HEADS * HEAD_PAD, Q_LORA)
    wqrt = jnp.transpose(_rot_half(wq[..., NOPE_DIM:]), (1, 2, 0)).reshape(MLA_HEADS * ROPE_DIM, Q_LORA)
    fcol = jnp.concatenate([inv_freq, inv_freq])[:, None]
    wkv = w_ukv[l].reshape(KV_LORA, MLA_HEADS, NOPE_DIM + V_DIM)
    wk_p = jnp.pad(wkv[..., :NOPE_DIM], ((0, 0), (0, 0), (0, HEAD_PAD - NOPE_DIM)))
    wvt = jnp.transpose(wkv[..., NOPE_DIM:], (1, 2, 0))
    tk = min(TOKEN_TILE, seq)

    rw, gt, q5, k5, vt5 = _inproj(
        x2, norm_mix[l][None, :], w_a, mu_a, positions.reshape(t_tok // tk, 1, tk), fcol,
        q_norm[l][None, :], kv_norm[l][None, :], wqt.astype(BF16), wqrt.astype(BF16),
        wk_p.reshape(KV_LORA, -1).astype(BF16), wvt.astype(BF16), batch, seq)

    prm = [p[None, :] for p in (w0[l], a0[l], k_k[l], k_a[l], r_k[l].reshape(-1), ln_w[l], ln_b[l])]
    aup = jnp.concatenate([jnp.zeros_like(a_up[l]), a_up[l]], axis=0)
    yr = _rwkv(rw, prm, _padr(w_up[l], 128).astype(BF16), aup.astype(BF16),
               _padr(g_up[l], ZG_PAD).astype(BF16), batch, seq)

    o_att = _attn(q5, k5, vt5, batch, seq, tk)

    out = _tail(x2, yr, o_att, gt, w_o_rwkv[l].astype(BF16), w_o_mla[l].astype(BF16),
                w_out[l].astype(BF16), norm_ffn[l][None, :], w_ff_up[l].astype(BF16),
                w_ff_down[l].astype(BF16), norm_final[None, :], seq)
    return out.reshape(batch, seq, D_MODEL)
```

```python
import functools
import math

import jax
import jax.numpy as jnp
from jax import lax
from jax.experimental import pallas as pl
from jax.experimental.pallas import tpu as pltpu

F32 = jnp.float32
BF16 = jnp.bfloat16

D_MODEL = 1024
NORM_EPS = 1e-6
RWKV_HEAD = 64
RWKV_HEADS = 8
RWKV_DIM = RWKV_HEADS * RWKV_HEAD
DECAY_LORA = 64
ICLR_LORA = 64
GATE_LORA = 160
GN_EPS = 64e-5
MLA_HEADS = 8
Q_LORA = 256
KV_LORA = 128
NOPE_DIM = 64
ROPE_DIM = 32
V_DIM = 64
ROPE_THETA = 10000.0
D_FF = 4 * D_MODEL

LANE = 128
CHUNK = 64
INV_BASE = 8
PAIR = 2 * RWKV_HEAD
HEAD_PAD = 128
V_EXT = V_DIM + 16

ZG_PAD = 256
RW_COLS = 3 * RWKV_DIM + ZG_PAD + 128
ML_COLS = Q_LORA + KV_LORA + 128
GT_COLS = 2 * D_MODEL

VMEM_BYTES_V7X = 64 * 1024 * 1024
VMEM_LIMIT = VMEM_BYTES_V7X * 7 // 8
TOKEN_TILE = 512
RWKV_TILE = 4 * CHUNK
RWKV_PAIRS = RWKV_DIM // PAIR


def _cparams(sem):
    return pltpu.CompilerParams(dimension_semantics=sem, vmem_limit_bytes=VMEM_LIMIT)


def _sigmoid(x):
    return 1.0 / (1.0 + jnp.exp(-x))


def _rms(x, g):
    ms = jnp.mean(x * x, axis=-1, keepdims=True)
    return x * lax.rsqrt(ms + NORM_EPS) * g


def _dot(a, b):
    return jnp.dot(a, b, preferred_element_type=F32)


def _dot_nt(a, b):
    return lax.dot_general(a, b, (((1,), (1,)), ((), ())), preferred_element_type=F32)


def _dot_tn(a, b):
    return lax.dot_general(a, b, (((0,), (0,)), ((), ())), preferred_element_type=F32)


def _mla_prep_parts(ml, posrow_ref, fcol_ref, qn_ref, kvn_ref, wqt_ref, wqrt_ref, wk_ref, wvt_ref,
                    q_ref, k_ref, vt_ref, tk):
    c = {}
    rope = slice(NOPE_DIM, NOPE_DIM + ROPE_DIM)

    def trig():
        ang_t = fcol_ref[...] * posrow_ref[...].astype(F32)
        c["cos_t"] = jnp.cos(ang_t)
        c["sin_t"] = jnp.sin(ang_t)
        c["cq"] = _rms(ml[:, :Q_LORA], qn_ref[...]).astype(BF16)
        c["ckv"] = _rms(ml[:, Q_LORA:Q_LORA + KV_LORA], kvn_ref[...]).astype(BF16)

    def project():
        c["qf"] = _dot_nt(wqt_ref[...], c["cq"])
        c["qr"] = _dot_nt(wqrt_ref[...], c["cq"])
        c["kf"] = _dot(c["ckv"], wk_ref[...])
        c["vt"] = _dot_nt(wvt_ref[...], c["ckv"])

    def k_rope():
        zlo = jnp.zeros((NOPE_DIM, tk), F32)
        zhi = jnp.zeros((HEAD_PAD - NOPE_DIM - ROPE_DIM, tk), F32)
        cosf = jnp.concatenate([zlo, c["cos_t"], zhi], axis=0).T
        sinf = jnp.concatenate([zlo, c["sin_t"], zhi], axis=0).T
        krb = ml[:, Q_LORA + KV_LORA:Q_LORA + KV_LORA + HEAD_PAD]
        c["k_rope"] = krb * cosf + pltpu.roll(krb, HEAD_PAD - ROPE_DIM, 1) * sinf

    def heads(lo, hi):
        def f():
            for h in range(lo, hi):
                hs = slice(h * HEAD_PAD, (h + 1) * HEAD_PAD)
                qf_t = c["qf"][hs]
                qr_t = c["qr"][h * ROPE_DIM:(h + 1) * ROPE_DIM]
                q_t = jnp.concatenate([qf_t[:NOPE_DIM], qf_t[rope] * c["cos_t"] + qr_t * c["sin_t"],
                                       qf_t[NOPE_DIM + ROPE_DIM:]], axis=0)
                q_ref[h] = q_t.astype(q_ref.dtype)
                k_ref[h] = (c["kf"][:, hs] + c["k_rope"]).astype(k_ref.dtype)
                vt_ref[h, :V_DIM] = c["vt"][h * V_DIM:(h + 1) * V_DIM].astype(vt_ref.dtype)
                vt_ref[h, V_DIM:] = jnp.ones((V_EXT - V_DIM, tk), vt_ref.dtype)
        return f

    half = MLA_HEADS // 2
    return [trig, project, k_rope, heads(0, half), heads(half, MLA_HEADS)]


def _inproj_kernel(x_ref, g_ref, w_ref, mu_ref, posrow_ref, fcol_ref, qn_ref, kvn_ref,
                   wqt_ref, wqrt_ref, wk_ref, wvt_ref,
                   rw_ref, gt_ref, q_ref, k_ref, vt_ref, carry_ref, *, tiles_per_seq, tm):
    i = pl.program_id(0)
    u = _rms(x_ref[...], g_ref[...]).astype(BF16)

    @pl.when(i % tiles_per_seq == 0)
    def _():
        carry_ref[...] = jnp.zeros_like(carry_ref)

    def shift_store(cs, z):
        prev = pltpu.roll(z, 1, 0)
        row0 = lax.broadcasted_iota(jnp.int32, z.shape, 0) == 0
        prev = jnp.where(row0, carry_ref[7:8, cs], prev)
        carry_ref[:, cs] = z[tm - 8:tm, :]
        rw_ref[:, cs] = z + (prev - z) * mu_ref[:, cs]

    def gate_store(cs, z):
        gt_ref[:, cs] = z

    cw = 512
    jobs = [(c0, slice(c0, min(c0 + cw, RW_COLS)), shift_store) for c0 in range(0, RW_COLS, cw)]
    jobs += [(RW_COLS + ML_COLS + c0, slice(c0, c0 + cw), gate_store) for c0 in range(0, GT_COLS, cw)]
    ml = _dot(u, w_ref[:, RW_COLS:RW_COLS + ML_COLS])
    fillers = _mla_prep_parts(ml, posrow_ref, fcol_ref, qn_ref, kvn_ref, wqt_ref, wqrt_ref, wk_ref, wvt_ref,
                              q_ref, k_ref, vt_ref, tm)
    mm = lambda job: _dot(u, w_ref[:, job[0]:job[0] + (job[1].stop - job[1].start)])
    z = mm(jobs[0])
    for n, job in enumerate(jobs):
        z_next = mm(jobs[n + 1]) if n + 1 < len(jobs) else None
        if n < len(fillers):
            fillers[n]()
        job[2](job[1], z)
        z = z_next


def _inproj(x2, g, w_a, mu_a, posrow, fcol, qn, kvn, wqt, wqrt, wk, wvt, batch, seq):
    t_tok = x2.shape[0]
    tm = min(TOKEN_TILE, seq)
    nk = seq // tm
    ncol = RW_COLS + ML_COLS + GT_COLS
    kern = functools.partial(_inproj_kernel, tiles_per_seq=nk, tm=tm)
    full = lambda shape: pl.BlockSpec(shape, lambda i: (0,) * len(shape))
    head_major = lambda rows, cols: pl.BlockSpec((None, MLA_HEADS, None, rows, cols),
                                                 lambda i: (i // nk, 0, i % nk, 0, 0))
    return pl.pallas_call(
        kern,
        grid=(t_tok // tm,),
        in_specs=[
            pl.BlockSpec((tm, D_MODEL), lambda i: (i, 0)),
            full((1, D_MODEL)),
            pl.BlockSpec((D_MODEL, ncol), lambda i: (0, 0), pipeline_mode=pl.Buffered(1)),
            full((1, RW_COLS)),
            pl.BlockSpec((None, 1, tm), lambda i: (i, 0, 0)),
            full((ROPE_DIM, 1)), full((1, Q_LORA)), full((1, KV_LORA)),
            full((MLA_HEADS * HEAD_PAD, Q_LORA)), full((MLA_HEADS * ROPE_DIM, Q_LORA)),
            full((KV_LORA, MLA_HEADS * HEAD_PAD)), full((MLA_HEADS * V_DIM, KV_LORA)),
        ],
        out_specs=[
            pl.BlockSpec((tm, RW_COLS), lambda i: (i, 0)),
            pl.BlockSpec((tm, GT_COLS), lambda i: (i, 0)),
            head_major(HEAD_PAD, tm), head_major(tm, HEAD_PAD), head_major(V_EXT, tm),
        ],
        out_shape=[
            jax.ShapeDtypeStruct((t_tok, RW_COLS), F32),
            jax.ShapeDtypeStruct((t_tok, GT_COLS), F32),
            jax.ShapeDtypeStruct((batch, MLA_HEADS, nk, HEAD_PAD, tm), BF16),
            jax.ShapeDtypeStruct((batch, MLA_HEADS, nk, tm, HEAD_PAD), BF16),
            jax.ShapeDtypeStruct((batch, MLA_HEADS, nk, V_EXT, tm), BF16),
        ],
        scratch_shapes=[pltpu.VMEM((8, RW_COLS), F32)],
        compiler_params=_cparams(("arbitrary",)),
        name="inproj",
    )(x2, g, w_a, mu_a, posrow, fcol, qn, kvn, wqt, wqrt, wk, wvt)


def _split3(x):
    hi = x.astype(BF16)
    r1 = x - hi.astype(F32)
    mid = r1.astype(BF16)
    lo = (r1 - mid.astype(F32)).astype(BF16)
    return hi, mid, lo


def _rwkv_masks():
    n = 2 * CHUNK
    row = lax.broadcasted_iota(jnp.int32, (n, n), 0)
    col = lax.broadcasted_iota(jnp.int32, (n, n), 1)
    same = (row // CHUNK) == (col // CHUNK)
    strict = jnp.where(same & (row > col), 1.0, 0.0).astype(F32)
    incl = jnp.where(same & (row >= col), 1.0, 0.0).astype(F32)
    base = jnp.where((row // INV_BASE) == (col // INV_BASE), 1.0, 0.0).astype(F32)
    levels = []
    blk = INV_BASE
    while blk < CHUNK:
        levels.append((jnp.where(((row // (2 * blk)) == (col // (2 * blk))) & ((row // blk) > (col // blk)),
                                 1.0, 0.0).astype(F32), blk))
        blk *= 2
    eye = jnp.where(row == col, 1.0, 0.0).astype(F32)
    headsel = (row // CHUNK) == (col // RWKV_HEAD)
    return strict, incl, base, levels, eye, headsel


def _rows(x, blk):
    return jnp.concatenate([x[s:s + blk] for s in range(blk, x.shape[0], 2 * blk)], axis=0)


def _merge_rows(x, odd, blk):
    parts = []
    for j, s in enumerate(range(0, x.shape[0], 2 * blk)):
        parts += [x[s:s + blk], odd[j * blk:(j + 1) * blk]]
    return jnp.concatenate(parts, axis=0)


def _rwkv_wave(refs, pairs, nchunk, consts, head_sum):
    (zr_ref, zk_ref, zv_ref, w0_ref, a0_ref, kk_ref, ka_ref, rk_ref, lnw_ref, lnb_ref,
     wup_ref, aup_ref, gup_ref, y_ref, h_ref, tanh_zw, za, sig_zg) = refs
    strict, incl, base, levels, eye, headsel, tril = consts
    incl2 = jnp.concatenate([incl, incl], axis=1)
    n = 2 * CHUNK
    bf = lambda x: x.astype(BF16)
    items = [(q, slice(c * CHUNK, (c + 1) * CHUNK)) for c in range(nchunk) for q in range(len(pairs))]
    rng = range(len(items))
    c = {}

    def stack(x):
        return jnp.where(headsel, jnp.concatenate([x, x], axis=0), 0.0)

    def pre():
        c["keep"] = []
        streams = []
        for p in pairs:
            ls = slice(p * PAIR, (p + 1) * PAIR)
            zr, zk, zv = zr_ref[:, ls], zk_ref[:, ls], zv_ref[:, ls]
            w_pre = w0_ref[:, ls] + _dot(tanh_zw, wup_ref[:, ls])
            lw = -math.exp(-0.5) * _sigmoid(w_pre)
            iclr = _sigmoid(a0_ref[:, ls] + _dot(za, aup_ref[:, ls]))
            gate = _dot(sig_zg, gup_ref[:, ls])
            kk = zk * kk_ref[:, ls]
            kk = kk / jnp.maximum(jnp.sqrt(head_sum(kk * kk)), 1e-12)
            k = zk * (1.0 + (iclr - 1.0) * ka_ref[:, ls])
            streams.append((zr, k, zv, -kk, kk * iclr, lw))
            c["keep"].append((zr, k, zv, gate))
        for j, name in enumerate(("r", "k", "v", "a", "b", "lw")):
            c[name] = [streams[q][j][s] for q, s in items]

    def cumsum():
        c["cs"] = []
        for i in rng:
            hi, mid, lo = _split3(c["lw"][i])
            c["cs"].append(_dot(tril, hi) + _dot(tril, mid) + _dot(tril, lo))

    def scale():
        cs, lw = c["cs"], c["lw"]
        g_in = [jnp.exp(cs[i]) for i in rng]
        g_ex = [jnp.exp(cs[i] - lw[i]) for i in rng]
        g_inv = [jnp.exp(-cs[i]) for i in rng]
        c["g_last"] = [jnp.exp(cs[i][CHUNK - 1:CHUNK, :]) for i in rng]
        c["rt"] = [stack(c["r"][i] * g_in[i]) for i in rng]
        c["at"] = [bf(stack(c["a"][i] * g_ex[i])) for i in rng]
        bt = [stack(c["b"][i] * g_inv[i]) for i in rng]
        kt = [stack(c["k"][i] * g_inv[i]) for i in rng]
        c["bkh_t"] = [bf(jnp.concatenate([(bt[i] * c["g_last"][i]).T, (kt[i] * c["g_last"][i]).T], axis=1))
                      for i in rng]
        c["v2"] = [bf(stack(c["v"][i])) for i in rng]
        c["bk"] = [bf(jnp.concatenate([bt[i], kt[i]], axis=0)) for i in rng]

    def gram():
        ga = [_dot_nt(c["at"][i], c["bk"][i]) for i in rng]
        gr = [_dot_nt(bf(c["rt"][i]), c["bk"][i]) for i in rng]
        c["a_ab"] = [g[:, :n] * strict for g in ga]
        c["a_ak"] = [bf(g[:, n:] * strict) for g in ga]
        c["a_rbk"] = [bf(g * incl2) for g in gr]

    def inv0():
        c["d"] = [bf(a * base) for a in c["a_ab"]]
        c["t"] = [eye + a * base for a in c["a_ab"]]
        c["p"] = [_dot(d, d) for d in c["d"]]

    def inv_double():
        xs = [_dot(bf(jnp.concatenate([t, p], axis=0)), bf(p)) for t, p in zip(c["t"], c["p"])]
        c["t"] = [t + x[:n] for t, x in zip(c["t"], xs)]
        c["p"] = [x[n:] for x in xs]

    def inv_last():
        c["t"] = [t + _dot(bf(t), bf(p)) for t, p in zip(c["t"], c["p"])]

    def level_a(msk, blk):
        def f():
            c["tb"] = [bf(t) for t in c["t"]]
            c["x"] = [bf(_dot(bf(_rows(t, blk)), bf(a * msk))) for t, a in zip(c["t"], c["a_ab"])]
        return f

    def level_b(blk):
        def f():
            c["t"] = [_merge_rows(t, _rows(t, blk) + _dot(x, tb), blk)
                      for t, x, tb in zip(c["t"], c["x"], c["tb"])]
        return f

    def apply_v():
        c["av"] = [bf(_dot(c["a_ak"][i], c["v2"][i])) for i in rng]

    def apply_t():
        c["wu"] = [bf(_dot(bf(c["t"][i]), jnp.concatenate([c["at"][i], c["av"][i]], axis=1)))
                   for i in rng]

    def assemble():
        zero = jnp.zeros((n, n), BF16)
        big = [_dot(jnp.concatenate([c["a_rbk"][i], c["bkh_t"][i]], axis=0),
                    jnp.concatenate([c["wu"][i], jnp.concatenate([zero, c["v2"][i]], axis=1)], axis=0))
               for i in rng]
        qeff = [c["rt"][i] + big[i][:n, :n] for i in rng]
        c["y0"] = [big[i][:n, n:] for i in rng]
        m = [eye * c["g_last"][i] + big[i][n:, :n] for i in rng]
        c["nn"] = [big[i][n:, n:] for i in rng]
        c["mq"] = [bf(jnp.concatenate([m[i], qeff[i]], axis=0)) for i in rng]

    def chain():
        hs = [h_ref[p] for p in pairs]
        ys = [[] for _ in pairs]
        for i in rng:
            q = items[i][0]
            hy = _dot(c["mq"][i], bf(hs[q]))
            hs[q] = hy[:n] + c["nn"][i]
            y2 = hy[n:] + c["y0"][i]
            ys[q].append(y2[:CHUNK, :] + y2[CHUNK:, :])
        for q, p in enumerate(pairs):
            h_ref[p] = hs[q]
        c["ys"] = [jnp.concatenate(y, axis=0) for y in ys]

    def post():
        for q, p in enumerate(pairs):
            ls = slice(p * PAIR, (p + 1) * PAIR)
            zr, k, zv, gate = c["keep"][q]
            y = c["ys"][q]
            mean = head_sum(y) * (1.0 / RWKV_HEAD)
            yc = y - mean
            var = head_sum(yc * yc) * (1.0 / RWKV_HEAD)
            yn = yc * lax.rsqrt(var + GN_EPS) * lnw_ref[:, ls] + lnb_ref[:, ls]
            bonus = head_sum(zr * k * rk_ref[:, ls]) * zv
            y_ref[:, ls] = ((yn + bonus) * gate).astype(y_ref.dtype)

    doublings = [inv_double] * (INV_BASE.bit_length() - 3)
    level_stages = [f for msk, blk in levels for f in (level_a(msk, blk), level_b(blk))]
    return ([pre, cumsum, scale, gram, inv0] + doublings + [inv_last] + level_stages
            + [apply_v, apply_t, assemble, chain, post])


def _rwkv_kernel(zr_ref, zk_ref, zv_ref, zwa_ref, zg_ref,
                 w0_ref, a0_ref, kk_ref, ka_ref, rk_ref, lnw_ref, lnb_ref,
                 wup_ref, aup_ref, gup_ref, y_ref, h_ref, *, tb, npair):
    @pl.when(pl.program_id(2) == 0)
    def _():
        h_ref[...] = jnp.zeros_like(h_ref)

    lane = lax.broadcasted_iota(jnp.int32, (tb, PAIR), 1)
    head0 = lane < RWKV_HEAD

    def head_sum(x):
        s0 = jnp.sum(jnp.where(head0, x, 0.0), axis=-1, keepdims=True)
        s1 = jnp.sum(jnp.where(head0, 0.0, x), axis=-1, keepdims=True)
        return jnp.where(head0, s0, s1)

    trow = lax.broadcasted_iota(jnp.int32, (CHUNK, CHUNK), 0)
    tcol = lax.broadcasted_iota(jnp.int32, (CHUNK, CHUNK), 1)
    tril = jnp.where(trow >= tcol, 1.0, 0.0).astype(BF16)
    consts = _rwkv_masks() + (tril,)
    refs = (zr_ref, zk_ref, zv_ref, w0_ref, a0_ref, kk_ref, ka_ref, rk_ref, lnw_ref, lnb_ref,
            wup_ref, aup_ref, gup_ref, y_ref, h_ref,
            jnp.tanh(zwa_ref[...]).astype(BF16), zwa_ref[...].astype(BF16),
            _sigmoid(zg_ref[...]).astype(BF16))
    for stage in _rwkv_wave(refs, list(range(npair)), tb // CHUNK, consts, head_sum):
        stage()


def _rwkv(rw, prm, wup, aup, gup, batch, seq):
    t_tok = rw.shape[0]
    tb = min(RWKV_TILE, seq)
    nt = seq // tb
    npair = RWKV_PAIRS
    wid = npair * PAIR
    ngrp = RWKV_DIM // wid

    def tok(base):
        return pl.BlockSpec((tb, wid), lambda b, p, t: (b * nt + t, base * ngrp + p))

    def lora(base, width):
        return pl.BlockSpec((tb, width), lambda b, p, t: (b * nt + t, base))

    prm_spec = pl.BlockSpec((1, wid), lambda b, p, t: (0, p))
    in_specs = [
        tok(0), tok(1), tok(2),
        lora((3 * RWKV_DIM + ZG_PAD) // 128, 128), lora(3 * RWKV_DIM // ZG_PAD, ZG_PAD),
    ] + [prm_spec] * 7 + [
        pl.BlockSpec((128, wid), lambda b, p, t: (0, p)),
        pl.BlockSpec((128, wid), lambda b, p, t: (0, p)),
        pl.BlockSpec((ZG_PAD, wid), lambda b, p, t: (0, p)),
    ]
    return pl.pallas_call(
        functools.partial(_rwkv_kernel, tb=tb, npair=npair),
        grid=(batch, ngrp, nt),
        in_specs=in_specs,
        out_specs=pl.BlockSpec((tb, wid), lambda b, p, t: (b * nt + t, p)),
        out_shape=jax.ShapeDtypeStruct((t_tok, RWKV_DIM), BF16),
        scratch_shapes=[pltpu.VMEM((npair, PAIR, PAIR), F32)],
        compiler_params=_cparams(("arbitrary", "arbitrary", "arbitrary")),
        name="rwkv7",
    )(rw, rw, rw, rw, rw, *prm, wup, aup, gup)


def _attn_kernel(q_ref, k_ref, vt_ref, o_ref, m_ref, acc_ref, s_ref, p_ref, al_ref,
                 *, tk, heads, ks, ws, ahead, dlag, behind):
    i = pl.program_id(2)
    krow = lax.broadcasted_iota(jnp.int32, (ks, ws), 0)
    qcol = lax.broadcasted_iota(jnp.int32, (ks, ws), 1)
    neg = jnp.finfo(F32).min
    nsub, nstrip = tk // ks, tk // ws
    items = [(h, g) for h in range(heads) for g in range(nstrip)]
    rng = range(len(items))
    qs = [q_ref[h, 0, :, g * ws:(g + 1) * ws] for h, g in items]

    full_units = [(sub, i) for sub in range(nsub) for i in rng]
    diag_units = [(sub, i) for sub, i in full_units if not sub * ks > items[i][1] * ws + ws - 1]
    lag = dlag + behind
    tail_units = full_units[len(full_units) - lag:]
    assert diag_units[:ahead] == full_units[:ahead] and len(diag_units) > ahead + lag

    def score(j, unit):
        sub, i = unit
        return _dot(k_ref[items[i][0], j, sub * ks:(sub + 1) * ks, :], qs[i])

    def pv(j, unit, p):
        sub, i = unit
        return _dot(vt_ref[items[i][0], j, :, sub * ks:(sub + 1) * ks], p)

    def step(j, masked):
        units = diag_units if masked else full_units
        nu = len(units)
        keep = 0 if masked else lag
        jprev = jnp.maximum(j - 1, 0)
        s_val, p_val, pv_val, al_val, pv_old = {}, {}, {}, {}, {}
        for t in range(nu + (lag if masked else 0)):
            if t + ahead < nu:
                s_val[t + ahead] = score(j, units[t + ahead])
            if t < nu:
                sub, i = units[t]
                s = s_val.pop(t) if t >= ahead else s_ref[t]
                if masked and sub * ks + ks - 1 > items[i][1] * ws:
                    s = jnp.where(krow + (sub * ks - items[i][1] * ws) <= qcol, s, neg)
                m_old = m_ref[i]
                m_new = jnp.maximum(m_old, jnp.max(s, axis=0, keepdims=True))
                al_val[t] = jnp.exp2(m_old - m_new)
                m_ref[i] = m_new
                p_val[t] = jnp.exp2(s - m_new).astype(BF16)
                if not masked and t + ahead >= nu:
                    s_ref[t + ahead - nu] = score(j + 1, full_units[t + ahead - nu])
            if t < lag:
                pv_old[t] = pv(jprev, tail_units[t], p_ref[t])
            d = t - dlag
            if 0 <= d < nu - keep:
                pv_val[d] = pv(j, units[d], p_val.pop(d))
            x = t - behind
            if 0 <= x < lag:
                i = tail_units[x][1]
                acc_ref[i] = al_ref[x] * acc_ref[i] + pv_old.pop(x)
            w = d - behind
            if 0 <= w < nu - keep:
                i = units[w][1]
                acc_ref[i] = al_val.pop(w) * acc_ref[i] + pv_val.pop(w)
        if not masked:
            for x in range(lag):
                p_ref[x] = p_val[nu - lag + x]
                al_ref[x] = al_val[nu - lag + x]

    m_ref[...] = jnp.full(m_ref.shape, neg, F32)
    acc_ref[...] = jnp.zeros(acc_ref.shape, F32)
    p_ref[...] = jnp.zeros(p_ref.shape, BF16)
    al_ref[...] = jnp.ones(al_ref.shape, F32)
    for t in range(ahead):
        s_ref[t] = score(0, full_units[t])

    def body(j, c):
        step(j, False)
        return c

    lax.fori_loop(0, i, body, 0)
    step(i, True)
    def normalized(h):
        accs = [acc_ref[h * nstrip + g] for g in range(nstrip)]
        return jnp.concatenate([a[:V_DIM] / a[V_DIM:V_DIM + 1] for a in accs], axis=1)

    outs = [jnp.concatenate([normalized(h), normalized(h + 1)], axis=0).T for h in range(0, heads, 2)]
    o_ref[...] = jnp.concatenate(outs, axis=-1).astype(o_ref.dtype)


def _attn(q5, k5, vt5, batch, seq, tk):
    nk = seq // tk
    heads = 8
    ngrp = MLA_HEADS // heads
    ks, ws = min(256, tk), min(256, tk)
    ahead, dlag, behind = 3, 2, 2
    nitem = heads * (tk // ws)
    return pl.pallas_call(
        functools.partial(_attn_kernel, tk=tk, heads=heads, ks=ks, ws=ws, ahead=ahead, dlag=dlag, behind=behind),
        grid=(batch, ngrp, nk),
        in_specs=[
            pl.BlockSpec((None, heads, 1, HEAD_PAD, tk), lambda b, g, i: (b, g, i, 0, 0)),
            pl.BlockSpec((None, heads, nk, tk, HEAD_PAD), lambda b, g, i: (b, g, 0, 0, 0)),
            pl.BlockSpec((None, heads, nk, V_EXT, tk), lambda b, g, i: (b, g, 0, 0, 0)),
        ],
        out_specs=pl.BlockSpec((tk, heads * V_DIM), lambda b, g, i: (b * nk + i, g)),
        out_shape=jax.ShapeDtypeStruct((batch * seq, MLA_HEADS * V_DIM), BF16),
        scratch_shapes=[
            pltpu.VMEM((nitem, 1, ws), F32),
            pltpu.VMEM((nitem, V_EXT, ws), F32),
            pltpu.VMEM((ahead, ks, ws), F32),
            pltpu.VMEM((dlag + behind, ks, ws), BF16),
            pltpu.VMEM((dlag + behind, 1, ws), F32),
        ],
        compiler_params=_cparams(("arbitrary", "arbitrary", "arbitrary")),
        name="mla_attn",
    )(q5, k5, vt5)


def _tail_kernel(x_ref, yr_ref, o_ref, gt_ref, wor_ref, wom_ref, wout_ref, nf_ref, wup_ref, wdn_ref,
                 nfin_ref, out_ref):
    y_a = _dot(yr_ref[...], wor_ref[...])
    y_b = _dot(o_ref[...], wom_ref[...])
    merged = _sigmoid(gt_ref[:, :D_MODEL]) * y_a + _sigmoid(gt_ref[:, D_MODEL:]) * y_b
    h = x_ref[...] + _dot(merged.astype(BF16), wout_ref[...])
    f_in = _rms(h, nf_ref[...]).astype(BF16)
    cw = 1024
    acc = h
    for c in range(D_FF // cw):
        f = _dot(f_in, wup_ref[:, c * cw:(c + 1) * cw])
        f = jnp.square(jnp.maximum(f, 0.0)).astype(BF16)
        acc = acc + _dot(f, wdn_ref[c * cw:(c + 1) * cw, :])
    out_ref[...] = _rms(acc, nfin_ref[...])


def _tail(x2, yr, o, gt, wor, wom, wout, nf, wup, wdn, nfin, seq):
    t_tok = x2.shape[0]
    tm = min(TOKEN_TILE, seq)
    full = lambda shape: pl.BlockSpec(shape, lambda i: (0,) * len(shape), pipeline_mode=pl.Buffered(1))
    return pl.pallas_call(
        _tail_kernel,
        grid=(t_tok // tm,),
        in_specs=[
            pl.BlockSpec((tm, D_MODEL), lambda i: (i, 0)),
            pl.BlockSpec((tm, RWKV_DIM), lambda i: (i, 0)),
            pl.BlockSpec((tm, MLA_HEADS * V_DIM), lambda i: (i, 0)),
            pl.BlockSpec((tm, GT_COLS), lambda i: (i, 0)),
            full((RWKV_DIM, D_MODEL)), full((MLA_HEADS * V_DIM, D_MODEL)), full((D_MODEL, D_MODEL)),
            full((1, D_MODEL)), full((D_MODEL, D_FF)), full((D_FF, D_MODEL)), full((1, D_MODEL)),
        ],
        out_specs=pl.BlockSpec((tm, D_MODEL), lambda i: (i, 0)),
        out_shape=jax.ShapeDtypeStruct((t_tok, D_MODEL), F32),
        compiler_params=_cparams(("arbitrary",)),
        name="merge_ffn",
    )(x2, yr, o, gt, wor, wom, wout, nf, wup, wdn, nfin)


def _padc(w, n):
    return jnp.pad(w, ((0, 0), (0, n - w.shape[1])))


def _padr(w, n):
    return jnp.pad(w, ((0, n - w.shape[0]), (0, 0)))


def _rot_half(w):
    half = w.shape[-1] // 2
    return jnp.concatenate([-w[..., half:], w[..., :half]], axis=-1)


def kernel(x, positions, norm_mix, w_in, mu_shift, w0, w_up, a0, a_up, g_up, k_k, k_a, r_k, ln_w, ln_b, w_o_rwkv, q_norm, w_uq, kv_norm, w_ukv, w_o_mla, w_out, norm_ffn, w_ff_up, w_ff_down, norm_final):
    batch, seq, _ = x.shape
    t_tok = batch * seq
    x2 = x.reshape(t_tok, D_MODEL)
    l = 0

    wi = w_in[l].astype(BF16)
    o = 0
    w_r3 = wi[:, o:o + 3 * RWKV_DIM]; o += 3 * RWKV_DIM
    w_zw = wi[:, o:o + DECAY_LORA]; o += DECAY_LORA
    w_za = wi[:, o:o + ICLR_LORA]; o += ICLR_LORA
    w_zg = wi[:, o:o + GATE_LORA]; o += GATE_LORA
    w_cq = wi[:, o:o + Q_LORA]; o += Q_LORA
    w_ckv = wi[:, o:o + KV_LORA]; o += KV_LORA
    w_kr = wi[:, o:o + ROPE_DIM]; o += ROPE_DIM
    w_gate = wi[:, o:o + 2 * D_MODEL]
    zeros64 = jnp.zeros((D_MODEL, NOPE_DIM), wi.dtype)
    w_krb = jnp.concatenate([zeros64, w_kr, _rot_half(w_kr)], axis=1)
    w_a = jnp.concatenate([
        w_r3, _padc(w_zg, ZG_PAD), w_zw, w_za, w_cq, w_ckv, w_krb, w_gate], axis=1)
    mu = mu_shift[l]
    o = 3 * RWKV_DIM
    lo = o + DECAY_LORA + ICLR_LORA
    mu_a = jnp.concatenate([mu[:o], jnp.pad(mu[lo:], (0, ZG_PAD - GATE_LORA)), mu[o:lo]])[None, :]

    half = ROPE_DIM // 2
    inv_freq = 1.0 / (ROPE_THETA ** (jnp.arange(half, dtype=F32) * (2.0 / ROPE_DIM)))
    scale = (NOPE_DIM + ROPE_DIM) ** -0.5 * math.log2(math.e)
    wq = w_uq[l].reshape(Q_LORA, MLA_HEADS, NOPE_DIM + ROPE_DIM) * scale
    wq_p = jnp.pad(wq, ((0, 0), (0, 0), (0, HEAD_PAD - NOPE_DIM - ROPE_DIM)))
    wqt = jnp.transpose(wq_p, (1, 2, 0)).reshape(MLA_HEADS * HEAD_PAD, Q_LORA)
    wqrt = jnp.transpose(_rot_half(wq[..., NOPE_DIM:]), (1, 2, 0)).reshape(MLA_HEADS * ROPE_DIM, Q_LORA)
    fcol = jnp.concatenate([inv_freq, inv_freq])[:, None]
    wkv = w_ukv[l].reshape(KV_LORA, MLA_HEADS, NOPE_DIM + V_DIM)
    wk_p = jnp.pad(wkv[..., :NOPE_DIM], ((0, 0), (0, 0), (0, HEAD_PAD - NOPE_DIM)))
    wvt = jnp.transpose(wkv[..., NOPE_DIM:], (1, 2, 0)).reshape(MLA_HEADS * V_DIM, KV_LORA)
    tk = min(TOKEN_TILE, seq)

    rw, gt, q5, k5, vt5 = _inproj(
        x2, norm_mix[l][None, :], w_a, mu_a, positions.reshape(t_tok // tk, 1, tk), fcol,
        q_norm[l][None, :], kv_norm[l][None, :], wqt.astype(BF16), wqrt.astype(BF16),
        wk_p.reshape(KV_LORA, -1).astype(BF16), wvt.astype(BF16), batch, seq)

    prm = [p[None, :] for p in (w0[l], a0[l], k_k[l], k_a[l], r_k[l].reshape(-1), ln_w[l], ln_b[l])]
    aup = jnp.concatenate([jnp.zeros_like(a_up[l]), a_up[l]], axis=0)
    yr = _rwkv(rw, prm, _padr(w_up[l], 128).astype(BF16), aup.astype(BF16),
               _padr(g_up[l], ZG_PAD).astype(BF16), batch, seq)

    o_att = _attn(q5, k5, vt5, batch, seq, tk)

    out = _tail(x2, yr, o_att, gt, w_o_rwkv[l].astype(BF16), w_o_mla[l].astype(BF16),
                w_out[l].astype(BF16), norm_ffn[l][None, :], w_ff_up[l].astype(BF16),
                w_ff_down[l].astype(BF16), norm_final[None, :], seq)
    return out.reshape(batch, seq, D_MODEL)
```

```python
import functools
import math

import jax
import jax.numpy as jnp
from jax import lax
from jax.experimental import pallas as pl
from jax.experimental.pallas import tpu as pltpu

F32 = jnp.float32
BF16 = jnp.bfloat16

D_MODEL = 1024
NORM_EPS = 1e-6
RWKV_HEAD = 64
RWKV_HEADS = 8
RWKV_DIM = RWKV_HEADS * RWKV_HEAD
DECAY_LORA = 64
ICLR_LORA = 64
GATE_LORA = 160
GN_EPS = 64e-5
MLA_HEADS = 8
Q_LORA = 256
KV_LORA = 128
NOPE_DIM = 64
ROPE_DIM = 32
V_DIM = 64
ROPE_THETA = 10000.0
D_FF = 4 * D_MODEL

LANE = 128
CHUNK = 64
INV_BASE = 8
PAIR = 2 * RWKV_HEAD
HEAD_PAD = 128
V_EXT = V_DIM + 16

ZG_PAD = 256
RW_COLS = 3 * RWKV_DIM + ZG_PAD + 128
ML_COLS = Q_LORA + KV_LORA + 128
GT_COLS = 2 * D_MODEL

VMEM_BYTES_V7X = 64 * 1024 * 1024
VMEM_LIMIT = VMEM_BYTES_V7X * 7 // 8
TOKEN_TILE = 512
RWKV_TILE = 4 * CHUNK
RWKV_PAIRS = RWKV_DIM // PAIR


def _cparams(sem):
    return pltpu.CompilerParams(dimension_semantics=sem, vmem_limit_bytes=VMEM_LIMIT)


def _sigmoid(x):
    return 1.0 / (1.0 + jnp.exp(-x))


def _rms(x, g):
    ms = jnp.mean(x * x, axis=-1, keepdims=True)
    return x * lax.rsqrt(ms + NORM_EPS) * g


def _dot(a, b):
    return jnp.dot(a, b, preferred_element_type=F32)


def _dot_nt(a, b):
    return lax.dot_general(a, b, (((1,), (1,)), ((), ())), preferred_element_type=F32)


def _dot_tn(a, b):
    return lax.dot_general(a, b, (((0,), (0,)), ((), ())), preferred_element_type=F32)


def _mla_prep_parts(ml, posrow_ref, fcol_ref, qn_ref, kvn_ref, wqt_ref, wqrt_ref, wk_ref, wvt_ref,
                    q_ref, k_ref, vt_ref, tk):
    c = {}
    rope = slice(NOPE_DIM, NOPE_DIM + ROPE_DIM)

    def trig():
        ang_t = fcol_ref[...] * posrow_ref[...].astype(F32)
        c["cos_t"] = jnp.cos(ang_t)
        c["sin_t"] = jnp.sin(ang_t)
        c["cq"] = _rms(ml[:, :Q_LORA], qn_ref[...]).astype(BF16)
        c["ckv"] = _rms(ml[:, Q_LORA:Q_LORA + KV_LORA], kvn_ref[...]).astype(BF16)

    def project():
        q_all = _dot_nt(jnp.concatenate([wqt_ref[...], wqrt_ref[...]], axis=0), c["cq"])
        c["qf"] = q_all[:MLA_HEADS * HEAD_PAD]
        c["qr"] = q_all[MLA_HEADS * HEAD_PAD:]
        c["kf"] = _dot(c["ckv"], wk_ref[...])
        c["vt"] = _dot_nt(wvt_ref[...], c["ckv"])

    def k_rope():
        zlo = jnp.zeros((NOPE_DIM, tk), F32)
        zhi = jnp.zeros((HEAD_PAD - NOPE_DIM - ROPE_DIM, tk), F32)
        cosf = jnp.concatenate([zlo, c["cos_t"], zhi], axis=0).T
        sinf = jnp.concatenate([zlo, c["sin_t"], zhi], axis=0).T
        krb = ml[:, Q_LORA + KV_LORA:Q_LORA + KV_LORA + HEAD_PAD]
        c["k_rope"] = krb * cosf + pltpu.roll(krb, HEAD_PAD - ROPE_DIM, 1) * sinf

    def heads(lo, hi):
        def f():
            for h in range(lo, hi):
                hs = slice(h * HEAD_PAD, (h + 1) * HEAD_PAD)
                qf_t = c["qf"][hs]
                qr_t = c["qr"][h * ROPE_DIM:(h + 1) * ROPE_DIM]
                q_t = jnp.concatenate([qf_t[:NOPE_DIM], qf_t[rope] * c["cos_t"] + qr_t * c["sin_t"],
                                       qf_t[NOPE_DIM + ROPE_DIM:]], axis=0)
                q_ref[h] = q_t.astype(q_ref.dtype)
                k_ref[h] = (c["kf"][:, hs] + c["k_rope"]).astype(k_ref.dtype)
                vt_ref[h, :V_DIM] = c["vt"][h * V_DIM:(h + 1) * V_DIM].astype(vt_ref.dtype)
                vt_ref[h, V_DIM:] = jnp.ones((V_EXT - V_DIM, tk), vt_ref.dtype)
        return f

    half = MLA_HEADS // 2
    return [trig, project, k_rope, heads(0, half), heads(half, MLA_HEADS)]


def _inproj_kernel(x_ref, g_ref, w_ref, mu_ref, posrow_ref, fcol_ref, qn_ref, kvn_ref,
                   wqt_ref, wqrt_ref, wk_ref, wvt_ref,
                   rw_ref, gt_ref, q_ref, k_ref, vt_ref, carry_ref, *, tiles_per_seq, tm):
    i = pl.program_id(0)
    u = _rms(x_ref[...], g_ref[...]).astype(BF16)

    @pl.when(i % tiles_per_seq == 0)
    def _():
        carry_ref[...] = jnp.zeros_like(carry_ref)

    def shift_store(cs, z):
        prev = pltpu.roll(z, 1, 0)
        row0 = lax.broadcasted_iota(jnp.int32, z.shape, 0) == 0
        prev = jnp.where(row0, carry_ref[7:8, cs], prev)
        carry_ref[:, cs] = z[tm - 8:tm, :]
        rw_ref[:, cs] = z + (prev - z) * mu_ref[:, cs]

    def gate_store(cs, z):
        gt_ref[:, cs] = z

    cw = 512
    jobs = [(c0, slice(c0, min(c0 + cw, RW_COLS)), shift_store) for c0 in range(0, RW_COLS, cw)]
    jobs += [(RW_COLS + ML_COLS + c0, slice(c0, c0 + cw), gate_store) for c0 in range(0, GT_COLS, cw)]
    ml = _dot(u, w_ref[:, RW_COLS:RW_COLS + ML_COLS])
    fillers = _mla_prep_parts(ml, posrow_ref, fcol_ref, qn_ref, kvn_ref, wqt_ref, wqrt_ref, wk_ref, wvt_ref,
                              q_ref, k_ref, vt_ref, tm)
    mm = lambda job: _dot(u, w_ref[:, job[0]:job[0] + (job[1].stop - job[1].start)])
    z = mm(jobs[0])
    for n, job in enumerate(jobs):
        z_next = mm(jobs[n + 1]) if n + 1 < len(jobs) else None
        if n < len(fillers):
            fillers[n]()
        job[2](job[1], z)
        z = z_next


def _inproj(x2, g, w_a, mu_a, posrow, fcol, qn, kvn, wqt, wqrt, wk, wvt, batch, seq):
    t_tok = x2.shape[0]
    tm = min(TOKEN_TILE, seq)
    nk = seq // tm
    ncol = RW_COLS + ML_COLS + GT_COLS
    kern = functools.partial(_inproj_kernel, tiles_per_seq=nk, tm=tm)
    full = lambda shape: pl.BlockSpec(shape, lambda i: (0,) * len(shape))
    head_major = lambda rows, cols: pl.BlockSpec((None, MLA_HEADS, None, rows, cols),
                                                 lambda i: (i // nk, 0, i % nk, 0, 0))
    return pl.pallas_call(
        kern,
        grid=(t_tok // tm,),
        in_specs=[
            pl.BlockSpec((tm, D_MODEL), lambda i: (i, 0)),
            full((1, D_MODEL)),
            pl.BlockSpec((D_MODEL, ncol), lambda i: (0, 0), pipeline_mode=pl.Buffered(1)),
            full((1, RW_COLS)),
            pl.BlockSpec((None, 1, tm), lambda i: (i, 0, 0)),
            full((ROPE_DIM, 1)), full((1, Q_LORA)), full((1, KV_LORA)),
            full((MLA_HEADS * HEAD_PAD, Q_LORA)), full((MLA_HEADS * ROPE_DIM, Q_LORA)),
            full((KV_LORA, MLA_HEADS * HEAD_PAD)), full((MLA_HEADS * V_DIM, KV_LORA)),
        ],
        out_specs=[
            pl.BlockSpec((tm, RW_COLS), lambda i: (i, 0)),
            pl.BlockSpec((tm, GT_COLS), lambda i: (i, 0)),
            head_major(HEAD_PAD, tm), head_major(tm, HEAD_PAD), head_major(V_EXT, tm),
        ],
        out_shape=[
            jax.ShapeDtypeStruct((t_tok, RW_COLS), F32),
            jax.ShapeDtypeStruct((t_tok, GT_COLS), F32),
            jax.ShapeDtypeStruct((batch, MLA_HEADS, nk, HEAD_PAD, tm), BF16),
            jax.ShapeDtypeStruct((batch, MLA_HEADS, nk, tm, HEAD_PAD), BF16),
            jax.ShapeDtypeStruct((batch, MLA_HEADS, nk, V_EXT, tm), BF16),
        ],
        scratch_shapes=[pltpu.VMEM((8, RW_COLS), F32)],
        compiler_params=_cparams(("arbitrary",)),
        name="inproj",
    )(x2, g, w_a, mu_a, posrow, fcol, qn, kvn, wqt, wqrt, wk, wvt)


def _split3(x):
    hi = x.astype(BF16)
    r1 = x - hi.astype(F32)
    mid = r1.astype(BF16)
    lo = (r1 - mid.astype(F32)).astype(BF16)
    return hi, mid, lo


def _rwkv_masks():
    n = 2 * CHUNK
    row = lax.broadcasted_iota(jnp.int32, (n, n), 0)
    col = lax.broadcasted_iota(jnp.int32, (n, n), 1)
    same = (row // CHUNK) == (col // CHUNK)
    strict = jnp.where(same & (row > col), 1.0, 0.0).astype(F32)
    incl = jnp.where(same & (row >= col), 1.0, 0.0).astype(F32)
    base = jnp.where((row // INV_BASE) == (col // INV_BASE), 1.0, 0.0).astype(F32)
    levels = []
    blk = INV_BASE
    while blk < CHUNK:
        levels.append((jnp.where(((row // (2 * blk)) == (col // (2 * blk))) & ((row // blk) > (col // blk)),
                                 1.0, 0.0).astype(F32), blk))
        blk *= 2
    eye = jnp.where(row == col, 1.0, 0.0).astype(F32)
    headsel = (row // CHUNK) == (col // RWKV_HEAD)
    return strict, incl, base, levels, eye, headsel


def _rows(x, blk):
    return jnp.concatenate([x[s:s + blk] for s in range(blk, x.shape[0], 2 * blk)], axis=0)


def _merge_rows(x, odd, blk):
    parts = []
    for j, s in enumerate(range(0, x.shape[0], 2 * blk)):
        parts += [x[s:s + blk], odd[j * blk:(j + 1) * blk]]
    return jnp.concatenate(parts, axis=0)


def _rwkv_wave(refs, pairs, nchunk, consts, head_sum):
    (zr_ref, zk_ref, zv_ref, w0_ref, a0_ref, kk_ref, ka_ref, rk_ref, lnw_ref, lnb_ref,
     wup_ref, aup_ref, gup_ref, y_ref, h_ref, tanh_zw, za, sig_zg) = refs
    strict, incl, base, levels, eye, headsel, tril = consts
    incl2 = jnp.concatenate([incl, incl], axis=1)
    n = 2 * CHUNK
    bf = lambda x: x.astype(BF16)
    items = [(q, slice(c * CHUNK, (c + 1) * CHUNK)) for c in range(nchunk) for q in range(len(pairs))]
    rng = range(len(items))
    c = {}

    def stack(x):
        return jnp.where(headsel, jnp.concatenate([x, x], axis=0), 0.0)

    def pre():
        c["keep"] = []
        streams = []
        for p in pairs:
            ls = slice(p * PAIR, (p + 1) * PAIR)
            zr, zk, zv = zr_ref[:, ls], zk_ref[:, ls], zv_ref[:, ls]
            w_pre = w0_ref[:, ls] + _dot(tanh_zw, wup_ref[:, ls])
            lw = -math.exp(-0.5) * _sigmoid(w_pre)
            iclr = _sigmoid(a0_ref[:, ls] + _dot(za, aup_ref[:, ls]))
            gate = _dot(sig_zg, gup_ref[:, ls])
            kk = zk * kk_ref[:, ls]
            kk = kk / jnp.maximum(jnp.sqrt(head_sum(kk * kk)), 1e-12)
            k = zk * (1.0 + (iclr - 1.0) * ka_ref[:, ls])
            streams.append((zr, k, zv, -kk, kk * iclr, lw))
            c["keep"].append((zr, k, zv, gate))
        for j, name in enumerate(("r", "k", "v", "a", "b", "lw")):
            c[name] = [streams[q][j][s] for q, s in items]

    def cumsum():
        c["cs"] = []
        for i in rng:
            hi, mid, lo = _split3(c["lw"][i])
            c["cs"].append(_dot(tril, hi) + _dot(tril, mid) + _dot(tril, lo))

    def scale():
        cs, lw = c["cs"], c["lw"]
        g_in = [jnp.exp(cs[i]) for i in rng]
        g_ex = [jnp.exp(cs[i] - lw[i]) for i in rng]
        g_inv = [jnp.exp(-cs[i]) for i in rng]
        c["g_last"] = [jnp.exp(cs[i][CHUNK - 1:CHUNK, :]) for i in rng]
        c["rt"] = [stack(c["r"][i] * g_in[i]) for i in rng]
        c["at"] = [bf(stack(c["a"][i] * g_ex[i])) for i in rng]
        bt = [stack(c["b"][i] * g_inv[i]) for i in rng]
        kt = [stack(c["k"][i] * g_inv[i]) for i in rng]
        c["bkh_t"] = [bf(jnp.concatenate([(bt[i] * c["g_last"][i]).T, (kt[i] * c["g_last"][i]).T], axis=1))
                      for i in rng]
        c["v2"] = [bf(stack(c["v"][i])) for i in rng]
        c["bk"] = [bf(jnp.concatenate([bt[i], kt[i]], axis=0)) for i in rng]

    def gram():
        g = [_dot_nt(jnp.concatenate([c["at"][i], bf(c["rt"][i])], axis=0), c["bk"][i]) for i in rng]
        ga = [x[:n] for x in g]
        gr = [x[n:] for x in g]
        c["a_ab"] = [g[:, :n] * strict for g in ga]
        c["a_ak"] = [bf(g[:, n:] * strict) for g in ga]
        c["a_rbk"] = [bf(g * incl2) for g in gr]

    def inv0():
        c["d"] = [bf(a * base) for a in c["a_ab"]]
        c["t"] = [eye + a * base for a in c["a_ab"]]
        c["p"] = [_dot(d, d) for d in c["d"]]

    def inv_double():
        xs = [_dot(bf(jnp.concatenate([t, p], axis=0)), bf(p)) for t, p in zip(c["t"], c["p"])]
        c["t"] = [t + x[:n] for t, x in zip(c["t"], xs)]
        c["p"] = [x[n:] for x in xs]

    def inv_last():
        c["t"] = [t + _dot(bf(t), bf(p)) for t, p in zip(c["t"], c["p"])]

    def level_a(msk, blk):
        def f():
            c["tb"] = [bf(t) for t in c["t"]]
            c["x"] = [bf(_dot(bf(_rows(t, blk)), bf(a * msk))) for t, a in zip(c["t"], c["a_ab"])]
        return f

    def level_b(blk):
        def f():
            c["t"] = [_merge_rows(t, _rows(t, blk) + _dot(x, tb), blk)
                      for t, x, tb in zip(c["t"], c["x"], c["tb"])]
        return f

    def apply_v():
        c["av"] = [bf(_dot(c["a_ak"][i], c["v2"][i])) for i in rng]

    def apply_t():
        c["wu"] = [bf(_dot(bf(c["t"][i]), jnp.concatenate([c["at"][i], c["av"][i]], axis=1)))
                   for i in rng]

    def assemble():
        zero = jnp.zeros((n, n), BF16)
        big = [_dot(jnp.concatenate([c["a_rbk"][i], c["bkh_t"][i]], axis=0),
                    jnp.concatenate([c["wu"][i], jnp.concatenate([zero, c["v2"][i]], axis=1)], axis=0))
               for i in rng]
        qeff = [c["rt"][i] + big[i][:n, :n] for i in rng]
        c["y0"] = [big[i][:n, n:] for i in rng]
        m = [eye * c["g_last"][i] + big[i][n:, :n] for i in rng]
        c["nn"] = [big[i][n:, n:] for i in rng]
        c["mq"] = [bf(jnp.concatenate([m[i], qeff[i]], axis=0)) for i in rng]

    def chain():
        hs = [h_ref[p] for p in pairs]
        ys = [[] for _ in pairs]
        for i in rng:
            q = items[i][0]
            hy = _dot(c["mq"][i], bf(hs[q]))
            hs[q] = hy[:n] + c["nn"][i]
            y2 = hy[n:] + c["y0"][i]
            ys[q].append(y2[:CHUNK, :] + y2[CHUNK:, :])
        for q, p in enumerate(pairs):
            h_ref[p] = hs[q]
        c["ys"] = [jnp.concatenate(y, axis=0) for y in ys]

    def post():
        for q, p in enumerate(pairs):
            ls = slice(p * PAIR, (p + 1) * PAIR)
            zr, k, zv, gate = c["keep"][q]
            y = c["ys"][q]
            mean = head_sum(y) * (1.0 / RWKV_HEAD)
            yc = y - mean
            var = head_sum(yc * yc) * (1.0 / RWKV_HEAD)
            yn = yc * lax.rsqrt(var + GN_EPS) * lnw_ref[:, ls] + lnb_ref[:, ls]
            bonus = head_sum(zr * k * rk_ref[:, ls]) * zv
            y_ref[:, ls] = ((yn + bonus) * gate).astype(y_ref.dtype)

    doublings = [inv_double] * (INV_BASE.bit_length() - 3)
    level_stages = [f for msk, blk in levels for f in (level_a(msk, blk), level_b(blk))]
    return ([pre, cumsum, scale, gram, inv0] + doublings + [inv_last] + level_stages
            + [apply_v, apply_t, assemble, chain, post])


def _rwkv_kernel(zr_ref, zk_ref, zv_ref, zwa_ref, zg_ref,
                 w0_ref, a0_ref, kk_ref, ka_ref, rk_ref, lnw_ref, lnb_ref,
                 wup_ref, aup_ref, gup_ref, y_ref, h_ref, *, tb, npair):
    @pl.when(pl.program_id(2) == 0)
    def _():
        h_ref[...] = jnp.zeros_like(h_ref)

    lane = lax.broadcasted_iota(jnp.int32, (tb, PAIR), 1)
    head0 = lane < RWKV_HEAD

    def head_sum(x):
        s0 = jnp.sum(jnp.where(head0, x, 0.0), axis=-1, keepdims=True)
        s1 = jnp.sum(jnp.where(head0, 0.0, x), axis=-1, keepdims=True)
        return jnp.where(head0, s0, s1)

    trow = lax.broadcasted_iota(jnp.int32, (CHUNK, CHUNK), 0)
    tcol = lax.broadcasted_iota(jnp.int32, (CHUNK, CHUNK), 1)
    tril = jnp.where(trow >= tcol, 1.0, 0.0).astype(BF16)
    consts = _rwkv_masks() + (tril,)
    refs = (zr_ref, zk_ref, zv_ref, w0_ref, a0_ref, kk_ref, ka_ref, rk_ref, lnw_ref, lnb_ref,
            wup_ref, aup_ref, gup_ref, y_ref, h_ref,
            jnp.tanh(zwa_ref[...]).astype(BF16), zwa_ref[...].astype(BF16),
            _sigmoid(zg_ref[...]).astype(BF16))
    for stage in _rwkv_wave(refs, list(range(npair)), tb // CHUNK, consts, head_sum):
        stage()


def _rwkv(rw, prm, wup, aup, gup, batch, seq):
    t_tok = rw.shape[0]
    tb = min(RWKV_TILE, seq)
    nt = seq // tb
    npair = RWKV_PAIRS
    wid = npair * PAIR
    ngrp = RWKV_DIM // wid

    def tok(base):
        return pl.BlockSpec((tb, wid), lambda b, p, t: (b * nt + t, base * ngrp + p))

    def lora(base, width):
        return pl.BlockSpec((tb, width), lambda b, p, t: (b * nt + t, base))

    prm_spec = pl.BlockSpec((1, wid), lambda b, p, t: (0, p))
    in_specs = [
        tok(0), tok(1), tok(2),
        lora((3 * RWKV_DIM + ZG_PAD) // 128, 128), lora(3 * RWKV_DIM // ZG_PAD, ZG_PAD),
    ] + [prm_spec] * 7 + [
        pl.BlockSpec((128, wid), lambda b, p, t: (0, p)),
        pl.BlockSpec((128, wid), lambda b, p, t: (0, p)),
        pl.BlockSpec((ZG_PAD, wid), lambda b, p, t: (0, p)),
    ]
    return pl.pallas_call(
        functools.partial(_rwkv_kernel, tb=tb, npair=npair),
        grid=(batch, ngrp, nt),
        in_specs=in_specs,
        out_specs=pl.BlockSpec((tb, wid), lambda b, p, t: (b * nt + t, p)),
        out_shape=jax.ShapeDtypeStruct((t_tok, RWKV_DIM), BF16),
        scratch_shapes=[pltpu.VMEM((npair, PAIR, PAIR), F32)],
        compiler_params=_cparams(("arbitrary", "arbitrary", "arbitrary")),
        name="rwkv7",
    )(rw, rw, rw, rw, rw, *prm, wup, aup, gup)


def _attn_kernel(q_ref, k_ref, vt_ref, o_ref, m_ref, acc_ref, s_ref, p_ref, al_ref,
                 *, tk, heads, ks, ws, ahead, dlag, behind):
    i = pl.program_id(2)
    krow = lax.broadcasted_iota(jnp.int32, (ks, ws), 0)
    qcol = lax.broadcasted_iota(jnp.int32, (ks, ws), 1)
    neg = jnp.finfo(F32).min
    nsub, nstrip = tk // ks, tk // ws
    items = [(h, g) for h in range(heads) for g in range(nstrip)]
    rng = range(len(items))
    qs = [q_ref[h, 0, :, g * ws:(g + 1) * ws] for h, g in items]

    full_units = [(sub, i) for sub in range(nsub) for i in rng]
    diag_units = [(sub, i) for sub, i in full_units if not sub * ks > items[i][1] * ws + ws - 1]
    lag = dlag + behind
    tail_units = full_units[len(full_units) - lag:]
    assert diag_units[:ahead] == full_units[:ahead] and len(diag_units) > ahead + lag

    def score(j, unit):
        sub, i = unit
        return _dot(k_ref[items[i][0], j, sub * ks:(sub + 1) * ks, :], qs[i])

    def pv(j, unit, p):
        sub, i = unit
        return _dot(vt_ref[items[i][0], j, :, sub * ks:(sub + 1) * ks], p)

    def step(j, masked):
        units = diag_units if masked else full_units
        nu = len(units)
        keep = 0 if masked else lag
        jprev = jnp.maximum(j - 1, 0)
        s_val, p_val, pv_val, al_val, pv_old = {}, {}, {}, {}, {}
        for t in range(nu + (lag if masked else 0)):
            if t + ahead < nu:
                s_val[t + ahead] = score(j, units[t + ahead])
            if t < nu:
                sub, i = units[t]
                s = s_val.pop(t) if t >= ahead else s_ref[t]
                if masked and sub * ks + ks - 1 > items[i][1] * ws:
                    s = jnp.where(krow + (sub * ks - items[i][1] * ws) <= qcol, s, neg)
                m_old = m_ref[i]
                m_new = jnp.maximum(m_old, jnp.max(s, axis=0, keepdims=True))
                al_val[t] = jnp.exp2(m_old - m_new)
                m_ref[i] = m_new
                p_val[t] = jnp.exp2(s - m_new).astype(BF16)
                if not masked and t + ahead >= nu:
                    s_ref[t + ahead - nu] = score(j + 1, full_units[t + ahead - nu])
            if t < lag:
                pv_old[t] = pv(jprev, tail_units[t], p_ref[t])
            d = t - dlag
            if 0 <= d < nu - keep:
                pv_val[d] = pv(j, units[d], p_val.pop(d))
            x = t - behind
            if 0 <= x < lag:
                i = tail_units[x][1]
                acc_ref[i] = al_ref[x] * acc_ref[i] + pv_old.pop(x)
            w = d - behind
            if 0 <= w < nu - keep:
                i = units[w][1]
                acc_ref[i] = al_val.pop(w) * acc_ref[i] + pv_val.pop(w)
        if not masked:
            for x in range(lag):
                p_ref[x] = p_val[nu - lag + x]
                al_ref[x] = al_val[nu - lag + x]

    m_ref[...] = jnp.full(m_ref.shape, neg, F32)
    acc_ref[...] = jnp.zeros(acc_ref.shape, F32)
    p_ref[...] = jnp.zeros(p_ref.shape, BF16)
    al_ref[...] = jnp.ones(al_ref.shape, F32)
    for t in range(ahead):
        s_ref[t] = score(0, full_units[t])

    def body(j, c):
        step(j, False)
        return c

    lax.fori_loop(0, i, body, 0)
    step(i, True)
    def normalized(h):
        accs = [acc_ref[h * nstrip + g] for g in range(nstrip)]
        return jnp.concatenate([a[:V_DIM] / a[V_DIM:V_DIM + 1] for a in accs], axis=1)

    outs = [jnp.concatenate([normalized(h), normalized(h + 1)], axis=0).T for h in range(0, heads, 2)]
    o_ref[...] = jnp.concatenate(outs, axis=-1).astype(o_ref.dtype)


def _attn(q5, k5, vt5, batch, seq, tk):
    nk = seq // tk
    heads = 8
    ngrp = MLA_HEADS // heads
    ks, ws = min(256, tk), min(256, tk)
    ahead, dlag, behind = 3, 2, 2
    nitem = heads * (tk // ws)
    return pl.pallas_call(
        functools.partial(_attn_kernel, tk=tk, heads=heads, ks=ks, ws=ws, ahead=ahead, dlag=dlag, behind=behind),
        grid=(batch, ngrp, nk),
        in_specs=[
            pl.BlockSpec((None, heads, 1, HEAD_PAD, tk), lambda b, g, i: (b, g, i, 0, 0)),
            pl.BlockSpec((None, heads, nk, tk, HEAD_PAD), lambda b, g, i: (b, g, 0, 0, 0)),
            pl.BlockSpec((None, heads, nk, V_EXT, tk), lambda b, g, i: (b, g, 0, 0, 0)),
        ],
        out_specs=pl.BlockSpec((tk, heads * V_DIM), lambda b, g, i: (b * nk + i, g)),
        out_shape=jax.ShapeDtypeStruct((batch * seq, MLA_HEADS * V_DIM), BF16),
        scratch_shapes=[
            pltpu.VMEM((nitem, 1, ws), F32),
            pltpu.VMEM((nitem, V_EXT, ws), F32),
            pltpu.VMEM((ahead, ks, ws), F32),
            pltpu.VMEM((dlag + behind, ks, ws), BF16),
            pltpu.VMEM((dlag + behind, 1, ws), F32),
        ],
        compiler_params=_cparams(("arbitrary", "arbitrary", "arbitrary")),
        name="mla_attn",
    )(q5, k5, vt5)


def _tail_kernel(x_ref, yr_ref, o_ref, gt_ref, wor_ref, wom_ref, wout_ref, nf_ref, wup_ref, wdn_ref,
                 nfin_ref, out_ref):
    y_a = _dot(yr_ref[...], wor_ref[...])
    y_b = _dot(o_ref[...], wom_ref[...])
    merged = _sigmoid(gt_ref[:, :D_MODEL]) * y_a + _sigmoid(gt_ref[:, D_MODEL:]) * y_b
    h = x_ref[...] + _dot(merged.astype(BF16), wout_ref[...])
    f_in = _rms(h, nf_ref[...]).astype(BF16)
    cw = 1024
    acc = h
    for c in range(D_FF // cw):
        f = _dot(f_in, wup_ref[:, c * cw:(c + 1) * cw])
        f = jnp.square(jnp.maximum(f, 0.0)).astype(BF16)
        acc = acc + _dot(f, wdn_ref[c * cw:(c + 1) * cw, :])
    out_ref[...] = _rms(acc, nfin_ref[...])


def _tail(x2, yr, o, gt, wor, wom, wout, nf, wup, wdn, nfin, seq):
    t_tok = x2.shape[0]
    tm = min(TOKEN_TILE, seq)
    full = lambda shape: pl.BlockSpec(shape, lambda i: (0,) * len(shape), pipeline_mode=pl.Buffered(1))
    return pl.pallas_call(
        _tail_kernel,
        grid=(t_tok // tm,),
        in_specs=[
            pl.BlockSpec((tm, D_MODEL), lambda i: (i, 0)),
            pl.BlockSpec((tm, RWKV_DIM), lambda i: (i, 0)),
            pl.BlockSpec((tm, MLA_HEADS * V_DIM), lambda i: (i, 0)),
            pl.BlockSpec((tm, GT_COLS), lambda i: (i, 0)),
            full((RWKV_DIM, D_MODEL)), full((MLA_HEADS * V_DIM, D_MODEL)), full((D_MODEL, D_MODEL)),
            full((1, D_MODEL)), full((D_MODEL, D_FF)), full((D_FF, D_MODEL)), full((1, D_MODEL)),
        ],
        out_specs=pl.BlockSpec((tm, D_MODEL), lambda i: (i, 0)),
        out_shape=jax.ShapeDtypeStruct((t_tok, D_MODEL), F32),
        compiler_params=_cparams(("arbitrary",)),
        name="merge_ffn",
    )(x2, yr, o, gt, wor, wom, wout, nf, wup, wdn, nfin)


def _padc(w, n):
    return jnp.pad(w, ((0, 0), (0, n - w.shape[1])))


def _padr(w, n):
    return jnp.pad(w, ((0, n - w.shape[0]), (0, 0)))


def _rot_half(w):
    half = w.shape[-1] // 2
    return jnp.concatenate([-w[..., half:], w[..., :half]], axis=-1)


def kernel(x, positions, norm_mix, w_in, mu_shift, w0, w_up, a0, a_up, g_up, k_k, k_a, r_k, ln_w, ln_b, w_o_rwkv, q_norm, w_uq, kv_norm, w_ukv, w_o_mla, w_out, norm_ffn, w_ff_up, w_ff_down, norm_final):
    batch, seq, _ = x.shape
    t_tok = batch * seq
    x2 = x.reshape(t_tok, D_MODEL)
    l = 0

    wi = w_in[l].astype(BF16)
    o = 0
    w_r3 = wi[:, o:o + 3 * RWKV_DIM]; o += 3 * RWKV_DIM
    w_zw = wi[:, o:o + DECAY_LORA]; o += DECAY_LORA
    w_za = wi[:, o:o + ICLR_LORA]; o += ICLR_LORA
    w_zg = wi[:, o:o + GATE_LORA]; o += GATE_LORA
    w_cq = wi[:, o:o + Q_LORA]; o += Q_LORA
    w_ckv = wi[:, o:o + KV_LORA]; o += KV_LORA
    w_kr = wi[:, o:o + ROPE_DIM]; o += ROPE_DIM
    w_gate = wi[:, o:o + 2 * D_MODEL]
    zeros64 = jnp.zeros((D_MODEL, NOPE_DIM), wi.dtype)
    w_krb = jnp.concatenate([zeros64, w_kr, _rot_half(w_kr)], axis=1)
    w_a = jnp.concatenate([
        w_r3, _padc(w_zg, ZG_PAD), w_zw, w_za, w_cq, w_ckv, w_krb, w_gate], axis=1)
    mu = mu_shift[l]
    o = 3 * RWKV_DIM
    lo = o + DECAY_LORA + ICLR_LORA
    mu_a = jnp.concatenate([mu[:o], jnp.pad(mu[lo:], (0, ZG_PAD - GATE_LORA)), mu[o:lo]])[None, :]

    half = ROPE_DIM // 2
    inv_freq = 1.0 / (ROPE_THETA ** (jnp.arange(half, dtype=F32) * (2.0 / ROPE_DIM)))
    scale = (NOPE_DIM + ROPE_DIM) ** -0.5 * math.log2(math.e)
    wq = w_uq[l].reshape(Q_LORA, MLA_HEADS, NOPE_DIM + ROPE_DIM) * scale
    wq_p = jnp.pad(wq, ((0, 0), (0, 0), (0, HEAD_PAD - NOPE_DIM - ROPE_DIM)))
    wqt = jnp.transpose(wq_p, (1, 2, 0)).reshape(MLA_HEADS * HEAD_PAD, Q_LORA)
    wqrt = jnp.transpose(_rot_half(wq[..., NOPE_DIM:]), (1, 2, 0)).reshape(MLA_HEADS * ROPE_DIM, Q_LORA)
    fcol = jnp.concatenate([inv_freq, inv_freq])[:, None]
    wkv = w_ukv[l].reshape(KV_LORA, MLA_HEADS, NOPE_DIM + V_DIM)
    wk_p = jnp.pad(wkv[..., :NOPE_DIM], ((0, 0), (0, 0), (0, HEAD_PAD - NOPE_DIM)))
    wvt = jnp.transpose(wkv[..., NOPE_DIM:], (1, 2, 0)).reshape(MLA_HEADS * V_DIM, KV_LORA)
    tk = min(TOKEN_TILE, seq)

    rw, gt, q5, k5, vt5 = _inproj(
        x2, norm_mix[l][None, :], w_a, mu_a, positions.reshape(t_tok // tk, 1, tk), fcol,
        q_norm[l][None, :], kv_norm[l][None, :], wqt.astype(BF16), wqrt.astype(BF16),
        wk_p.reshape(KV_LORA, -1).astype(BF16), wvt.astype(BF16), batch, seq)

    prm = [p[None, :] for p in (w0[l], a0[l], k_k[l], k_a[l], r_k[l].reshape(-1), ln_w[l], ln_b[l])]
    aup = jnp.concatenate([jnp.zeros_like(a_up[l]), a_up[l]], axis=0)
    yr = _rwkv(rw, prm, _padr(w_up[l], 128).astype(BF16), aup.astype(BF16),
               _padr(g_up[l], ZG_PAD).astype(BF16), batch, seq)

    o_att = _attn(q5, k5, vt5, batch, seq, tk)

    out = _tail(x2, yr, o_att, gt, w_o_rwkv[l].astype(BF16), w_o_mla[l].astype(BF16),
                w_out[l].astype(BF16), norm_ffn[l][None, :], w_ff_up[l].astype(BF16),
                w_ff_down[l].astype(BF16), norm_final[None, :], seq)
    return out.reshape(batch, seq, D_MODEL)
```
